```python
import math
import jax
import jax.numpy as jnp
from jax import lax
import numpy as np

D_MODEL = 4096
BATCH = 2
SEQ = 8192
DEPTH = 1
DEC_BATCH = 16
DEC_SEQ = 16
PAST_LEN = 1024

CHUNK = 64
MIX_WIDTH = D_MODEL
HD_DIFF = 128
H_DIFF = (MIX_WIDTH // 2) // (2 * HD_DIFF)
H_GLA = 4
GLA_DK = (MIX_WIDTH // 4) // H_GLA
GLA_DV = (MIX_WIDTH // 2) // H_GLA
GATE_RANK = 16
GLA_TAU = 16.0
ROPE_THETA = 10000.0
Q_BLOCK = 128
N_KEYS = 128
N_EXPERTS = N_KEYS * N_KEYS
PEER_HEADS = 8
PEER_DKEY = 256
PEER_HALF = PEER_DKEY // 2
PEER_TOPK = 16
PEER_BLOCK = 128
ALPHA = (2 * DEPTH) ** 0.25
BETA = (8 * DEPTH) ** -0.25
EPS = 1e-5
NEG_INF = -1e30

DIFF_QK = H_DIFF * 2 * HD_DIFF
DIFF_V = H_DIFF * 2 * HD_DIFF
GLA_QK = H_GLA * GLA_DK
GLA_V = H_GLA * GLA_DV
SPLIT_SIZES = (DIFF_QK, DIFF_QK, DIFF_V, GLA_QK, GLA_QK, GLA_V, GLA_V, GATE_RANK)
IN_COLS = sum(SPLIT_SIZES)

kernel_name = "hybrid_diffattn_gla_peer_stream_step"


def layernorm(x, g, b):
    xf = x.astype(jnp.float32)
    mu = jnp.mean(xf, axis=-1, keepdims=True)
    var = jnp.mean(jnp.square(xf - mu), axis=-1, keepdims=True)
    return ((xf - mu) * lax.rsqrt(var + EPS)).astype(x.dtype) * g + b


def rmsnorm(x, g):
    xf = x.astype(jnp.float32)
    return (xf * lax.rsqrt(jnp.mean(jnp.square(xf), axis=-1, keepdims=True) + EPS)).astype(x.dtype) * g


def rope(x, pos):
    d = x.shape[-1]
    inv = ROPE_THETA ** (-jnp.arange(0, d, 2, dtype=jnp.float32) / d)
    ang = pos.astype(jnp.float32)[:, None] * inv[None, :]
    ang = jnp.concatenate([ang, ang], axis=-1)
    cos = jnp.cos(ang)[:, None, None, :].astype(x.dtype)
    sin = jnp.sin(ang)[:, None, None, :].astype(x.dtype)
    x1, x2 = jnp.split(x, 2, axis=-1)
    return x * cos + jnp.concatenate([-x2, x1], axis=-1) * sin


def diff_attention(q, qpos, k, v, kpos, lam):
    B, T = q.shape[:2]
    qblk = Q_BLOCK if T % Q_BLOCK == 0 else T
    nblk = T // qblk
    qs = q.reshape(B, nblk, qblk, H_DIFF, 2, HD_DIFF).swapaxes(0, 1)
    qp = qpos.reshape(nblk, qblk)
    kchunk = kpos // CHUNK
    scale = HD_DIFF ** -0.5

    def block(args):
        qb, pb = args
        s = jnp.einsum('bqhid,bkhid->bihqk', qb, k).astype(jnp.float32) * scale
        mask = (pb // CHUNK)[:, None] >= kchunk[None, :]
        p = jax.nn.softmax(jnp.where(mask, s, NEG_INF), axis=-1)
        w = p[:, 0] - lam * p[:, 1]
        return jnp.einsum('bhqk,bkhe->bqhe', w.astype(v.dtype), v)

    o = lax.map(block, (qs, qp))
    return o.swapaxes(0, 1).reshape(B, T, H_DIFF, 2 * HD_DIFF)


def gla_chunked(q, k, v, logf, s0, L):
    B, T, H, dk = q.shape
    dv = v.shape[-1]
    n = T // L
    scale = dk ** -0.5
    causal = jnp.tril(jnp.ones((L, L), dtype=bool))

    def to_chunks(a):
        return a.astype(jnp.float32).reshape(B, n, L, H, a.shape[-1]).transpose(1, 0, 3, 2, 4)

    def step(S, inp):
        qc, kc, vc, gc = inp
        b = jnp.cumsum(gc, axis=-2)
        b_last = b[..., -1:, :]
        q_in = qc * jnp.exp(b) * scale
        k_in = kc * jnp.exp(-b)
        k_end = kc * jnp.exp(b_last - b)
        a = jnp.where(causal, jnp.einsum('bhtk,bhsk->bhts', q_in, k_in), 0.0)
        o = jnp.einsum('bhts,bhsv->bhtv', a, vc) + jnp.einsum('bhtk,bhkv->bhtv', q_in, S)
        S = jnp.exp(b_last).swapaxes(-1, -2) * S + jnp.einsum('bhsk,bhsv->bhkv', k_end, vc)
        return S, o

    S, o = lax.scan(step, s0.astype(jnp.float32), (to_chunks(q), to_chunks(k), to_chunks(v), to_chunks(logf)))
    o = o.transpose(1, 0, 3, 2, 4).reshape(B, T, H, dv)
    return o, S


def peer(x, w_q, keys1, keys2, u_tab, v_tab):
    B, T, D = x.shape
    n = B * T
    blk = min(PEER_BLOCK, n)
    nb = -(-n // blk)
    xt = jnp.pad(x.reshape(n, D), ((0, nb * blk - n), (0, 0))).reshape(nb, blk, D)

    def one(xb):
        q = (xb @ w_q).reshape(blk, PEER_HEADS, 2, PEER_HALF)
        s1 = jnp.einsum('nhd,hkd->nhk', q[:, :, 0], keys1).astype(jnp.float32)
        s2 = jnp.einsum('nhd,hkd->nhk', q[:, :, 1], keys2).astype(jnp.float32)
        t1, i1 = lax.top_k(s1, PEER_TOPK)
        t2, i2 = lax.top_k(s2, PEER_TOPK)
        cand = (t1[..., :, None] + t2[..., None, :]).reshape(blk, PEER_HEADS, PEER_TOPK * PEER_TOPK)
        cidx = (i1[..., :, None] * N_KEYS + i2[..., None, :]).reshape(blk, PEER_HEADS, PEER_TOPK * PEER_TOPK)
        best, sel = lax.top_k(cand, PEER_TOPK)
        idx = jnp.take_along_axis(cidx, sel, axis=-1)
        g = jax.nn.softmax(best, axis=-1)
        u = jnp.take(u_tab, idx, axis=0)
        act = jax.nn.gelu(jnp.einsum('nd,nhkd->nhk', xb, u), approximate=False)
        coeff = (g * act.astype(jnp.float32)).astype(xb.dtype)
        return jnp.einsum('nhk,nhkd->nd', coeff, jnp.take(v_tab, idx, axis=0))

    y = lax.map(one, xt).reshape(nb * blk, D)[:n]
    return y.reshape(B, T, D)


def encoder_layer(x, pos, past_k, past_v, gla_s0, gla_chunk, lam_init,
                  w_in, w_gate2, b_gate, lam_q1, lam_k1, lam_q2, lam_k2,
                  diff_norm_g, gla_norm_g, w_out, ln1_g, ln1_b, ln2_g, ln2_b,
                  peer_wq, peer_keys1, peer_keys2, peer_u, peer_v):
    B, T, _ = x.shape
    proj = x @ w_in
    points = np.cumsum(SPLIT_SIZES)[:-1].tolist()
    dq, dk, dv, gq, gk, gv, gr, gz = jnp.split(proj, points, axis=-1)

    dq = rope(dq.reshape(B, T, H_DIFF, 2, HD_DIFF), pos)
    dk = rope(dk.reshape(B, T, H_DIFF, 2, HD_DIFF), pos)
    dv = dv.reshape(B, T, H_DIFF, 2 * HD_DIFF)
    new_k = dk.reshape(B, T, H_DIFF, 2 * HD_DIFF)
    new_v = dv
    if past_k is None:
        keys, vals, kpos = dk, dv, pos
    else:
        P = past_k.shape[1]
        keys = jnp.concatenate([past_k.reshape(B, P, H_DIFF, 2, HD_DIFF).astype(dk.dtype), dk], axis=1)
        vals = jnp.concatenate([past_v.astype(dv.dtype), dv], axis=1)
        kpos = jnp.arange(P + T, dtype=jnp.int32)
    lam = (jnp.exp(jnp.sum(lam_q1.astype(jnp.float32) * lam_k1.astype(jnp.float32)))
           - jnp.exp(jnp.sum(lam_q2.astype(jnp.float32) * lam_k2.astype(jnp.float32))) + lam_init)
    o_diff = diff_attention(dq, pos, keys, vals, kpos, lam)
    o_diff = rmsnorm(o_diff, diff_norm_g) * (1.0 - lam_init)

    logf = jax.nn.log_sigmoid((gz @ w_gate2 + b_gate).astype(jnp.float32)) / GLA_TAU
    o_gla, s_new = gla_chunked(gq.reshape(B, T, H_GLA, GLA_DK), gk.reshape(B, T, H_GLA, GLA_DK),
                               gv.reshape(B, T, H_GLA, GLA_DV), logf.reshape(B, T, H_GLA, GLA_DK),
                               gla_s0, gla_chunk)
    o_gla = rmsnorm(o_gla.astype(x.dtype), gla_norm_g) * jax.nn.silu(gr.reshape(B, T, H_GLA, GLA_DV))

    mix = jnp.concatenate([o_diff.reshape(B, T, DIFF_V), o_gla.reshape(B, T, GLA_V)], axis=-1) @ w_out
    x1 = layernorm(ALPHA * x + mix, ln1_g, ln1_b)
    x2 = layernorm(ALPHA * x1 + peer(x1, peer_wq, peer_keys1, peer_keys2, peer_u, peer_v), ln2_g, ln2_b)
    return x2, new_k, new_v, s_new.astype(x.dtype)


def setup_inputs(seed: int = 0) -> dict:
    key = jax.random.key(seed)
    ks = jax.random.split(key, 26)
    nrm = jax.random.normal
    f32 = jnp.float32
    return {
        "x_prompt": nrm(ks[0], (BATCH, SEQ, D_MODEL), f32),
        "x_sample": nrm(ks[1], (DEC_BATCH, DEC_SEQ, D_MODEL), f32),
        "cache_diff_k": nrm(ks[2], (DEPTH, DEC_BATCH, PAST_LEN, H_DIFF, 2 * HD_DIFF), f32),
        "cache_diff_v": nrm(ks[3], (DEPTH, DEC_BATCH, PAST_LEN, H_DIFF, 2 * HD_DIFF), f32),
        "state_gla": nrm(ks[4], (DEPTH, DEC_BATCH, H_GLA, GLA_DK, GLA_DV), f32),
        "w_in": nrm(ks[5], (DEPTH, D_MODEL, IN_COLS), f32) * D_MODEL ** -0.5,
        "w_gate2": nrm(ks[6], (DEPTH, GATE_RANK, GLA_QK), f32) * GATE_RANK ** -0.5,
        "b_gate": nrm(ks[7], (DEPTH, GLA_QK), f32) * 0.1,
        "lam_q1": nrm(ks[8], (DEPTH, HD_DIFF), f32) * 0.1,
        "lam_k1": nrm(ks[9], (DEPTH, HD_DIFF), f32) * 0.1,
        "lam_q2": nrm(ks[10], (DEPTH, HD_DIFF), f32) * 0.1,
        "lam_k2": nrm(ks[11], (DEPTH, HD_DIFF), f32) * 0.1,
        "diff_norm_g": 1.0 + 0.01 * nrm(ks[12], (DEPTH, 2 * HD_DIFF), f32),
        "gla_norm_g": 1.0 + 0.01 * nrm(ks[13], (DEPTH, GLA_DV), f32),
        "w_out": nrm(ks[14], (DEPTH, MIX_WIDTH, D_MODEL), f32) * (MIX_WIDTH ** -0.5 * BETA),
        "ln1_g": 1.0 + 0.01 * nrm(ks[15], (DEPTH, D_MODEL), f32),
        "ln1_b": 0.01 * nrm(ks[16], (DEPTH, D_MODEL), f32),
        "ln2_g": 1.0 + 0.01 * nrm(ks[17], (DEPTH, D_MODEL), f32),
        "ln2_b": 0.01 * nrm(ks[18], (DEPTH, D_MODEL), f32),
        "peer_wq": nrm(ks[19], (DEPTH, D_MODEL, PEER_HEADS * PEER_DKEY), f32) * D_MODEL ** -0.5,
        "peer_keys1": nrm(ks[20], (DEPTH, PEER_HEADS, N_KEYS, PEER_HALF), f32) * PEER_HALF ** -0.5,
        "peer_keys2": nrm(ks[21], (DEPTH, PEER_HEADS, N_KEYS, PEER_HALF), f32) * PEER_HALF ** -0.5,
        "peer_u": nrm(ks[22], (DEPTH, N_EXPERTS, D_MODEL), f32) * D_MODEL ** -0.5,
        "peer_v": nrm(ks[23], (DEPTH, N_EXPERTS, D_MODEL), f32) * (BETA * PEER_HEADS ** -0.5),
    }


def reference(x_prompt, x_sample, cache_diff_k, cache_diff_v, state_gla,
              w_in, w_gate2, b_gate, lam_q1, lam_k1, lam_q2, lam_k2,
              diff_norm_g, gla_norm_g, w_out, ln1_g, ln1_b, ln2_g, ln2_b,
              peer_wq, peer_keys1, peer_keys2, peer_u, peer_v):
    Bp, Tp = x_prompt.shape[:2]
    Ts = x_sample.shape[1]
    P = cache_diff_k.shape[2]
    pos_p = jnp.arange(Tp, dtype=jnp.int32)
    pos_s = P + jnp.arange(Ts, dtype=jnp.int32)
    y_p, y_s = x_prompt, x_sample
    kp_list, vp_list, sp_list, ks_list, vs_list, ss_list = [], [], [], [], [], []
    for l in range(DEPTH):
        lam_init = 0.8 - 0.6 * math.exp(-0.3 * l)
        params = (w_in[l], w_gate2[l], b_gate[l], lam_q1[l], lam_k1[l], lam_q2[l], lam_k2[l],
                  diff_norm_g[l], gla_norm_g[l], w_out[l], ln1_g[l], ln1_b[l], ln2_g[l], ln2_b[l],
                  peer_wq[l], peer_keys1[l], peer_keys2[l], peer_u[l], peer_v[l])
        s0 = jnp.zeros((Bp, H_GLA, GLA_DK, GLA_DV), jnp.float32)
        y_p, kp, vp, sp = encoder_layer(y_p, pos_p, None, None, s0, CHUNK, lam_init, *params)
        y_s, k_s, v_s, s_s = encoder_layer(y_s, pos_s, cache_diff_k[l], cache_diff_v[l], state_gla[l],
                                           Ts, lam_init, *params)
        kp_list.append(kp)
        vp_list.append(vp)
        sp_list.append(sp)
        ks_list.append(k_s)
        vs_list.append(v_s)
        ss_list.append(s_s)
    return (y_p, y_s, jnp.stack(kp_list), jnp.stack(vp_list), jnp.stack(sp_list),
            jnp.stack(ks_list), jnp.stack(vs_list), jnp.stack(ss_list))
```

```python
import functools
import math

import jax
import jax.numpy as jnp
from jax import lax
from jax.experimental import pallas as pl
from jax.experimental.pallas import tpu as pltpu

CHUNK = 64
HD_DIFF = 128
H_DIFF = 8
H_GLA = 4
GLA_DK = 256
GLA_DV = 512
GATE_RANK = 16
GLA_TAU = 16.0
ROPE_THETA = 10000.0
N_KEYS = 128
PEER_HEADS = 8
PEER_HALF = 128
PEER_TOPK = 16
EPS = 1e-5
NEG_INF = -1e30

DIFF_W = H_DIFF * 2 * HD_DIFF
GLA_QK = H_GLA * GLA_DK
GLA_V = H_GLA * GLA_DV

LANES = 128
VMEM_CAP = 60000 * 1024

BF16 = jnp.bfloat16
F32 = jnp.float32

NT_DIMS = (((1,), (1,)), ((), ()))
TN_DIMS = (((0,), (0,)), ((), ()))


def _pick(n, cands):
    for c in cands:
        if n % c == 0:
            return c
    return n


def _params(sem, est_bytes):
    limit = int(min(VMEM_CAP, max(32 * 1024 * 1024, est_bytes * 5 // 4)))
    return pltpu.CompilerParams(dimension_semantics=sem, vmem_limit_bytes=limit)


def _mm_kernel(*refs, n_pairs, n_extra, epilogue):
    a_refs = refs[:n_pairs]
    b_refs = refs[n_pairs:2 * n_pairs]
    extras = refs[2 * n_pairs:2 * n_pairs + n_extra]
    outs = refs[2 * n_pairs + n_extra:]
    acc = jnp.dot(a_refs[0][...], b_refs[0][...], preferred_element_type=F32)
    for a, b in zip(a_refs[1:], b_refs[1:]):
        acc = acc + jnp.dot(a[...], b[...], preferred_element_type=F32)
    epilogue(acc, extras, outs)


def _matmul(pairs, epilogue, out_dtypes, row_extras=(), tn=512, name="matmul"):
    m = pairs[0][0].shape[0]
    n = pairs[0][1].shape[1]
    tm = _pick(m, (1024, 512, 256, 128, 64, 32, 16, 8))
    tn = min(tn, n)
    assert n % tn == 0
    in_specs = []
    est = 0
    for a, _ in pairs:
        in_specs.append(pl.BlockSpec((tm, a.shape[1]), lambda i, j: (i, 0)))
        est += 2 * tm * a.shape[1] * a.dtype.itemsize
    for _, b in pairs:
        in_specs.append(pl.BlockSpec((b.shape[0], tn), lambda i, j: (0, j)))
        est += 2 * b.shape[0] * tn * b.dtype.itemsize
    for e in row_extras:
        in_specs.append(pl.BlockSpec((tm, e.shape[1]), lambda i, j: (i, 0)))
        est += 2 * tm * e.shape[1] * 4
    out_specs = [pl.BlockSpec((tm, tn), lambda i, j: (i, j)) for _ in out_dtypes]
    out_shape = [jax.ShapeDtypeStruct((m, n), d) for d in out_dtypes]
    est += sum(2 * tm * tn * jnp.dtype(d).itemsize for d in out_dtypes) + 3 * tm * tn * 4
    kern = functools.partial(_mm_kernel, n_pairs=len(pairs), n_extra=len(row_extras), epilogue=epilogue)
    return pl.pallas_call(
        kern,
        grid=(m // tm, n // tn),
        in_specs=in_specs,
        out_specs=out_specs,
        out_shape=out_shape,
        compiler_params=_params(("parallel", "arbitrary"), est),
        name=name,
    )(*[a for a, _ in pairs], *[b for _, b in pairs], *row_extras)


def _rope_tile(acc, cos, sin_signed):
    outs = []
    for g in range(acc.shape[1] // LANES):
        xg = acc[:, g * LANES:(g + 1) * LANES]
        outs.append(xg * cos + pltpu.roll(xg, LANES // 2, 1) * sin_signed)
    return outs[0] if len(outs) == 1 else jnp.concatenate(outs, axis=1)


def _epi_q(acc, extras, outs):
    r = _rope_tile(acc, extras[0][...], extras[1][...])
    outs[0][...] = (r * (HD_DIFF ** -0.5)).astype(BF16)


def _epi_k(acc, extras, outs):
    r = _rope_tile(acc, extras[0][...], extras[1][...])
    outs[0][...] = r
    outs[1][...] = r.astype(BF16)


def _epi_f32_bf16(acc, extras, outs):
    outs[0][...] = acc
    outs[1][...] = acc.astype(BF16)


def _epi_f32(acc, extras, outs):
    outs[0][...] = acc


def _lambda_value(lam_ref, lam_init):
    lq1 = lam_ref[0:1, :]
    lk1 = lam_ref[1:2, :]
    lq2 = lam_ref[2:3, :]
    lk2 = lam_ref[3:4, :]
    return (jnp.exp(jnp.sum(lq1 * lk1, axis=1, keepdims=True))
            - jnp.exp(jnp.sum(lq2 * lk2, axis=1, keepdims=True)) + lam_init)


def _diff_finalize(o1, o2, lam_ref, g_ref, lam_init):
    lam = _lambda_value(lam_ref, lam_init)
    o = o1 - lam * o2
    ms = jnp.mean(o * o, axis=1, keepdims=True)
    return (o * lax.rsqrt(ms + EPS)) * g_ref[...] * (1.0 - lam_init)


def _attn_prompt_kernel(q_ref, k_ref, v_ref, lam_ref, g_ref, o_ref,
                        m1, l1, a1, m2, l2, a2, *, tq, tk, lam_init):
    i = pl.program_id(2)
    j = pl.program_id(3)

    @pl.when(j == 0)
    def _():
        m1[...] = jnp.full_like(m1, NEG_INF)
        m2[...] = jnp.full_like(m2, NEG_INF)
        l1[...] = jnp.zeros_like(l1)
        l2[...] = jnp.zeros_like(l2)
        a1[...] = jnp.zeros_like(a1)
        a2[...] = jnp.zeros_like(a2)

    def step(masked):
        q = q_ref[...]
        k = k_ref[...]
        v = v_ref[...]
        if masked:
            qc = (i * tq + lax.broadcasted_iota(jnp.int32, (tq, tk), 0)) // CHUNK
            kc = (j * tk + lax.broadcasted_iota(jnp.int32, (tq, tk), 1)) // CHUNK
            mask = qc >= kc
        for half, (m_ref, l_ref, a_ref) in enumerate(((m1, l1, a1), (m2, l2, a2))):
            qh = q[:, half * HD_DIFF:(half + 1) * HD_DIFF]
            kh = k[:, half * HD_DIFF:(half + 1) * HD_DIFF]
            s = lax.dot_general(qh, kh, NT_DIMS, preferred_element_type=F32)
            if masked:
                s = jnp.where(mask, s, NEG_INF)
            m_old = m_ref[...]
            m_new = jnp.maximum(m_old, jnp.max(s, axis=1, keepdims=True))
            alpha = jnp.exp(m_old - m_new)
            p = jnp.exp(s - m_new)
            l_ref[...] = alpha * l_ref[...] + jnp.sum(p, axis=1, keepdims=True)
            a_ref[...] = alpha * a_ref[...] + jnp.dot(p.astype(BF16), v, preferred_element_type=F32)
            m_ref[...] = m_new

    @pl.when((j + 1) * tk <= i * tq)
    def _():
        step(False)

    @pl.when(jnp.logical_and((j + 1) * tk > i * tq, j * tk < (i + 1) * tq))
    def _():
        step(True)

    @pl.when(j == pl.num_programs(3) - 1)
    def _():
        o = _diff_finalize(a1[...] / l1[...], a2[...] / l2[...], lam_ref, g_ref, lam_init)
        o_ref[...] = o.astype(o_ref.dtype)


def _attn_prompt(q, k, v, lam_vecs, g, batch, seq, lam_init):
    tq = _pick(seq, (512, 256, 128, 64))
    tk = tq
    nq = seq // tq
    nk = seq // tk
    hw = 2 * HD_DIFF

    def kv_map(b, h, i, j):
        return (b * nk + jnp.minimum(j, ((i + 1) * tq - 1) // tk), h)

    kern = functools.partial(_attn_prompt_kernel, tq=tq, tk=tk, lam_init=lam_init)
    est = 2 * (tq * hw * 2 * 2 + 2 * tk * hw * 2) + 2 * tq * hw * 4 + 4 * tq * LANES * 4 + 6 * tq * tk * 4
    return pl.pallas_call(
        kern,
        grid=(batch, H_DIFF, nq, nk),
        in_specs=[
            pl.BlockSpec((tq, hw), lambda b, h, i, j: (b * nq + i, h)),
            pl.BlockSpec((tk, hw), kv_map),
            pl.BlockSpec((tk, hw), kv_map),
            pl.BlockSpec((4, HD_DIFF), lambda b, h, i, j: (0, 0)),
            pl.BlockSpec((1, hw), lambda b, h, i, j: (0, 0)),
        ],
        out_specs=pl.BlockSpec((tq, hw), lambda b, h, i, j: (b * nq + i, h)),
        out_shape=jax.ShapeDtypeStruct((batch * seq, DIFF_W), BF16),
        scratch_shapes=[
            pltpu.VMEM((tq, 1), F32), pltpu.VMEM((tq, 1), F32), pltpu.VMEM((tq, hw), F32),
            pltpu.VMEM((tq, 1), F32), pltpu.VMEM((tq, 1), F32), pltpu.VMEM((tq, hw), F32),
        ],
        compiler_params=_params(("parallel", "parallel", "parallel", "arbitrary"), est),
        name="diff_attn_prompt",
    )(q, k, v, lam_vecs, g)


def _attn_cached_kernel(q_ref, kn_ref, vn_ref, kc_ref, vc_ref, lam_ref, g_ref, o_ref, *, past, lam_init):
    q = q_ref[...]
    kn = kn_ref[...]
    vn = vn_ref[...]
    kc = kc_ref[0].astype(BF16)
    vc = vc_ref[0].astype(BF16)
    ts = q.shape[0]
    qchunk_c = (past + lax.broadcasted_iota(jnp.int32, (ts, past), 0)) // CHUNK
    kchunk_c = lax.broadcasted_iota(jnp.int32, (ts, past), 1) // CHUNK
    qchunk_n = (past + lax.broadcasted_iota(jnp.int32, (ts, ts), 0)) // CHUNK
    kchunk_n = (past + lax.broadcasted_iota(jnp.int32, (ts, ts), 1)) // CHUNK
    outs = []
    for half in range(2):
        sl = slice(half * HD_DIFF, (half + 1) * HD_DIFF)
        sc = lax.dot_general(q[:, sl], kc[:, sl], NT_DIMS, preferred_element_type=F32)
        sn = lax.dot_general(q[:, sl], kn[:, sl], NT_DIMS, preferred_element_type=F32)
        sc = jnp.where(qchunk_c >= kchunk_c, sc, NEG_INF)
        sn = jnp.where(qchunk_n >= kchunk_n, sn, NEG_INF)
        m = jnp.maximum(jnp.max(sc, axis=1, keepdims=True), jnp.max(sn, axis=1, keepdims=True))
        pc = jnp.exp(sc - m)
        pn = jnp.exp(sn - m)
        l = jnp.sum(pc, axis=1, keepdims=True) + jnp.sum(pn, axis=1, keepdims=True)
        acc = (jnp.dot(pc.astype(BF16), vc, preferred_element_type=F32)
               + jnp.dot(pn.astype(BF16), vn, preferred_element_type=F32))
        outs.append(acc / l)
    o = _diff_finalize(outs[0], outs[1], lam_ref, g_ref, lam_init)
    o_ref[...] = o.astype(o_ref.dtype)


def _attn_cached(q, kn, vn, cache_k, cache_v, lam_vecs, g, batch, seq, lam_init):
    past = cache_k.shape[1]
    hw = 2 * HD_DIFF
    ck = cache_k.reshape(batch, past, DIFF_W)
    cv = cache_v.reshape(batch, past, DIFF_W)
    kern = functools.partial(_attn_cached_kernel, past=past, lam_init=lam_init)
    est = 2 * 2 * past * hw * 4 + 2 * past * hw * 2 + 8 * seq * past * 4
    return pl.pallas_call(
        kern,
        grid=(batch, H_DIFF),
        in_specs=[
            pl.BlockSpec((seq, hw), lambda b, h: (b, h)),
            pl.BlockSpec((seq, hw), lambda b, h: (b, h)),
            pl.BlockSpec((seq, hw), lambda b, h: (b, h)),
            pl.BlockSpec((1, past, hw), lambda b, h: (b, 0, h)),
            pl.BlockSpec((1, past, hw), lambda b, h: (b, 0, h)),
            pl.BlockSpec((4, HD_DIFF), lambda b, h: (0, 0)),
            pl.BlockSpec((1, hw), lambda b, h: (0, 0)),
        ],
        out_specs=pl.BlockSpec((seq, hw), lambda b, h: (b, h)),
        out_shape=jax.ShapeDtypeStruct((batch * seq, DIFF_W), BF16),
        compiler_params=_params(("parallel", "parallel"), est),
        name="diff_attn_cached",
    )(q, kn, vn, ck, cv, lam_vecs, g)


def _split3(x):
    hi = x.astype(BF16)
    r1 = x - hi.astype(F32)
    mid = r1.astype(BF16)
    lo = (r1 - mid.astype(F32)).astype(BF16)
    return hi, mid, lo


def _gla_kernel(*refs, chunk, n_chunks, has_s0):
    if has_s0:
        q_ref, k_ref, v_ref, r_ref, z_ref, w2_ref, bg_ref, gn_ref, s0_ref, o_ref, s_ref, st = refs
    else:
        q_ref, k_ref, v_ref, r_ref, z_ref, w2_ref, bg_ref, gn_ref, o_ref, s_ref, st = refs
    i = pl.program_id(2)

    @pl.when(i == 0)
    def _():
        if has_s0:
            st[...] = s0_ref[0, 0].T
        else:
            st[...] = jnp.zeros_like(st)

    scale = GLA_DK ** -0.5
    zz = jnp.dot(z_ref[...].astype(BF16), w2_ref[...], preferred_element_type=F32) + bg_ref[...]
    logf = jax.nn.log_sigmoid(zz) / GLA_TAU
    row = lax.broadcasted_iota(jnp.int32, (chunk, chunk), 0)
    col = lax.broadcasted_iota(jnp.int32, (chunk, chunk), 1)
    causal = row >= col
    tri = jnp.where(causal, 1.0, 0.0).astype(BF16)
    for c in range(n_chunks):
        rows = slice(c * chunk, (c + 1) * chunk)
        g = logf[rows]
        bcum = sum(jnp.dot(tri, part, preferred_element_type=F32) for part in _split3(g))
        b_last = bcum[chunk - 1:chunk, :]
        qc = q_ref[rows, :]
        kc = k_ref[rows, :]
        vc = v_ref[rows, :].astype(BF16)
        q_in = (qc * jnp.exp(bcum) * scale).astype(BF16)
        k_in = (kc * jnp.exp(-bcum)).astype(BF16)
        k_end = (kc * jnp.exp(b_last - bcum)).astype(BF16)
        a = lax.dot_general(q_in, k_in, NT_DIMS, preferred_element_type=F32)
        a = jnp.where(causal, a, 0.0)
        s_old = st[...]
        o = (jnp.dot(a.astype(BF16), vc, preferred_element_type=F32)
             + lax.dot_general(q_in, s_old.astype(BF16), NT_DIMS, preferred_element_type=F32))
        st[...] = s_old * jnp.exp(b_last) + lax.dot_general(vc, k_end, TN_DIMS, preferred_element_type=F32)
        ms = jnp.mean(o * o, axis=1, keepdims=True)
        on = (o * lax.rsqrt(ms + EPS)) * gn_ref[...]
        gate = r_ref[rows, :]
        o_ref[rows, :] = (on * (gate * jax.nn.sigmoid(gate))).astype(o_ref.dtype)

    @pl.when(i == pl.num_programs(2) - 1)
    def _():
        s_ref[0, 0] = st[...].T


def _gla(gp, gz, w2, bg, gn, s0, batch, seq, chunk):
    tb = chunk * max(1, min(8, seq // chunk))
    while seq % tb:
        tb -= chunk
    nb = seq // tb
    kq = GLA_QK // GLA_DK
    kv = (2 * GLA_QK) // GLA_DV
    kr = kv + GLA_V // GLA_DV
    in_specs = [
        pl.BlockSpec((tb, GLA_DK), lambda b, h, i: (b * nb + i, h)),
        pl.BlockSpec((tb, GLA_DK), lambda b, h, i: (b * nb + i, kq + h)),
        pl.BlockSpec((tb, GLA_DV), lambda b, h, i: (b * nb + i, kv + h)),
        pl.BlockSpec((tb, GLA_DV), lambda b, h, i: (b * nb + i, kr + h)),
        pl.BlockSpec((tb, LANES), lambda b, h, i: (b * nb + i, 0)),
        pl.BlockSpec((LANES, GLA_DK), lambda b, h, i: (0, h)),
        pl.BlockSpec((1, GLA_DK), lambda b, h, i: (0, h)),
        pl.BlockSpec((1, GLA_DV), lambda b, h, i: (0, 0)),
    ]
    args = [gp, gp, gp, gp, gz, w2, bg, gn]
    if s0 is not None:
        in_specs.append(pl.BlockSpec((1, 1, GLA_DK, GLA_DV), lambda b, h, i: (b, h, 0, 0)))
        args.append(s0)
    kern = functools.partial(_gla_kernel, chunk=chunk, n_chunks=tb // chunk, has_s0=s0 is not None)
    est = 2 * tb * (2 * GLA_DK + 2 * GLA_DV + LANES) * 4 + 2 * tb * GLA_DV * 2 + 7 * GLA_DK * GLA_DV * 4
    return pl.pallas_call(
        kern,
        grid=(batch, H_GLA, nb),
        in_specs=in_specs,
        out_specs=[
            pl.BlockSpec((tb, GLA_DV), lambda b, h, i: (b * nb + i, h)),
            pl.BlockSpec((1, 1, GLA_DK, GLA_DV), lambda b, h, i: (b, h, 0, 0)),
        ],
        out_shape=[
            jax.ShapeDtypeStruct((batch * seq, GLA_V), BF16),
            jax.ShapeDtypeStruct((batch, H_GLA, GLA_DK, GLA_DV), F32),
        ],
        scratch_shapes=[pltpu.VMEM((GLA_DV, GLA_DK), F32)],
        compiler_params=_params(("parallel", "parallel", "arbitrary"), est),
        name="gla",
    )(*args)


def _ln_kernel(x_ref, y_ref, g_ref, b_ref, *outs, alpha):
    h = alpha * x_ref[...] + y_ref[...].astype(F32)
    mu = jnp.mean(h, axis=1, keepdims=True)
    d = h - mu
    var = jnp.mean(d * d, axis=1, keepdims=True)
    r = (d * lax.rsqrt(var + EPS)) * g_ref[...] + b_ref[...]
    for o in outs:
        o[...] = r.astype(o.dtype)


def _res_layernorm(x, y, g, b, alpha, out_dtypes):
    m, d = x.shape
    tm = _pick(m, (128, 64, 32, 16, 8))
    est = 2 * tm * d * (x.dtype.itemsize + y.dtype.itemsize + sum(jnp.dtype(t).itemsize for t in out_dtypes)) \
        + 4 * tm * d * 4
    return pl.pallas_call(
        functools.partial(_ln_kernel, alpha=alpha),
        grid=(m // tm,),
        in_specs=[
            pl.BlockSpec((tm, d), lambda i: (i, 0)),
            pl.BlockSpec((tm, d), lambda i: (i, 0)),
            pl.BlockSpec((1, d), lambda i: (0, 0)),
            pl.BlockSpec((1, d), lambda i: (0, 0)),
        ],
        out_specs=[pl.BlockSpec((tm, d), lambda i: (i, 0)) for _ in out_dtypes],
        out_shape=[jax.ShapeDtypeStruct((m, d), t) for t in out_dtypes],
        compiler_params=_params(("parallel",), est),
        name="res_layernorm",
    )(x, y, g, b)


def _top_values(s, count):
    iota = lax.broadcasted_iota(jnp.int32, s.shape, 0).astype(F32)
    vals = []
    cur = s
    for t in range(count):
        m = jnp.max(cur, axis=0, keepdims=True)
        vals.append(m)
        if t + 1 < count:
            first = jnp.min(jnp.where(cur == m, iota, float(s.shape[0])), axis=0, keepdims=True)
            cur = jnp.where(iota == first, -jnp.inf, cur)
    return vals


def _peer_select_kernel(q_ref, k1_ref, k2_ref, s1_ref, s2_ref, c1_ref, e2_ref, tau_ref, cand_ref):
    for h in range(PEER_HEADS):
        base = h * 2 * PEER_HALF
        q1 = q_ref[:, base:base + PEER_HALF].astype(BF16)
        q2 = q_ref[:, base + PEER_HALF:base + 2 * PEER_HALF].astype(BF16)
        s1 = lax.dot_general(k1_ref[h], q1, NT_DIMS, preferred_element_type=F32)
        s2 = lax.dot_general(k2_ref[h], q2, NT_DIMS, preferred_element_type=F32)
        t1 = _top_values(s1, PEER_TOPK)
        t2 = _top_values(s2, PEER_TOPK)
        pairs = [(i, j) for i in range(PEER_TOPK) for j in range(PEER_TOPK) if (i + 1) * (j + 1) <= PEER_TOPK]
        cand_ref[...] = jnp.full_like(cand_ref, -jnp.inf)
        for r, (i, j) in enumerate(pairs):
            cand_ref[r:r + 1, :] = t1[i] + t2[j]
        best = _top_values(cand_ref[...], PEER_TOPK)
        z = sum(jnp.exp(bv - best[0]) for bv in best)
        s1_ref[h] = s1
        s2_ref[h] = s2
        c1_ref[h] = jnp.exp(s1 - t1[0]) / z
        e2_ref[h] = jnp.exp(s2 - t2[0])
        tau_ref[h:h + 1, :] = best[PEER_TOPK - 1]


def _peer_select(q, keys1, keys2):
    m = q.shape[0]
    tm = _pick(m, (256, 128))
    assert m % tm == 0
    n_pairs = sum((i + 1) * (j + 1) <= PEER_TOPK for i in range(PEER_TOPK) for j in range(PEER_TOPK))
    cand_rows = -(-n_pairs // 8) * 8
    big = jax.ShapeDtypeStruct((PEER_HEADS, N_KEYS, m), F32)
    big_spec = pl.BlockSpec((PEER_HEADS, N_KEYS, tm), lambda i: (0, 0, i))
    est = 2 * tm * q.shape[1] * 4 + 2 * 4 * PEER_HEADS * N_KEYS * tm * 4 + (8 << 20)
    return pl.pallas_call(
        _peer_select_kernel,
        grid=(m // tm,),
        in_specs=[
            pl.BlockSpec((tm, q.shape[1]), lambda i: (i, 0)),
            pl.BlockSpec(keys1.shape, lambda i: (0, 0, 0)),
            pl.BlockSpec(keys2.shape, lambda i: (0, 0, 0)),
        ],
        out_specs=[big_spec, big_spec, big_spec, big_spec, pl.BlockSpec((PEER_HEADS, tm), lambda i: (0, i))],
        out_shape=[big, big, big, big, jax.ShapeDtypeStruct((PEER_HEADS, m), F32)],
        scratch_shapes=[pltpu.VMEM((cand_rows, tm), F32)],
        compiler_params=_params(("parallel",), est),
        name="peer_select",
    )(q, keys1, keys2)


def _peer_mix_kernel(x_ref, u_ref, v_ref, s1_ref, s2_ref, c1_ref, e2_ref, tau_ref, y_ref, *, te):
    j = pl.program_id(1)

    @pl.when(j == 0)
    def _():
        y_ref[...] = jnp.zeros_like(y_ref)

    act = lax.dot_general(u_ref[...], x_ref[...], NT_DIMS, preferred_element_type=F32)
    act = 0.5 * act * (1.0 + lax.erf(act * (2.0 ** -0.5)))
    groups = te // N_KEYS
    coefs = []
    for a_local in range(groups):
        a = j * groups + a_local
        w = None
        for h in range(PEER_HEADS):
            s1a = s1_ref[h, pl.ds(a, 1), :]
            c1a = c1_ref[h, pl.ds(a, 1), :]
            sel = (s1a + s2_ref[h]) >= tau_ref[h:h + 1, :]
            wh = jnp.where(sel, c1a * e2_ref[h], 0.0)
            w = wh if w is None else w + wh
        coefs.append((w * act[a_local * N_KEYS:(a_local + 1) * N_KEYS, :]).astype(BF16))
    coef = coefs[0] if groups == 1 else jnp.concatenate(coefs, axis=0)
    y_ref[...] += lax.dot_general(coef, v_ref[...], TN_DIMS, preferred_element_type=F32)


def _peer_mix(x, u, v, s1, s2, c1, e2, tau):
    m, d = x.shape
    ne = u.shape[0]
    tm = _pick(m, (256, 128))
    te = 256
    assert m % tm == 0 and ne % te == 0 and te % N_KEYS == 0
    big_spec = pl.BlockSpec((PEER_HEADS, N_KEYS, tm), lambda i, j: (0, 0, i))
    est = (2 * tm * d * 2 + 2 * 2 * te * d * 2 + 2 * 4 * PEER_HEADS * N_KEYS * tm * 4
           + 2 * tm * d * 4 + 8 * te * tm * 4)
    return pl.pallas_call(
        functools.partial(_peer_mix_kernel, te=te),
        grid=(m // tm, ne // te),
        in_specs=[
            pl.BlockSpec((tm, d), lambda i, j: (i, 0)),
            pl.BlockSpec((te, d), lambda i, j: (j, 0)),
            pl.BlockSpec((te, d), lambda i, j: (j, 0)),
            big_spec, big_spec, big_spec, big_spec,
            pl.BlockSpec((PEER_HEADS, tm), lambda i, j: (0, i)),
        ],
        out_specs=pl.BlockSpec((tm, d), lambda i, j: (i, 0)),
        out_shape=jax.ShapeDtypeStruct((m, d), F32),
        compiler_params=_params(("parallel", "arbitrary"), est),
        name="peer_mix",
    )(x, u, v, s1, s2, c1, e2, tau)


def _rope_tables(pos, batch):
    inv = ROPE_THETA ** (-jnp.arange(0, HD_DIFF, 2, dtype=F32) / HD_DIFF)
    ang = pos.astype(F32)[:, None] * inv[None, :]
    ang = jnp.concatenate([ang, ang], axis=-1)
    sign = jnp.where(jnp.arange(HD_DIFF) < HD_DIFF // 2, -1.0, 1.0).astype(F32)
    cos = jnp.tile(jnp.cos(ang), (batch, 1))
    sin_signed = jnp.tile(jnp.sin(ang) * sign[None, :], (batch, 1))
    return cos, sin_signed


def _prep_weights(w_in, w_gate2, b_gate, lam_q1, lam_k1, lam_q2, lam_k2, diff_norm_g, gla_norm_g, w_out,
                  ln1_g, ln1_b, ln2_g, ln2_b, peer_wq, peer_keys1, peer_keys2, peer_u, peer_v):
    o_k = DIFF_W
    o_v = 2 * DIFF_W
    o_g = 3 * DIFF_W
    o_z = o_g + 2 * GLA_QK + 2 * GLA_V
    wz = jnp.pad(w_in[:, o_z:o_z + GATE_RANK], ((0, 0), (0, LANES - GATE_RANK)))
    return dict(
        wq=w_in[:, :o_k].astype(BF16),
        wk=w_in[:, o_k:o_v].astype(BF16),
        wv=w_in[:, o_v:o_g].astype(BF16),
        wg=w_in[:, o_g:o_z].astype(BF16),
        wz=wz.astype(BF16),
        w2=jnp.pad(w_gate2, ((0, LANES - GATE_RANK), (0, 0))).astype(BF16),
        bg=b_gate.reshape(1, GLA_QK),
        lam=jnp.stack([lam_q1, lam_k1, lam_q2, lam_k2]),
        dng=diff_norm_g.reshape(1, 2 * HD_DIFF),
        gng=gla_norm_g.reshape(1, GLA_DV),
        wo_a=w_out[:DIFF_W].astype(BF16),
        wo_b=w_out[DIFF_W:].astype(BF16),
        ln1_g=ln1_g.reshape(1, -1), ln1_b=ln1_b.reshape(1, -1),
        ln2_g=ln2_g.reshape(1, -1), ln2_b=ln2_b.reshape(1, -1),
        pwq=peer_wq.astype(BF16),
        pk1=peer_keys1.astype(BF16), pk2=peer_keys2.astype(BF16),
        pu=peer_u.astype(BF16), pv=peer_v.astype(BF16),
    )


def _mixer(x, pos, past_k, past_v, s0, chunk, lam_init, alpha, w):
    batch, seq, d = x.shape
    m = batch * seq
    x2d = x.reshape(m, d)
    xb = x2d.astype(BF16)
    cos, sin_signed = _rope_tables(pos, batch)

    (q,) = _matmul([(xb, w["wq"])], _epi_q, [BF16], (cos, sin_signed), name="proj_q")
    k32, kb = _matmul([(xb, w["wk"])], _epi_k, [F32, BF16], (cos, sin_signed), name="proj_k")
    v32, vb = _matmul([(xb, w["wv"])], _epi_f32_bf16, [F32, BF16], name="proj_v")
    (gp,) = _matmul([(xb, w["wg"])], _epi_f32, [F32], name="proj_gla")
    (gz,) = _matmul([(xb, w["wz"])], _epi_f32, [F32], tn=LANES, name="proj_gate")

    if past_k is None:
        o_diff = _attn_prompt(q, kb, vb, w["lam"], w["dng"], batch, seq, lam_init)
    else:
        o_diff = _attn_cached(q, kb, vb, past_k, past_v, w["lam"], w["dng"], batch, seq, lam_init)
    o_gla, s_new = _gla(gp, gz, w["w2"], w["bg"], w["gng"], s0, batch, seq, chunk)

    (mix,) = _matmul([(o_diff, w["wo_a"]), (o_gla, w["wo_b"])], _epi_f32, [F32], tn=1024, name="out_proj")
    x1, x1b = _res_layernorm(x2d, mix, w["ln1_g"], w["ln1_b"], alpha, [F32, BF16])
    new_k = k32.reshape(batch, seq, H_DIFF, 2 * HD_DIFF)
    new_v = v32.reshape(batch, seq, H_DIFF, 2 * HD_DIFF)
    return x1, x1b, new_k, new_v, s_new


def _peer_ffn(x1, x1b, alpha, w):
    (pq,) = _matmul([(x1b, w["pwq"])], _epi_f32, [F32], name="peer_query")
    s1, s2, c1, e2, tau = _peer_select(pq, w["pk1"], w["pk2"])
    y = _peer_mix(x1b, w["pu"], w["pv"], s1, s2, c1, e2, tau)
    (x2,) = _res_layernorm(x1, y, w["ln2_g"], w["ln2_b"], alpha, [F32])
    return x2


def kernel(x_prompt, x_sample, cache_diff_k, cache_diff_v, state_gla, w_in, w_gate2, b_gate, lam_q1, lam_k1, lam_q2, lam_k2, diff_norm_g, gla_norm_g, w_out, ln1_g, ln1_b, ln2_g, ln2_b, peer_wq, peer_keys1, peer_keys2, peer_u, peer_v):
    depth = w_in.shape[0]
    bp, tp, d = x_prompt.shape
    bs, ts, _ = x_sample.shape
    past = cache_diff_k.shape[2]
    alpha = (2 * depth) ** 0.25
    pos_p = jnp.arange(tp, dtype=jnp.int32)
    pos_s = past + jnp.arange(ts, dtype=jnp.int32)
    y_p, y_s = x_prompt, x_sample
    outs = [[] for _ in range(6)]
    for l in range(depth):
        lam_init = 0.8 - 0.6 * math.exp(-0.3 * l)
        w = _prep_weights(w_in[l], w_gate2[l], b_gate[l], lam_q1[l], lam_k1[l], lam_q2[l], lam_k2[l],
                          diff_norm_g[l], gla_norm_g[l], w_out[l], ln1_g[l], ln1_b[l], ln2_g[l], ln2_b[l],
                          peer_wq[l], peer_keys1[l], peer_keys2[l], peer_u[l], peer_v[l])
        p1, p1b, kp, vp, sp = _mixer(y_p, pos_p, None, None, None, CHUNK, lam_init, alpha, w)
        s1, s1b, k_s, v_s, s_s = _mixer(y_s, pos_s, cache_diff_k[l], cache_diff_v[l], state_gla[l],
                                        ts, lam_init, alpha, w)
        mp = bp * tp
        x2 = _peer_ffn(jnp.concatenate([p1, s1], axis=0), jnp.concatenate([p1b, s1b], axis=0), alpha, w)
        y_p = x2[:mp].reshape(bp, tp, d)
        y_s = x2[mp:].reshape(bs, ts, d)
        for lst, val in zip(outs, (kp, vp, sp, k_s, v_s, s_s)):
            lst.append(val)
    return (y_p, y_s) + tuple(jnp.stack(o) for o in outs)
```

```python
import functools
import math

import jax
import jax.numpy as jnp
from jax import lax
from jax.experimental import pallas as pl
from jax.experimental.pallas import tpu as pltpu

CHUNK = 64
HD_DIFF = 128
H_DIFF = 8
H_GLA = 4
GLA_DK = 256
GLA_DV = 512
GATE_RANK = 16
GLA_TAU = 16.0
ROPE_THETA = 10000.0
N_KEYS = 128
PEER_HEADS = 8
PEER_HALF = 128
PEER_TOPK = 16
EPS = 1e-5
NEG_INF = -1e30

DIFF_W = H_DIFF * 2 * HD_DIFF
GLA_QK = H_GLA * GLA_DK
GLA_V = H_GLA * GLA_DV

LANES = 128
VMEM_CAP = 60000 * 1024

BF16 = jnp.bfloat16
F32 = jnp.float32

NT_DIMS = (((1,), (1,)), ((), ()))
TN_DIMS = (((0,), (0,)), ((), ()))


def _pick(n, cands):
    for c in cands:
        if n % c == 0:
            return c
    return n


def _params(sem, est_bytes):
    limit = int(min(VMEM_CAP, max(32 * 1024 * 1024, est_bytes * 5 // 4)))
    return pltpu.CompilerParams(dimension_semantics=sem, vmem_limit_bytes=limit)


def _mm_kernel(*refs, n_pairs, n_extra, epilogue):
    a_refs = refs[:n_pairs]
    b_refs = refs[n_pairs:2 * n_pairs]
    extras = refs[2 * n_pairs:2 * n_pairs + n_extra]
    outs = refs[2 * n_pairs + n_extra:]
    acc = jnp.dot(a_refs[0][...], b_refs[0][...], preferred_element_type=F32)
    for a, b in zip(a_refs[1:], b_refs[1:]):
        acc = acc + jnp.dot(a[...], b[...], preferred_element_type=F32)
    epilogue(acc, extras, outs)


def _matmul(pairs, epilogue, out_dtypes, row_extras=(), tn=512, name="matmul"):
    m = pairs[0][0].shape[0]
    n = pairs[0][1].shape[1]
    tm = _pick(m, (1024, 512, 256, 128, 64, 32, 16, 8))
    tn = min(tn, n)
    assert n % tn == 0
    in_specs = []
    est = 0
    for a, _ in pairs:
        in_specs.append(pl.BlockSpec((tm, a.shape[1]), lambda i, j: (i, 0)))
        est += 2 * tm * a.shape[1] * a.dtype.itemsize
    for _, b in pairs:
        in_specs.append(pl.BlockSpec((b.shape[0], tn), lambda i, j: (0, j)))
        est += 2 * b.shape[0] * tn * b.dtype.itemsize
    for e in row_extras:
        in_specs.append(pl.BlockSpec((tm, e.shape[1]), lambda i, j: (i, 0)))
        est += 2 * tm * e.shape[1] * 4
    out_specs = [pl.BlockSpec((tm, tn), lambda i, j: (i, j)) for _ in out_dtypes]
    out_shape = [jax.ShapeDtypeStruct((m, n), d) for d in out_dtypes]
    est += sum(2 * tm * tn * jnp.dtype(d).itemsize for d in out_dtypes) + 3 * tm * tn * 4
    kern = functools.partial(_mm_kernel, n_pairs=len(pairs), n_extra=len(row_extras), epilogue=epilogue)
    return pl.pallas_call(
        kern,
        grid=(m // tm, n // tn),
        in_specs=in_specs,
        out_specs=out_specs,
        out_shape=out_shape,
        compiler_params=_params(("parallel", "arbitrary"), est),
        name=name,
    )(*[a for a, _ in pairs], *[b for _, b in pairs], *row_extras)


def _rope_tile(acc, cos, sin_signed):
    outs = []
    for g in range(acc.shape[1] // LANES):
        xg = acc[:, g * LANES:(g + 1) * LANES]
        outs.append(xg * cos + pltpu.roll(xg, LANES // 2, 1) * sin_signed)
    return outs[0] if len(outs) == 1 else jnp.concatenate(outs, axis=1)


def _epi_q(acc, extras, outs):
    r = _rope_tile(acc, extras[0][...], extras[1][...])
    outs[0][...] = (r * (HD_DIFF ** -0.5 * math.log2(math.e))).astype(BF16)


def _epi_k(acc, extras, outs):
    r = _rope_tile(acc, extras[0][...], extras[1][...])
    outs[0][...] = r
    outs[1][...] = r.astype(BF16)


def _epi_f32_bf16(acc, extras, outs):
    outs[0][...] = acc
    outs[1][...] = acc.astype(BF16)


def _epi_f32(acc, extras, outs):
    outs[0][...] = acc


def _lambda_value(lam_ref, lam_init):
    lq1 = lam_ref[0:1, :]
    lk1 = lam_ref[1:2, :]
    lq2 = lam_ref[2:3, :]
    lk2 = lam_ref[3:4, :]
    return (jnp.exp(jnp.sum(lq1 * lk1, axis=1, keepdims=True))
            - jnp.exp(jnp.sum(lq2 * lk2, axis=1, keepdims=True)) + lam_init)


def _diff_finalize(o1, o2, lam_ref, g_ref, lam_init):
    lam = _lambda_value(lam_ref, lam_init)
    o = o1 - lam * o2
    ms = jnp.mean(o * o, axis=1, keepdims=True)
    return (o * lax.rsqrt(ms + EPS)) * g_ref[...] * (1.0 - lam_init)


def _attn_prompt_kernel(qi_ref, kj_ref, q_ref, k_ref, v_ref, lam_ref, g_ref, o_ref,
                        m1, l1, a1, m2, l2, a2, *, tq, lam_init):
    p = pl.program_id(2)
    i = qi_ref[p]
    j = kj_ref[p]
    half_t = tq // 2

    @pl.when(j == 0)
    def _():
        m1[...] = jnp.full_like(m1, NEG_INF)
        m2[...] = jnp.full_like(m2, NEG_INF)
        l1[...] = jnp.zeros_like(l1)
        l2[...] = jnp.zeros_like(l2)
        a1[...] = jnp.zeros_like(a1)
        a2[...] = jnp.zeros_like(a2)

    def update(r0, nr, nc, masked):
        rows = slice(r0, r0 + nr)
        q = q_ref[rows, :]
        k = k_ref[0:nc, :]
        v = v_ref[0:nc, :]
        if masked:
            qc = (r0 + lax.broadcasted_iota(jnp.int32, (nr, nc), 0)) // CHUNK
            kc = lax.broadcasted_iota(jnp.int32, (nr, nc), 1) // CHUNK
            mask = qc >= kc
        for half, (m_ref, l_ref, a_ref) in enumerate(((m1, l1, a1), (m2, l2, a2))):
            qh = q[:, half * HD_DIFF:(half + 1) * HD_DIFF]
            kh = k[:, half * HD_DIFF:(half + 1) * HD_DIFF]
            s = lax.dot_general(qh, kh, NT_DIMS, preferred_element_type=F32)
            if masked:
                s = jnp.where(mask, s, NEG_INF)
            m_old = m_ref[rows, :]
            m_new = jnp.maximum(m_old, jnp.max(s, axis=1, keepdims=True))
            alpha = jnp.exp2(m_old - m_new)
            pr = jnp.exp2(s - m_new)
            l_ref[rows, :] = alpha * l_ref[rows, :] + jnp.sum(pr, axis=1, keepdims=True)
            a_ref[rows, :] = alpha * a_ref[rows, :] + jnp.dot(pr.astype(BF16), v, preferred_element_type=F32)
            m_ref[rows, :] = m_new

    @pl.when(j < i)
    def _():
        update(0, tq, tq, False)

    @pl.when(j == i)
    def _():
        if half_t % CHUNK == 0:
            update(0, half_t, half_t, True)
            update(half_t, half_t, tq, True)
        else:
            update(0, tq, tq, True)
        o = _diff_finalize(a1[...] / l1[...], a2[...] / l2[...], lam_ref, g_ref, lam_init)
        o_ref[...] = o.astype(o_ref.dtype)


def _attn_prompt(q, k, v, lam_vecs, g, batch, seq, lam_init):
    tq = _pick(seq, (1024, 512, 256, 128, 64))
    nq = seq // tq
    hw = 2 * HD_DIFF
    pairs = [(i, j) for i in range(nq) for j in range(i + 1)]
    qi = jnp.asarray([p[0] for p in pairs], jnp.int32)
    kj = jnp.asarray([p[1] for p in pairs], jnp.int32)

    def q_map(b, h, p, qi_ref, kj_ref):
        return (b * nq + qi_ref[p], h)

    def kv_map(b, h, p, qi_ref, kj_ref):
        return (b * nq + kj_ref[p], h)

    kern = functools.partial(_attn_prompt_kernel, tq=tq, lam_init=lam_init)
    est = 2 * 4 * tq * hw * 2 + 2 * tq * hw * 4 + 4 * tq * LANES * 4 + 8 * tq * tq * 4
    return pl.pallas_call(
        kern,
        grid_spec=pltpu.PrefetchScalarGridSpec(
            num_scalar_prefetch=2,
            grid=(batch, H_DIFF, len(pairs)),
            in_specs=[
                pl.BlockSpec((tq, hw), q_map),
                pl.BlockSpec((tq, hw), kv_map),
                pl.BlockSpec((tq, hw), kv_map),
                pl.BlockSpec((4, HD_DIFF), lambda b, h, p, qi_ref, kj_ref: (0, 0)),
                pl.BlockSpec((1, hw), lambda b, h, p, qi_ref, kj_ref: (0, 0)),
            ],
            out_specs=pl.BlockSpec((tq, hw), q_map),
            scratch_shapes=[
                pltpu.VMEM((tq, 1), F32), pltpu.VMEM((tq, 1), F32), pltpu.VMEM((tq, hw), F32),
                pltpu.VMEM((tq, 1), F32), pltpu.VMEM((tq, 1), F32), pltpu.VMEM((tq, hw), F32),
            ],
        ),
        out_shape=jax.ShapeDtypeStruct((batch * seq, DIFF_W), BF16),
        compiler_params=_params(("parallel", "parallel", "arbitrary"), est),
        name="diff_attn_prompt",
    )(qi, kj, q, k, v, lam_vecs, g)


def _attn_cached_kernel(q_ref, kn_ref, vn_ref, kc_ref, vc_ref, lam_ref, g_ref, o_ref, *, past, lam_init):
    q = q_ref[...]
    kn = kn_ref[...]
    vn = vn_ref[...]
    kc = kc_ref[0].astype(BF16)
    vc = vc_ref[0].astype(BF16)
    ts = q.shape[0]
    qchunk_c = (past + lax.broadcasted_iota(jnp.int32, (ts, past), 0)) // CHUNK
    kchunk_c = lax.broadcasted_iota(jnp.int32, (ts, past), 1) // CHUNK
    qchunk_n = (past + lax.broadcasted_iota(jnp.int32, (ts, ts), 0)) // CHUNK
    kchunk_n = (past + lax.broadcasted_iota(jnp.int32, (ts, ts), 1)) // CHUNK
    outs = []
    for half in range(2):
        sl = slice(half * HD_DIFF, (half + 1) * HD_DIFF)
        sc = lax.dot_general(q[:, sl], kc[:, sl], NT_DIMS, preferred_element_type=F32)
        sn = lax.dot_general(q[:, sl], kn[:, sl], NT_DIMS, preferred_element_type=F32)
        sc = jnp.where(qchunk_c >= kchunk_c, sc, NEG_INF)
        sn = jnp.where(qchunk_n >= kchunk_n, sn, NEG_INF)
        m = jnp.maximum(jnp.max(sc, axis=1, keepdims=True), jnp.max(sn, axis=1, keepdims=True))
        pc = jnp.exp2(sc - m)
        pn = jnp.exp2(sn - m)
        l = jnp.sum(pc, axis=1, keepdims=True) + jnp.sum(pn, axis=1, keepdims=True)
        acc = (jnp.dot(pc.astype(BF16), vc, preferred_element_type=F32)
               + jnp.dot(pn.astype(BF16), vn, preferred_element_type=F32))
        outs.append(acc / l)
    o = _diff_finalize(outs[0], outs[1], lam_ref, g_ref, lam_init)
    o_ref[...] = o.astype(o_ref.dtype)


def _attn_cached(q, kn, vn, cache_k, cache_v, lam_vecs, g, batch, seq, lam_init):
    past = cache_k.shape[1]
    hw = 2 * HD_DIFF
    ck = cache_k.reshape(batch, past, DIFF_W)
    cv = cache_v.reshape(batch, past, DIFF_W)
    kern = functools.partial(_attn_cached_kernel, past=past, lam_init=lam_init)
    est = 2 * 2 * past * hw * 4 + 2 * past * hw * 2 + 8 * seq * past * 4
    return pl.pallas_call(
        kern,
        grid=(batch, H_DIFF),
        in_specs=[
            pl.BlockSpec((seq, hw), lambda b, h: (b, h)),
            pl.BlockSpec((seq, hw), lambda b, h: (b, h)),
            pl.BlockSpec((seq, hw), lambda b, h: (b, h)),
            pl.BlockSpec((1, past, hw), lambda b, h: (b, 0, h)),
            pl.BlockSpec((1, past, hw), lambda b, h: (b, 0, h)),
            pl.BlockSpec((4, HD_DIFF), lambda b, h: (0, 0)),
            pl.BlockSpec((1, hw), lambda b, h: (0, 0)),
        ],
        out_specs=pl.BlockSpec((seq, hw), lambda b, h: (b, h)),
        out_shape=jax.ShapeDtypeStruct((batch * seq, DIFF_W), BF16),
        compiler_params=_params(("parallel", "parallel"), est),
        name="diff_attn_cached",
    )(q, kn, vn, ck, cv, lam_vecs, g)


def _split3(x):
    hi = x.astype(BF16)
    r1 = x - hi.astype(F32)
    mid = r1.astype(BF16)
    lo = (r1 - mid.astype(F32)).astype(BF16)
    return hi, mid, lo


def _gla_kernel(*refs, chunk, n_chunks, has_s0):
    if has_s0:
        q_ref, k_ref, v_ref, r_ref, z_ref, w2_ref, bg_ref, gn_ref, s0_ref, o_ref, s_ref, st = refs
    else:
        q_ref, k_ref, v_ref, r_ref, z_ref, w2_ref, bg_ref, gn_ref, o_ref, s_ref, st = refs
    i = pl.program_id(2)

    @pl.when(i == 0)
    def _():
        if has_s0:
            st[...] = s0_ref[0, 0].T
        else:
            st[...] = jnp.zeros_like(st)

    scale = GLA_DK ** -0.5
    zz = jnp.dot(z_ref[...].astype(BF16), w2_ref[...], preferred_element_type=F32) + bg_ref[...]
    logf = jax.nn.log_sigmoid(zz) / GLA_TAU
    row = lax.broadcasted_iota(jnp.int32, (chunk, chunk), 0)
    col = lax.broadcasted_iota(jnp.int32, (chunk, chunk), 1)
    causal = row >= col
    tri = jnp.where(causal, 1.0, 0.0).astype(BF16)
    for c in range(n_chunks):
        rows = slice(c * chunk, (c + 1) * chunk)
        g = logf[rows]
        bcum = sum(jnp.dot(tri, part, preferred_element_type=F32) for part in _split3(g))
        b_last = bcum[chunk - 1:chunk, :]
        qc = q_ref[rows, :]
        kc = k_ref[rows, :]
        vc = v_ref[rows, :].astype(BF16)
        q_in = (qc * jnp.exp(bcum) * scale).astype(BF16)
        k_in = (kc * jnp.exp(-bcum)).astype(BF16)
        k_end = (kc * jnp.exp(b_last - bcum)).astype(BF16)
        a = lax.dot_general(q_in, k_in, NT_DIMS, preferred_element_type=F32)
        a = jnp.where(causal, a, 0.0)
        s_old = st[...]
        o = (jnp.dot(a.astype(BF16), vc, preferred_element_type=F32)
             + lax.dot_general(q_in, s_old.astype(BF16), NT_DIMS, preferred_element_type=F32))
        st[...] = s_old * jnp.exp(b_last) + lax.dot_general(vc, k_end, TN_DIMS, preferred_element_type=F32)
        ms = jnp.mean(o * o, axis=1, keepdims=True)
        on = (o * lax.rsqrt(ms + EPS)) * gn_ref[...]
        gate = r_ref[rows, :]
        o_ref[rows, :] = (on * (gate * jax.nn.sigmoid(gate))).astype(o_ref.dtype)

    @pl.when(i == pl.num_programs(2) - 1)
    def _():
        s_ref[0, 0] = st[...].T


def _gla(gp, gz, w2, bg, gn, s0, batch, seq, chunk):
    tb = chunk * max(1, min(8, seq // chunk))
    while seq % tb:
        tb -= chunk
    nb = seq // tb
    kq = GLA_QK // GLA_DK
    kv = (2 * GLA_QK) // GLA_DV
    kr = kv + GLA_V // GLA_DV
    in_specs = [
        pl.BlockSpec((tb, GLA_DK), lambda b, h, i: (b * nb + i, h)),
        pl.BlockSpec((tb, GLA_DK), lambda b, h, i: (b * nb + i, kq + h)),
        pl.BlockSpec((tb, GLA_DV), lambda b, h, i: (b * nb + i, kv + h)),
        pl.BlockSpec((tb, GLA_DV), lambda b, h, i: (b * nb + i, kr + h)),
        pl.BlockSpec((tb, LANES), lambda b, h, i: (b * nb + i, 0)),
        pl.BlockSpec((LANES, GLA_DK), lambda b, h, i: (0, h)),
        pl.BlockSpec((1, GLA_DK), lambda b, h, i: (0, h)),
        pl.BlockSpec((1, GLA_DV), lambda b, h, i: (0, 0)),
    ]
    args = [gp, gp, gp, gp, gz, w2, bg, gn]
    if s0 is not None:
        in_specs.append(pl.BlockSpec((1, 1, GLA_DK, GLA_DV), lambda b, h, i: (b, h, 0, 0)))
        args.append(s0)
    kern = functools.partial(_gla_kernel, chunk=chunk, n_chunks=tb // chunk, has_s0=s0 is not None)
    est = 2 * tb * (2 * GLA_DK + 2 * GLA_DV + LANES) * 4 + 2 * tb * GLA_DV * 2 + 7 * GLA_DK * GLA_DV * 4
    return pl.pallas_call(
        kern,
        grid=(batch, H_GLA, nb),
        in_specs=in_specs,
        out_specs=[
            pl.BlockSpec((tb, GLA_DV), lambda b, h, i: (b * nb + i, h)),
            pl.BlockSpec((1, 1, GLA_DK, GLA_DV), lambda b, h, i: (b, h, 0, 0)),
        ],
        out_shape=[
            jax.ShapeDtypeStruct((batch * seq, GLA_V), BF16),
            jax.ShapeDtypeStruct((batch, H_GLA, GLA_DK, GLA_DV), F32),
        ],
        scratch_shapes=[pltpu.VMEM((GLA_DV, GLA_DK), F32)],
        compiler_params=_params(("parallel", "parallel", "arbitrary"), est),
        name="gla",
    )(*args)


def _ln_kernel(x_ref, y_ref, g_ref, b_ref, *outs, alpha, n_plain):
    h = alpha * x_ref[...] + y_ref[...].astype(F32)
    mu = jnp.mean(h, axis=1, keepdims=True)
    d = h - mu
    var = jnp.mean(d * d, axis=1, keepdims=True)
    r = (d * lax.rsqrt(var + EPS)) * g_ref[...] + b_ref[...]
    for o in outs[:n_plain]:
        o[...] = r.astype(o.dtype)
    for o in outs[n_plain:]:
        o[...] = r.T.astype(o.dtype)


def _res_layernorm(x, y, g, b, alpha, out_dtypes, transposed_dtypes=()):
    m, d = x.shape
    tm = _pick(m, (128,)) if transposed_dtypes else _pick(m, (128, 64, 32, 16, 8))
    assert m % tm == 0
    sizes = [jnp.dtype(t).itemsize for t in (*out_dtypes, *transposed_dtypes)]
    est = 2 * tm * d * (x.dtype.itemsize + y.dtype.itemsize + sum(sizes)) + 5 * tm * d * 4
    return pl.pallas_call(
        functools.partial(_ln_kernel, alpha=alpha, n_plain=len(out_dtypes)),
        grid=(m // tm,),
        in_specs=[
            pl.BlockSpec((tm, d), lambda i: (i, 0)),
            pl.BlockSpec((tm, d), lambda i: (i, 0)),
            pl.BlockSpec((1, d), lambda i: (0, 0)),
            pl.BlockSpec((1, d), lambda i: (0, 0)),
        ],
        out_specs=[pl.BlockSpec((tm, d), lambda i: (i, 0)) for _ in out_dtypes]
        + [pl.BlockSpec((d, tm), lambda i: (0, i)) for _ in transposed_dtypes],
        out_shape=[jax.ShapeDtypeStruct((m, d), t) for t in out_dtypes]
        + [jax.ShapeDtypeStruct((d, m), t) for t in transposed_dtypes],
        compiler_params=_params(("parallel",), est),
        name="res_layernorm",
    )(x, y, g, b)


def _top_values(s, count):
    iota = lax.broadcasted_iota(jnp.int32, s.shape, 0).astype(F32)
    vals = []
    cur = s
    for t in range(count):
        m = jnp.max(cur, axis=0, keepdims=True)
        vals.append(m)
        if t + 1 < count:
            first = jnp.min(jnp.where(cur == m, iota, float(s.shape[0])), axis=0, keepdims=True)
            cur = jnp.where(iota == first, -jnp.inf, cur)
    return vals


def _peer_select_kernel(q_ref, k1_ref, k2_ref, s1_ref, s2_ref, c1_ref, e2_ref, tau_ref, cand_ref):
    for h in range(PEER_HEADS):
        base = h * 2 * PEER_HALF
        q1 = q_ref[:, base:base + PEER_HALF].astype(BF16)
        q2 = q_ref[:, base + PEER_HALF:base + 2 * PEER_HALF].astype(BF16)
        s1 = lax.dot_general(k1_ref[h], q1, NT_DIMS, preferred_element_type=F32)
        s2 = lax.dot_general(k2_ref[h], q2, NT_DIMS, preferred_element_type=F32)
        t1 = _top_values(s1, PEER_TOPK)
        t2 = _top_values(s2, PEER_TOPK)
        pairs = [(i, j) for i in range(PEER_TOPK) for j in range(PEER_TOPK) if (i + 1) * (j + 1) <= PEER_TOPK]
        cand_ref[...] = jnp.full_like(cand_ref, -jnp.inf)
        for r, (i, j) in enumerate(pairs):
            cand_ref[r:r + 1, :] = t1[i] + t2[j]
        best = _top_values(cand_ref[...], PEER_TOPK)
        z = sum(jnp.exp(bv - best[0]) for bv in best)
        s1_ref[h] = s1
        s2_ref[h] = s2
        c1_ref[h] = jnp.exp(s1 - t1[0]) / z
        e2_ref[h] = jnp.exp(s2 - t2[0])
        tau_ref[h:h + 1, :] = best[PEER_TOPK - 1]


def _peer_select(q, keys1, keys2):
    m = q.shape[0]
    tm = _pick(m, (256, 128))
    assert m % tm == 0
    n_pairs = sum((i + 1) * (j + 1) <= PEER_TOPK for i in range(PEER_TOPK) for j in range(PEER_TOPK))
    cand_rows = -(-n_pairs // 8) * 8
    big = jax.ShapeDtypeStruct((PEER_HEADS, N_KEYS, m), F32)
    big_spec = pl.BlockSpec((PEER_HEADS, N_KEYS, tm), lambda i: (0, 0, i))
    est = 2 * tm * q.shape[1] * 4 + 2 * 4 * PEER_HEADS * N_KEYS * tm * 4 + (8 << 20)
    return pl.pallas_call(
        _peer_select_kernel,
        grid=(m // tm,),
        in_specs=[
            pl.BlockSpec((tm, q.shape[1]), lambda i: (i, 0)),
            pl.BlockSpec(keys1.shape, lambda i: (0, 0, 0)),
            pl.BlockSpec(keys2.shape, lambda i: (0, 0, 0)),
        ],
        out_specs=[big_spec, big_spec, big_spec, big_spec, pl.BlockSpec((PEER_HEADS, tm), lambda i: (0, i))],
        out_shape=[big, big, big, big, jax.ShapeDtypeStruct((PEER_HEADS, m), F32)],
        scratch_shapes=[pltpu.VMEM((cand_rows, tm), F32)],
        compiler_params=_params(("parallel",), est),
        name="peer_select",
    )(q, keys1, keys2)


def _peer_mix_kernel(xt_ref, u_ref, v_ref, s1_ref, s2_ref, c1_ref, e2_ref, tau_ref, y_ref,
                     coef_a, coef_b, act_ref, w_ref, *, te, n_blocks):
    j = pl.program_id(1)
    d, tm = xt_ref.shape
    groups = te // N_KEYS
    jb = jnp.minimum(j, n_blocks - 1)
    pieces = [(g, c) for c in range(tm // LANES) for g in range(groups)]
    col_w = 2 * LANES
    n_vchunks = d // col_w
    half_rows = N_KEYS // 2

    @pl.when(j == 0)
    def _():
        y_ref[...] = jnp.zeros_like(y_ref)
        coef_b[...] = jnp.zeros_like(coef_b)

    def select_weights(g, c):
        a = jb * groups + g
        lanes = slice(c * LANES, (c + 1) * LANES)
        for b0 in range(0, N_KEYS, half_rows):
            bs = slice(b0, b0 + half_rows)
            acc = None
            for h in range(PEER_HEADS):
                s1a = s1_ref[h, pl.ds(a, 1), :][:, lanes]
                c1a = c1_ref[h, pl.ds(a, 1), :][:, lanes]
                sel = (s1a + s2_ref[h, bs, lanes]) >= tau_ref[h:h + 1, lanes]
                wh = jnp.where(sel, c1a * e2_ref[h, bs, lanes], 0.0)
                acc = wh if acc is None else acc + wh
            w_ref[g * N_KEYS + b0:g * N_KEYS + b0 + half_rows, lanes] = acc

    def finish_piece(cur, g, c):
        rows = slice(g * N_KEYS, (g + 1) * N_KEYS)
        lanes = slice(c * LANES, (c + 1) * LANES)
        a = act_ref[rows, lanes]
        a = 0.5 * a * (1.0 + lax.erf(a * (2.0 ** -0.5)))
        cur[lanes, rows] = (a * w_ref[rows, lanes]).T.astype(BF16)

    def step(cur, prev):
        n_split = 2 if tm % (2 * col_w) == 0 else 1
        per_split = -(-len(pieces) // n_split)
        for sp in range(n_split):
            cols = slice(sp * (tm // n_split), (sp + 1) * (tm // n_split))
            for g, c in pieces[sp * per_split:(sp + 1) * per_split]:
                select_weights(g, c)
            act_ref[:, cols] = jnp.dot(u_ref[...], xt_ref[:, cols], preferred_element_type=F32)
        for n in range(n_vchunks):
            cs = slice(n * col_w, (n + 1) * col_w)
            y_ref[:, cs] += jnp.dot(prev[...], v_ref[:, cs], preferred_element_type=F32)
            for g, c in pieces[n::n_vchunks]:
                finish_piece(cur, g, c)

    @pl.when(j % 2 == 0)
    def _():
        step(coef_a, coef_b)

    @pl.when(j % 2 == 1)
    def _():
        step(coef_b, coef_a)


def _peer_mix(xt, u, v, s1, s2, c1, e2, tau):
    d, m = xt.shape
    ne = u.shape[0]
    tm = _pick(m, (512, 256, 128))
    te = 512
    assert m % tm == 0 and ne % te == 0 and te % N_KEYS == 0
    nj = ne // te
    once = pl.Buffered(1)
    big_spec = pl.BlockSpec((PEER_HEADS, N_KEYS, tm), lambda i, j: (0, 0, i), pipeline_mode=once)
    est = (tm * d * 2 + 2 * 2 * te * d * 2 + 4 * PEER_HEADS * N_KEYS * tm * 4
           + 2 * tm * d * 4 + 2 * tm * te * 2 + 10 * te * tm * 4)
    return pl.pallas_call(
        functools.partial(_peer_mix_kernel, te=te, n_blocks=nj),
        grid=(m // tm, nj + 1),
        in_specs=[
            pl.BlockSpec((d, tm), lambda i, j: (0, i), pipeline_mode=once),
            pl.BlockSpec((te, d), lambda i, j: (jnp.minimum(j, nj - 1), 0)),
            pl.BlockSpec((te, d), lambda i, j: (jnp.maximum(j - 1, 0), 0)),
            big_spec, big_spec, big_spec, big_spec,
            pl.BlockSpec((PEER_HEADS, tm), lambda i, j: (0, i), pipeline_mode=once),
        ],
        out_specs=pl.BlockSpec((tm, d), lambda i, j: (i, 0)),
        out_shape=jax.ShapeDtypeStruct((m, d), F32),
        scratch_shapes=[pltpu.VMEM((tm, te), BF16), pltpu.VMEM((tm, te), BF16),
                        pltpu.VMEM((te, tm), F32), pltpu.VMEM((te, tm), F32)],
        compiler_params=_params(("parallel", "arbitrary"), est),
        name="peer_mix",
    )(xt, u, v, s1, s2, c1, e2, tau)


def _rope_tables(pos, batch):
    inv = ROPE_THETA ** (-jnp.arange(0, HD_DIFF, 2, dtype=F32) / HD_DIFF)
    ang = pos.astype(F32)[:, None] * inv[None, :]
    ang = jnp.concatenate([ang, ang], axis=-1)
    sign = jnp.where(jnp.arange(HD_DIFF) < HD_DIFF // 2, -1.0, 1.0).astype(F32)
    cos = jnp.tile(jnp.cos(ang), (batch, 1))
    sin_signed = jnp.tile(jnp.sin(ang) * sign[None, :], (batch, 1))
    return cos, sin_signed


def _prep_weights(w_in, w_gate2, b_gate, lam_q1, lam_k1, lam_q2, lam_k2, diff_norm_g, gla_norm_g, w_out,
                  ln1_g, ln1_b, ln2_g, ln2_b, peer_wq, peer_keys1, peer_keys2, peer_u, peer_v):
    o_k = DIFF_W
    o_v = 2 * DIFF_W
    o_g = 3 * DIFF_W
    o_z = o_g + 2 * GLA_QK + 2 * GLA_V
    wz = jnp.pad(w_in[:, o_z:o_z + GATE_RANK], ((0, 0), (0, LANES - GATE_RANK)))
    return dict(
        wq=w_in[:, :o_k].astype(BF16),
        wk=w_in[:, o_k:o_v].astype(BF16),
        wv=w_in[:, o_v:o_g].astype(BF16),
        wg=w_in[:, o_g:o_z].astype(BF16),
        wz=wz.astype(BF16),
        w2=jnp.pad(w_gate2, ((0, LANES - GATE_RANK), (0, 0))).astype(BF16),
        bg=b_gate.reshape(1, GLA_QK),
        lam=jnp.stack([lam_q1, lam_k1, lam_q2, lam_k2]),
        dng=diff_norm_g.reshape(1, 2 * HD_DIFF),
        gng=gla_norm_g.reshape(1, GLA_DV),
        wo_a=w_out[:DIFF_W].astype(BF16),
        wo_b=w_out[DIFF_W:].astype(BF16),
        ln1_g=ln1_g.reshape(1, -1), ln1_b=ln1_b.reshape(1, -1),
        ln2_g=ln2_g.reshape(1, -1), ln2_b=ln2_b.reshape(1, -1),
        pwq=peer_wq.astype(BF16),
        pk1=peer_keys1.astype(BF16), pk2=peer_keys2.astype(BF16),
        pu=peer_u.astype(BF16), pv=peer_v.astype(BF16),
    )


def _layer(x, pos, past_k, past_v, s0, chunk, lam_init, alpha, w):
    batch, seq, d = x.shape
    m = batch * seq
    x2d = x.reshape(m, d)
    xb = x2d.astype(BF16)
    cos, sin_signed = _rope_tables(pos, batch)

    (q,) = _matmul([(xb, w["wq"])], _epi_q, [BF16], (cos, sin_signed), name="proj_q")
    k32, kb = _matmul([(xb, w["wk"])], _epi_k, [F32, BF16], (cos, sin_signed), name="proj_k")
    v32, vb = _matmul([(xb, w["wv"])], _epi_f32_bf16, [F32, BF16], name="proj_v")
    (gp,) = _matmul([(xb, w["wg"])], _epi_f32, [F32], name="proj_gla")
    (gz,) = _matmul([(xb, w["wz"])], _epi_f32, [F32], tn=LANES, name="proj_gate")

    if past_k is None:
        o_diff = _attn_prompt(q, kb, vb, w["lam"], w["dng"], batch, seq, lam_init)
    else:
        o_diff = _attn_cached(q, kb, vb, past_k, past_v, w["lam"], w["dng"], batch, seq, lam_init)
    o_gla, s_new = _gla(gp, gz, w["w2"], w["bg"], w["gng"], s0, batch, seq, chunk)

    (mix,) = _matmul([(o_diff, w["wo_a"]), (o_gla, w["wo_b"])], _epi_f32, [F32], tn=1024, name="out_proj")
    x1, x1b, x1t = _res_layernorm(x2d, mix, w["ln1_g"], w["ln1_b"], alpha, [F32, BF16], [BF16])
    new_k = k32.reshape(batch, seq, H_DIFF, 2 * HD_DIFF)
    new_v = v32.reshape(batch, seq, H_DIFF, 2 * HD_DIFF)

    (pq,) = _matmul([(x1b, w["pwq"])], _epi_f32, [F32], name="peer_query")
    s1, s2, c1, e2, tau = _peer_select(pq, w["pk1"], w["pk2"])
    y = _peer_mix(x1t, w["pu"], w["pv"], s1, s2, c1, e2, tau)
    (x2,) = _res_layernorm(x1, y, w["ln2_g"], w["ln2_b"], alpha, [F32])
    return x2.reshape(batch, seq, d), new_k, new_v, s_new


def kernel(x_prompt, x_sample, cache_diff_k, cache_diff_v, state_gla, w_in, w_gate2, b_gate, lam_q1, lam_k1, lam_q2, lam_k2, diff_norm_g, gla_norm_g, w_out, ln1_g, ln1_b, ln2_g, ln2_b, peer_wq, peer_keys1, peer_keys2, peer_u, peer_v):
    depth = w_in.shape[0]
    tp = x_prompt.shape[1]
    ts = x_sample.shape[1]
    past = cache_diff_k.shape[2]
    alpha = (2 * depth) ** 0.25
    pos_p = jnp.arange(tp, dtype=jnp.int32)
    pos_s = past + jnp.arange(ts, dtype=jnp.int32)
    y_p, y_s = x_prompt, x_sample
    outs = [[] for _ in range(6)]
    for l in range(depth):
        lam_init = 0.8 - 0.6 * math.exp(-0.3 * l)
        w = _prep_weights(w_in[l], w_gate2[l], b_gate[l], lam_q1[l], lam_k1[l], lam_q2[l], lam_k2[l],
                          diff_norm_g[l], gla_norm_g[l], w_out[l], ln1_g[l], ln1_b[l], ln2_g[l], ln2_b[l],
                          peer_wq[l], peer_keys1[l], peer_keys2[l], peer_u[l], peer_v[l])
        y_p, kp, vp, sp = _layer(y_p, pos_p, None, None, None, CHUNK, lam_init, alpha, w)
        y_s, k_s, v_s, s_s = _layer(y_s, pos_s, cache_diff_k[l], cache_diff_v[l], state_gla[l],
                                    ts, lam_init, alpha, w)
        for lst, val in zip(outs, (kp, vp, sp, k_s, v_s, s_s)):
            lst.append(val)
    return (y_p, y_s) + tuple(jnp.stack(o) for o in outs)
```

```python
import functools
import math

import jax
import jax.numpy as jnp
from jax import lax
from jax.experimental import pallas as pl
from jax.experimental.pallas import tpu as pltpu

CHUNK = 64
HD_DIFF = 128
H_DIFF = 8
H_GLA = 4
GLA_DK = 256
GLA_DV = 512
GATE_RANK = 16
GLA_TAU = 16.0
ROPE_THETA = 10000.0
N_KEYS = 128
PEER_HEADS = 8
PEER_HALF = 128
PEER_TOPK = 16
EPS = 1e-5
NEG_INF = -1e30
ROW_BLOCK = 256

DIFF_W = H_DIFF * 2 * HD_DIFF
GLA_QK = H_GLA * GLA_DK
GLA_V = H_GLA * GLA_DV

LANES = 128
SUBLANES = 8
VMEM_CAP = 60000 * 1024

BF16 = jnp.bfloat16
F32 = jnp.float32

NT_DIMS = (((1,), (1,)), ((), ()))
TN_DIMS = (((0,), (0,)), ((), ()))


def _pick(n, cands):
    for c in cands:
        if n % c == 0:
            return c
    return n


def _params(sem, est_bytes, flags=None):
    limit = int(min(VMEM_CAP, max(32 * 1024 * 1024, est_bytes * 5 // 4)))
    return pltpu.CompilerParams(dimension_semantics=sem, vmem_limit_bytes=limit, flags=flags)


def _mm_kernel(*refs, n_pairs, n_extra, epilogue):
    a_refs = refs[:n_pairs]
    b_refs = refs[n_pairs:2 * n_pairs]
    extras = refs[2 * n_pairs:2 * n_pairs + n_extra]
    outs = refs[2 * n_pairs + n_extra:]
    acc = jnp.dot(a_refs[0][...], b_refs[0][...], preferred_element_type=F32)
    for a, b in zip(a_refs[1:], b_refs[1:]):
        acc = acc + jnp.dot(a[...], b[...], preferred_element_type=F32)
    epilogue(acc, extras, outs)


def _matmul(pairs, epilogue, out_dtypes, row_extras=(), tn=512, name="matmul"):
    m = pairs[0][0].shape[0]
    n = pairs[0][1].shape[1]
    tm = _pick(m, (1024, 512, 256, 128, 64, 32, 16, 8))
    tn = min(tn, n)
    assert n % tn == 0
    in_specs = []
    est = 0
    for a, _ in pairs:
        in_specs.append(pl.BlockSpec((tm, a.shape[1]), lambda i, j: (i, 0)))
        est += 2 * tm * a.shape[1] * a.dtype.itemsize
    for _, b in pairs:
        in_specs.append(pl.BlockSpec((b.shape[0], tn), lambda i, j: (0, j)))
        est += 2 * b.shape[0] * tn * b.dtype.itemsize
    for e in row_extras:
        in_specs.append(pl.BlockSpec((tm, e.shape[1]), lambda i, j: (i, 0)))
        est += 2 * tm * e.shape[1] * 4
    out_specs = [pl.BlockSpec((tm, tn), lambda i, j: (i, j)) for _ in out_dtypes]
    out_shape = [jax.ShapeDtypeStruct((m, n), d) for d in out_dtypes]
    est += sum(2 * tm * tn * jnp.dtype(d).itemsize for d in out_dtypes) + 3 * tm * tn * 4
    kern = functools.partial(_mm_kernel, n_pairs=len(pairs), n_extra=len(row_extras), epilogue=epilogue)
    return pl.pallas_call(
        kern,
        grid=(m // tm, n // tn),
        in_specs=in_specs,
        out_specs=out_specs,
        out_shape=out_shape,
        compiler_params=_params(("parallel", "arbitrary"), est),
        name=name,
    )(*[a for a, _ in pairs], *[b for _, b in pairs], *row_extras)


def _rope_tile(acc, cos, sin_signed):
    outs = []
    for g in range(acc.shape[1] // LANES):
        xg = acc[:, g * LANES:(g + 1) * LANES]
        outs.append(xg * cos + pltpu.roll(xg, LANES // 2, 1) * sin_signed)
    return outs[0] if len(outs) == 1 else jnp.concatenate(outs, axis=1)


def _epi_q(acc, extras, outs):
    r = _rope_tile(acc, extras[0][...], extras[1][...])
    outs[0][...] = (r * (HD_DIFF ** -0.5 * math.log2(math.e))).astype(BF16)


def _epi_k(acc, extras, outs):
    r = _rope_tile(acc, extras[0][...], extras[1][...])
    outs[0][...] = r
    outs[1][...] = r.astype(BF16)


def _epi_f32_bf16(acc, extras, outs):
    outs[0][...] = acc
    outs[1][...] = acc.astype(BF16)


def _epi_f32(acc, extras, outs):
    outs[0][...] = acc


def _lambda_value(lam_ref, lam_init):
    lq1 = lam_ref[0:1, :]
    lk1 = lam_ref[1:2, :]
    lq2 = lam_ref[2:3, :]
    lk2 = lam_ref[3:4, :]
    return (jnp.exp(jnp.sum(lq1 * lk1, axis=1, keepdims=True))
            - jnp.exp(jnp.sum(lq2 * lk2, axis=1, keepdims=True)) + lam_init)


def _diff_finalize(o1, o2, lam_ref, g_ref, lam_init):
    lam = _lambda_value(lam_ref, lam_init)
    o = o1 - lam * o2
    ms = jnp.mean(o * o, axis=1, keepdims=True)
    return (o * lax.rsqrt(ms + EPS)) * g_ref[...] * (1.0 - lam_init)


def _attn_prompt_kernel(qi_ref, kj_ref, q_ref, k_ref, v_ref, lam_ref, g_ref, o_ref,
                        m1, l1, a1, m2, l2, a2, *, tq, lam_init):
    p = pl.program_id(2)
    i = qi_ref[p]
    j = kj_ref[p]
    half_t = tq // 2

    @pl.when(j == 0)
    def _():
        m1[...] = jnp.full_like(m1, NEG_INF)
        m2[...] = jnp.full_like(m2, NEG_INF)
        l1[...] = jnp.zeros_like(l1)
        l2[...] = jnp.zeros_like(l2)
        a1[...] = jnp.zeros_like(a1)
        a2[...] = jnp.zeros_like(a2)

    def update(r0, nr, nc, masked):
        if nr > ROW_BLOCK:
            for rr in range(r0, r0 + nr, ROW_BLOCK):
                update(rr, ROW_BLOCK, nc, masked)
            return
        rows = slice(r0, r0 + nr)
        q = q_ref[rows, :]
        k = k_ref[0:nc, :]
        v = v_ref[0:nc, :]
        if masked:
            qc = (r0 + lax.broadcasted_iota(jnp.int32, (nr, nc), 0)) // CHUNK
            kc = lax.broadcasted_iota(jnp.int32, (nr, nc), 1) // CHUNK
            mask = qc >= kc
        for half, (m_ref, l_ref, a_ref) in enumerate(((m1, l1, a1), (m2, l2, a2))):
            qh = q[:, half * HD_DIFF:(half + 1) * HD_DIFF]
            kh = k[:, half * HD_DIFF:(half + 1) * HD_DIFF]
            s = lax.dot_general(qh, kh, NT_DIMS, preferred_element_type=F32)
            if masked:
                s = jnp.where(mask, s, NEG_INF)
            m_old = m_ref[rows, :]
            m_new = jnp.maximum(m_old, jnp.max(s, axis=1, keepdims=True))
            alpha = jnp.exp2(m_old - m_new)
            pr = jnp.exp2(s - m_new)
            l_ref[rows, :] = alpha * l_ref[rows, :] + jnp.sum(pr, axis=1, keepdims=True)
            a_ref[rows, :] = alpha * a_ref[rows, :] + jnp.dot(pr.astype(BF16), v, preferred_element_type=F32)
            m_ref[rows, :] = m_new

    @pl.when(j < i)
    def _():
        update(0, tq, tq, False)

    @pl.when(j == i)
    def _():
        if half_t % CHUNK == 0:
            update(0, half_t, half_t, True)
            update(half_t, half_t, tq, True)
        else:
            update(0, tq, tq, True)
        o = _diff_finalize(a1[...] / l1[...], a2[...] / l2[...], lam_ref, g_ref, lam_init)
        o_ref[...] = o.astype(o_ref.dtype)


def _attn_prompt(q, k, v, lam_vecs, g, batch, seq, lam_init):
    tq = _pick(seq, (1024, 512, 256, 128, 64))
    nq = seq // tq
    hw = 2 * HD_DIFF
    pairs = [(i, j) for i in range(nq) for j in range(i + 1)]
    qi = jnp.asarray([p[0] for p in pairs], jnp.int32)
    kj = jnp.asarray([p[1] for p in pairs], jnp.int32)

    def q_map(b, h, p, qi_ref, kj_ref):
        return (b * nq + qi_ref[p], h)

    def kv_map(b, h, p, qi_ref, kj_ref):
        return (b * nq + kj_ref[p], h)

    kern = functools.partial(_attn_prompt_kernel, tq=tq, lam_init=lam_init)
    est = 2 * 4 * tq * hw * 2 + 2 * tq * hw * 4 + 4 * tq * LANES * 4 + 8 * tq * tq * 4
    return pl.pallas_call(
        kern,
        grid_spec=pltpu.PrefetchScalarGridSpec(
            num_scalar_prefetch=2,
            grid=(batch, H_DIFF, len(pairs)),
            in_specs=[
                pl.BlockSpec((tq, hw), q_map),
                pl.BlockSpec((tq, hw), kv_map),
                pl.BlockSpec((tq, hw), kv_map),
                pl.BlockSpec((4, HD_DIFF), lambda b, h, p, qi_ref, kj_ref: (0, 0)),
                pl.BlockSpec((1, hw), lambda b, h, p, qi_ref, kj_ref: (0, 0)),
            ],
            out_specs=pl.BlockSpec((tq, hw), q_map),
            scratch_shapes=[
                pltpu.VMEM((tq, 1), F32), pltpu.VMEM((tq, 1), F32), pltpu.VMEM((tq, hw), F32),
                pltpu.VMEM((tq, 1), F32), pltpu.VMEM((tq, 1), F32), pltpu.VMEM((tq, hw), F32),
            ],
        ),
        out_shape=jax.ShapeDtypeStruct((batch * seq, DIFF_W), BF16),
        compiler_params=_params(("parallel", "parallel", "arbitrary"), est),
        name="diff_attn_prompt",
    )(qi, kj, q, k, v, lam_vecs, g)


def _attn_cached_kernel(q_ref, kn_ref, vn_ref, kc_ref, vc_ref, lam_ref, g_ref, o_ref, *, past, lam_init):
    q = q_ref[...]
    kn = kn_ref[...]
    vn = vn_ref[...]
    kc = kc_ref[0].astype(BF16)
    vc = vc_ref[0].astype(BF16)
    ts = q.shape[0]
    qchunk_c = (past + lax.broadcasted_iota(jnp.int32, (ts, past), 0)) // CHUNK
    kchunk_c = lax.broadcasted_iota(jnp.int32, (ts, past), 1) // CHUNK
    qchunk_n = (past + lax.broadcasted_iota(jnp.int32, (ts, ts), 0)) // CHUNK
    kchunk_n = (past + lax.broadcasted_iota(jnp.int32, (ts, ts), 1)) // CHUNK
    outs = []
    for half in range(2):
        sl = slice(half * HD_DIFF, (half + 1) * HD_DIFF)
        sc = lax.dot_general(q[:, sl], kc[:, sl], NT_DIMS, preferred_element_type=F32)
        sn = lax.dot_general(q[:, sl], kn[:, sl], NT_DIMS, preferred_element_type=F32)
        sc = jnp.where(qchunk_c >= kchunk_c, sc, NEG_INF)
        sn = jnp.where(qchunk_n >= kchunk_n, sn, NEG_INF)
        m = jnp.maximum(jnp.max(sc, axis=1, keepdims=True), jnp.max(sn, axis=1, keepdims=True))
        pc = jnp.exp2(sc - m)
        pn = jnp.exp2(sn - m)
        l = jnp.sum(pc, axis=1, keepdims=True) + jnp.sum(pn, axis=1, keepdims=True)
        acc = (jnp.dot(pc.astype(BF16), vc, preferred_element_type=F32)
               + jnp.dot(pn.astype(BF16), vn, preferred_element_type=F32))
        outs.append(acc / l)
    o = _diff_finalize(outs[0], outs[1], lam_ref, g_ref, lam_init)
    o_ref[...] = o.astype(o_ref.dtype)


def _attn_cached(q, kn, vn, cache_k, cache_v, lam_vecs, g, batch, seq, lam_init):
    past = cache_k.shape[1]
    hw = 2 * HD_DIFF
    ck = cache_k.reshape(batch, past, DIFF_W)
    cv = cache_v.reshape(batch, past, DIFF_W)
    kern = functools.partial(_attn_cached_kernel, past=past, lam_init=lam_init)
    est = 2 * 2 * past * hw * 4 + 2 * past * hw * 2 + 8 * seq * past * 4
    return pl.pallas_call(
        kern,
        grid=(batch, H_DIFF),
        in_specs=[
            pl.BlockSpec((seq, hw), lambda b, h: (b, h)),
            pl.BlockSpec((seq, hw), lambda b, h: (b, h)),
            pl.BlockSpec((seq, hw), lambda b, h: (b, h)),
            pl.BlockSpec((1, past, hw), lambda b, h: (b, 0, h)),
            pl.BlockSpec((1, past, hw), lambda b, h: (b, 0, h)),
            pl.BlockSpec((4, HD_DIFF), lambda b, h: (0, 0)),
            pl.BlockSpec((1, hw), lambda b, h: (0, 0)),
        ],
        out_specs=pl.BlockSpec((seq, hw), lambda b, h: (b, h)),
        out_shape=jax.ShapeDtypeStruct((batch * seq, DIFF_W), BF16),
        compiler_params=_params(("parallel", "parallel"), est),
        name="diff_attn_cached",
    )(q, kn, vn, ck, cv, lam_vecs, g)


def _split3(x):
    hi = x.astype(BF16)
    r1 = x - hi.astype(F32)
    mid = r1.astype(BF16)
    lo = (r1 - mid.astype(F32)).astype(BF16)
    return hi, mid, lo


def _gla_kernel(*refs, chunk, n_chunks, has_s0):
    if has_s0:
        q_ref, k_ref, v_ref, r_ref, z_ref, w2_ref, bg_ref, gn_ref, s0_ref, o_ref, s_ref, st = refs
    else:
        q_ref, k_ref, v_ref, r_ref, z_ref, w2_ref, bg_ref, gn_ref, o_ref, s_ref, st = refs
    i = pl.program_id(2)

    @pl.when(i == 0)
    def _():
        if has_s0:
            st[...] = s0_ref[0, 0].T
        else:
            st[...] = jnp.zeros_like(st)

    scale = GLA_DK ** -0.5
    zz = jnp.dot(z_ref[...].astype(BF16), w2_ref[...], preferred_element_type=F32) + bg_ref[...]
    logf = jax.nn.log_sigmoid(zz) / GLA_TAU
    row = lax.broadcasted_iota(jnp.int32, (chunk, chunk), 0)
    col = lax.broadcasted_iota(jnp.int32, (chunk, chunk), 1)
    causal = row >= col
    tri = jnp.where(causal, 1.0, 0.0).astype(BF16)
    for c in range(n_chunks):
        rows = slice(c * chunk, (c + 1) * chunk)
        g = logf[rows]
        bcum = sum(jnp.dot(tri, part, preferred_element_type=F32) for part in _split3(g))
        b_last = bcum[chunk - 1:chunk, :]
        qc = q_ref[rows, :]
        kc = k_ref[rows, :]
        vc = v_ref[rows, :].astype(BF16)
        q_in = (qc * jnp.exp(bcum) * scale).astype(BF16)
        k_in = (kc * jnp.exp(-bcum)).astype(BF16)
        k_end = (kc * jnp.exp(b_last - bcum)).astype(BF16)
        a = lax.dot_general(q_in, k_in, NT_DIMS, preferred_element_type=F32)
        a = jnp.where(causal, a, 0.0)
        s_old = st[...]
        o = (jnp.dot(a.astype(BF16), vc, preferred_element_type=F32)
             + lax.dot_general(q_in, s_old.astype(BF16), NT_DIMS, preferred_element_type=F32))
        st[...] = s_old * jnp.exp(b_last) + lax.dot_general(vc, k_end, TN_DIMS, preferred_element_type=F32)
        ms = jnp.mean(o * o, axis=1, keepdims=True)
        on = (o * lax.rsqrt(ms + EPS)) * gn_ref[...]
        gate = r_ref[rows, :]
        o_ref[rows, :] = (on * (gate * jax.nn.sigmoid(gate))).astype(o_ref.dtype)

    @pl.when(i == pl.num_programs(2) - 1)
    def _():
        s_ref[0, 0] = st[...].T


def _gla(gp, gz, w2, bg, gn, s0, batch, seq, chunk):
    tb = chunk * max(1, min(8, seq // chunk))
    while seq % tb:
        tb -= chunk
    nb = seq // tb
    kq = GLA_QK // GLA_DK
    kv = (2 * GLA_QK) // GLA_DV
    kr = kv + GLA_V // GLA_DV
    in_specs = [
        pl.BlockSpec((tb, GLA_DK), lambda b, h, i: (b * nb + i, h)),
        pl.BlockSpec((tb, GLA_DK), lambda b, h, i: (b * nb + i, kq + h)),
        pl.BlockSpec((tb, GLA_DV), lambda b, h, i: (b * nb + i, kv + h)),
        pl.BlockSpec((tb, GLA_DV), lambda b, h, i: (b * nb + i, kr + h)),
        pl.BlockSpec((tb, LANES), lambda b, h, i: (b * nb + i, 0)),
        pl.BlockSpec((LANES, GLA_DK), lambda b, h, i: (0, h)),
        pl.BlockSpec((1, GLA_DK), lambda b, h, i: (0, h)),
        pl.BlockSpec((1, GLA_DV), lambda b, h, i: (0, 0)),
    ]
    args = [gp, gp, gp, gp, gz, w2, bg, gn]
    if s0 is not None:
        in_specs.append(pl.BlockSpec((1, 1, GLA_DK, GLA_DV), lambda b, h, i: (b, h, 0, 0)))
        args.append(s0)
    kern = functools.partial(_gla_kernel, chunk=chunk, n_chunks=tb // chunk, has_s0=s0 is not None)
    est = 2 * tb * (2 * GLA_DK + 2 * GLA_DV + LANES) * 4 + 2 * tb * GLA_DV * 2 + 7 * GLA_DK * GLA_DV * 4
    return pl.pallas_call(
        kern,
        grid=(batch, H_GLA, nb),
        in_specs=in_specs,
        out_specs=[
            pl.BlockSpec((tb, GLA_DV), lambda b, h, i: (b * nb + i, h)),
            pl.BlockSpec((1, 1, GLA_DK, GLA_DV), lambda b, h, i: (b, h, 0, 0)),
        ],
        out_shape=[
            jax.ShapeDtypeStruct((batch * seq, GLA_V), BF16),
            jax.ShapeDtypeStruct((batch, H_GLA, GLA_DK, GLA_DV), F32),
        ],
        scratch_shapes=[pltpu.VMEM((GLA_DV, GLA_DK), F32)],
        compiler_params=_params(("parallel", "parallel", "arbitrary"), est),
        name="gla",
    )(*args)


def _ln_kernel(x_ref, y_ref, g_ref, b_ref, *outs, alpha, n_plain):
    h = alpha * x_ref[...] + y_ref[...].astype(F32)
    mu = jnp.mean(h, axis=1, keepdims=True)
    d = h - mu
    var = jnp.mean(d * d, axis=1, keepdims=True)
    r = (d * lax.rsqrt(var + EPS)) * g_ref[...] + b_ref[...]
    for o in outs[:n_plain]:
        o[...] = r.astype(o.dtype)
    for o in outs[n_plain:]:
        o[...] = r.T.astype(o.dtype)


def _res_layernorm(x, y, g, b, alpha, out_dtypes, transposed_dtypes=()):
    m, d = x.shape
    tm = _pick(m, (128,)) if transposed_dtypes else _pick(m, (128, 64, 32, 16, 8))
    assert m % tm == 0
    sizes = [jnp.dtype(t).itemsize for t in (*out_dtypes, *transposed_dtypes)]
    est = 2 * tm * d * (x.dtype.itemsize + y.dtype.itemsize + sum(sizes)) + 5 * tm * d * 4
    return pl.pallas_call(
        functools.partial(_ln_kernel, alpha=alpha, n_plain=len(out_dtypes)),
        grid=(m // tm,),
        in_specs=[
            pl.BlockSpec((tm, d), lambda i: (i, 0)),
            pl.BlockSpec((tm, d), lambda i: (i, 0)),
            pl.BlockSpec((1, d), lambda i: (0, 0)),
            pl.BlockSpec((1, d), lambda i: (0, 0)),
        ],
        out_specs=[pl.BlockSpec((tm, d), lambda i: (i, 0)) for _ in out_dtypes]
        + [pl.BlockSpec((d, tm), lambda i: (0, i)) for _ in transposed_dtypes],
        out_shape=[jax.ShapeDtypeStruct((m, d), t) for t in out_dtypes]
        + [jax.ShapeDtypeStruct((d, m), t) for t in transposed_dtypes],
        compiler_params=_params(("parallel",), est),
        name="res_layernorm",
    )(x, y, g, b)


def _top_values(s, count):
    vals = []
    cur = s
    for t in range(count):
        m = jnp.max(cur, axis=0, keepdims=True)
        vals.append(m)
        if t + 1 < count:
            cur = jnp.where(cur == m, -jnp.inf, cur)
    return vals


MIX_FLAGS = None
NOT_RANKED = float(N_KEYS - 1)


def _peer_select_kernel(q_ref, k1_ref, k2_ref, cnt_ref, c1_ref, r2_ref, e2_ref, cand_ref):
    pairs = [(i, j) for i in range(PEER_TOPK) for j in range(PEER_TOPK) if (i + 1) * (j + 1) <= PEER_TOPK]
    for h in range(PEER_HEADS):
        base = h * 2 * PEER_HALF
        q1 = q_ref[:, base:base + PEER_HALF].astype(BF16)
        q2 = q_ref[:, base + PEER_HALF:base + 2 * PEER_HALF].astype(BF16)
        s1 = lax.dot_general(k1_ref[h], q1, NT_DIMS, preferred_element_type=F32)
        s2 = lax.dot_general(k2_ref[h], q2, NT_DIMS, preferred_element_type=F32)
        t1 = _top_values(s1, PEER_TOPK)
        t2 = _top_values(s2, PEER_TOPK)
        cand_ref[...] = jnp.full_like(cand_ref, -jnp.inf)
        for r, (i, j) in enumerate(pairs):
            cand_ref[r:r + 1, :] = t1[i] + t2[j]
        best = _top_values(cand_ref[...], PEER_TOPK)
        tau = best[PEER_TOPK - 1]
        z = sum(jnp.exp(bv - best[0]) for bv in best)
        counts = [sum(jnp.where(t1[i] + t2[j] >= tau, 1.0, 0.0) for j in range(PEER_TOPK) if (i, j) in pairs)
                  for i in range(PEER_TOPK)]
        cnt = jnp.zeros_like(s1)
        rank2 = jnp.full_like(s2, NOT_RANKED)
        for i in reversed(range(PEER_TOPK)):
            cnt = jnp.where(s1 == t1[i], counts[i], cnt)
            rank2 = jnp.where(s2 == t2[i], float(i), rank2)
        cnt_ref[h] = cnt
        c1_ref[h] = jnp.exp(s1 - t1[0]) / z
        r2_ref[h] = rank2
        e2_ref[h] = jnp.exp(s2 - t2[0])


def _peer_select(q, keys1, keys2):
    m = q.shape[0]
    tm = _pick(m, (256, 128))
    assert m % tm == 0
    n_pairs = sum((i + 1) * (j + 1) <= PEER_TOPK for i in range(PEER_TOPK) for j in range(PEER_TOPK))
    cand_rows = -(-n_pairs // 8) * 8
    spec = pl.BlockSpec((PEER_HEADS, N_KEYS, tm), lambda i: (0, 0, i))
    est = 2 * tm * q.shape[1] * 4 + 2 * 4 * PEER_HEADS * N_KEYS * tm * 4 + (8 << 20)
    return pl.pallas_call(
        _peer_select_kernel,
        grid=(m // tm,),
        in_specs=[
            pl.BlockSpec((tm, q.shape[1]), lambda i: (i, 0)),
            pl.BlockSpec(keys1.shape, lambda i: (0, 0, 0)),
            pl.BlockSpec(keys2.shape, lambda i: (0, 0, 0)),
        ],
        out_specs=[spec, spec, spec, spec],
        out_shape=[jax.ShapeDtypeStruct((PEER_HEADS, N_KEYS, m), F32)] * 4,
        scratch_shapes=[pltpu.VMEM((cand_rows, tm), F32)],
        compiler_params=_params(("parallel",), est),
        name="peer_select",
    )(q, keys1, keys2)


def _peer_mix_kernel(xt_ref, u_ref, v_ref, cnt_ref, c1_ref, r2_ref, e2_ref, y_ref,
                     coef_a, coef_b, act_ref, *, te, n_blocks):
    j = pl.program_id(1)
    d, tm = xt_ref.shape
    groups = te // N_KEYS
    jb = jnp.minimum(j, n_blocks - 1)
    pieces = [(g, c) for c in range(tm // LANES) for g in range(groups)]
    col_w = 2 * LANES
    n_vchunks = d // col_w

    @pl.when(j == 0)
    def _():
        y_ref[...] = jnp.zeros_like(y_ref)
        coef_b[...] = jnp.zeros_like(coef_b)

    def select_weights(g, c):
        a = jb * groups + g
        lanes = slice(c * LANES, (c + 1) * LANES)
        cnt = [jnp.broadcast_to(cnt_ref[h, pl.ds(a, 1), :][:, lanes], (SUBLANES, LANES)) for h in range(PEER_HEADS)]
        c1a = [jnp.broadcast_to(c1_ref[h, pl.ds(a, 1), :][:, lanes], (SUBLANES, LANES)) for h in range(PEER_HEADS)]
        out = []
        for r in range(N_KEYS // SUBLANES):
            bs = slice(r * SUBLANES, (r + 1) * SUBLANES)
            acc = None
            for h in range(PEER_HEADS):
                wh = jnp.where(r2_ref[h, bs, lanes] < cnt[h], e2_ref[h, bs, lanes], 0.0) * c1a[h]
                acc = wh if acc is None else acc + wh
            out.append(acc)
        return jnp.concatenate(out, axis=0)

    def finish_piece(cur, g, c):
        rows = slice(g * N_KEYS, (g + 1) * N_KEYS)
        lanes = slice(c * LANES, (c + 1) * LANES)
        x = act_ref[rows, lanes]
        x = 0.5 * x * (1.0 + lax.erf(x * (2.0 ** -0.5)))
        cur[lanes, rows] = (x * select_weights(g, c)).T.astype(BF16)

    def step(cur, prev):
        act_ref[...] = jnp.dot(u_ref[...], xt_ref[...], preferred_element_type=F32)
        for n in range(n_vchunks):
            cs = slice(n * col_w, (n + 1) * col_w)
            y_ref[:, cs] += jnp.dot(prev[...], v_ref[:, cs], preferred_element_type=F32)
            for g, c in pieces[n::n_vchunks]:
                finish_piece(cur, g, c)

    @pl.when(j % 2 == 0)
    def _():
        step(coef_a, coef_b)

    @pl.when(j % 2 == 1)
    def _():
        step(coef_b, coef_a)


def _peer_mix(xt, u, v, cnt, c1, r2, e2):
    d, m = xt.shape
    ne = u.shape[0]
    tm = _pick(m, (512, 256, 128))
    te = 512
    assert m % tm == 0 and ne % te == 0 and te % N_KEYS == 0
    nj = ne // te
    once = pl.Buffered(1)
    sel_spec = pl.BlockSpec((PEER_HEADS, N_KEYS, tm), lambda i, j: (0, 0, i), pipeline_mode=once)
    est = (tm * d * 2 + 2 * 2 * te * d * 2 + 4 * PEER_HEADS * N_KEYS * tm * 4
           + 2 * tm * d * 4 + 2 * tm * te * 2 + 6 * te * tm * 4)
    return pl.pallas_call(
        functools.partial(_peer_mix_kernel, te=te, n_blocks=nj),
        grid=(m // tm, nj + 1),
        in_specs=[
            pl.BlockSpec((d, tm), lambda i, j: (0, i), pipeline_mode=once),
            pl.BlockSpec((te, d), lambda i, j: (jnp.minimum(j, nj - 1), 0)),
            pl.BlockSpec((te, d), lambda i, j: (jnp.maximum(j - 1, 0), 0)),
            sel_spec, sel_spec, sel_spec, sel_spec,
        ],
        out_specs=pl.BlockSpec((tm, d), lambda i, j: (i, 0)),
        out_shape=jax.ShapeDtypeStruct((m, d), F32),
        scratch_shapes=[pltpu.VMEM((tm, te), BF16), pltpu.VMEM((tm, te), BF16), pltpu.VMEM((te, tm), F32)],
        compiler_params=_params(("parallel", "arbitrary"), est, MIX_FLAGS),
        name="peer_mix",
    )(xt, u, v, cnt, c1, r2, e2)


def _rope_tables(pos, batch):
    inv = ROPE_THETA ** (-jnp.arange(0, HD_DIFF, 2, dtype=F32) / HD_DIFF)
    ang = pos.astype(F32)[:, None] * inv[None, :]
    ang = jnp.concatenate([ang, ang], axis=-1)
    sign = jnp.where(jnp.arange(HD_DIFF) < HD_DIFF // 2, -1.0, 1.0).astype(F32)
    cos = jnp.tile(jnp.cos(ang), (batch, 1))
    sin_signed = jnp.tile(jnp.sin(ang) * sign[None, :], (batch, 1))
    return cos, sin_signed


def _prep_weights(w_in, w_gate2, b_gate, lam_q1, lam_k1, lam_q2, lam_k2, diff_norm_g, gla_norm_g, w_out,
                  ln1_g, ln1_b, ln2_g, ln2_b, peer_wq, peer_keys1, peer_keys2, peer_u, peer_v):
    o_k = DIFF_W
    o_v = 2 * DIFF_W
    o_g = 3 * DIFF_W
    o_z = o_g + 2 * GLA_QK + 2 * GLA_V
    wz = jnp.pad(w_in[:, o_z:o_z + GATE_RANK], ((0, 0), (0, LANES - GATE_RANK)))
    return dict(
        wq=w_in[:, :o_k].astype(BF16),
        wk=w_in[:, o_k:o_v].astype(BF16),
        wv=w_in[:, o_v:o_g].astype(BF16),
        wg=w_in[:, o_g:o_z].astype(BF16),
        wz=wz.astype(BF16),
        w2=jnp.pad(w_gate2, ((0, LANES - GATE_RANK), (0, 0))).astype(BF16),
        bg=b_gate.reshape(1, GLA_QK),
        lam=jnp.stack([lam_q1, lam_k1, lam_q2, lam_k2]),
        dng=diff_norm_g.reshape(1, 2 * HD_DIFF),
        gng=gla_norm_g.reshape(1, GLA_DV),
        wo_a=w_out[:DIFF_W].astype(BF16),
        wo_b=w_out[DIFF_W:].astype(BF16),
        ln1_g=ln1_g.reshape(1, -1), ln1_b=ln1_b.reshape(1, -1),
        ln2_g=ln2_g.reshape(1, -1), ln2_b=ln2_b.reshape(1, -1),
        pwq=peer_wq.astype(BF16),
        pk1=peer_keys1.astype(BF16), pk2=peer_keys2.astype(BF16),
        pu=peer_u.astype(BF16), pv=peer_v.astype(BF16),
    )


def _layer(x, pos, past_k, past_v, s0, chunk, lam_init, alpha, w):
    batch, seq, d = x.shape
    m = batch * seq
    x2d = x.reshape(m, d)
    xb = x2d.astype(BF16)
    cos, sin_signed = _rope_tables(pos, batch)

    (q,) = _matmul([(xb, w["wq"])], _epi_q, [BF16], (cos, sin_signed), name="proj_q")
    k32, kb = _matmul([(xb, w["wk"])], _epi_k, [F32, BF16], (cos, sin_signed), name="proj_k")
    v32, vb = _matmul([(xb, w["wv"])], _epi_f32_bf16, [F32, BF16], name="proj_v")
    (gp,) = _matmul([(xb, w["wg"])], _epi_f32, [F32], name="proj_gla")
    (gz,) = _matmul([(xb, w["wz"])], _epi_f32, [F32], tn=LANES, name="proj_gate")

    if past_k is None:
        o_diff = _attn_prompt(q, kb, vb, w["lam"], w["dng"], batch, seq, lam_init)
    else:
        o_diff = _attn_cached(q, kb, vb, past_k, past_v, w["lam"], w["dng"], batch, seq, lam_init)
    o_gla, s_new = _gla(gp, gz, w["w2"], w["bg"], w["gng"], s0, batch, seq, chunk)

    (mix,) = _matmul([(o_diff, w["wo_a"]), (o_gla, w["wo_b"])], _epi_f32, [F32], tn=1024, name="out_proj")
    x1, x1b, x1t = _res_layernorm(x2d, mix, w["ln1_g"], w["ln1_b"], alpha, [F32, BF16], [BF16])
    new_k = k32.reshape(batch, seq, H_DIFF, 2 * HD_DIFF)
    new_v = v32.reshape(batch, seq, H_DIFF, 2 * HD_DIFF)

    (pq,) = _matmul([(x1b, w["pwq"])], _epi_f32, [F32], name="peer_query")
    cnt, c1, r2, e2 = _peer_select(pq, w["pk1"], w["pk2"])
    y = _peer_mix(x1t, w["pu"], w["pv"], cnt, c1, r2, e2)
    (x2,) = _res_layernorm(x1, y, w["ln2_g"], w["ln2_b"], alpha, [F32])
    return x2.reshape(batch, seq, d), new_k, new_v, s_new


def kernel(x_prompt, x_sample, cache_diff_k, cache_diff_v, state_gla, w_in, w_gate2, b_gate, lam_q1, lam_k1, lam_q2, lam_k2, diff_norm_g, gla_norm_g, w_out, ln1_g, ln1_b, ln2_g, ln2_b, peer_wq, peer_keys1, peer_keys2, peer_u, peer_v):
    depth = w_in.shape[0]
    tp = x_prompt.shape[1]
    ts = x_sample.shape[1]
    past = cache_diff_k.shape[2]
    alpha = (2 * depth) ** 0.25
    pos_p = jnp.arange(tp, dtype=jnp.int32)
    pos_s = past + jnp.arange(ts, dtype=jnp.int32)
    y_p, y_s = x_prompt, x_sample
    outs = [[] for _ in range(6)]
    for l in range(depth):
        lam_init = 0.8 - 0.6 * math.exp(-0.3 * l)
        w = _prep_weights(w_in[l], w_gate2[l], b_gate[l], lam_q1[l], lam_k1[l], lam_q2[l], lam_k2[l],
                          diff_norm_g[l], gla_norm_g[l], w_out[l], ln1_g[l], ln1_b[l], ln2_g[l], ln2_b[l],
                          peer_wq[l], peer_keys1[l], peer_keys2[l], peer_u[l], peer_v[l])
        y_p, kp, vp, sp = _layer(y_p, pos_p, None, None, None, CHUNK, lam_init, alpha, w)
        y_s, k_s, v_s, s_s = _layer(y_s, pos_s, cache_diff_k[l], cache_diff_v[l], state_gla[l],
                                    ts, lam_init, alpha, w)
        for lst, val in zip(outs, (kp, vp, sp, k_s, v_s, s_s)):
            lst.append(val)
    return (y_p, y_s) + tuple(jnp.stack(o) for o in outs)
```

```python
import functools
import math

import jax
import jax.numpy as jnp
from jax import lax
from jax.experimental import pallas as pl
from jax.experimental.pallas import tpu as pltpu

CHUNK = 64
HD_DIFF = 128
H_DIFF = 8
H_GLA = 4
GLA_DK = 256
GLA_DV = 512
GATE_RANK = 16
GLA_TAU = 16.0
ROPE_THETA = 10000.0
N_KEYS = 128
PEER_HEADS = 8
PEER_HALF = 128
PEER_TOPK = 16
EPS = 1e-5
NEG_INF = -1e30
ROW_BLOCK = 256

DIFF_W = H_DIFF * 2 * HD_DIFF
GLA_QK = H_GLA * GLA_DK
GLA_V = H_GLA * GLA_DV
IN_GLA = 3 * DIFF_W
IN_GATE = IN_GLA + 2 * GLA_QK + 2 * GLA_V

LANES = 128
SUBLANES = 8
VMEM_CAP = 60000 * 1024

BF16 = jnp.bfloat16
F32 = jnp.float32

NT_DIMS = (((1,), (1,)), ((), ()))
TN_DIMS = (((0,), (0,)), ((), ()))


def _pick(n, cands):
    for c in cands:
        if n % c == 0:
            return c
    return n


def _params(sem, est_bytes, flags=None):
    limit = int(min(VMEM_CAP, max(32 * 1024 * 1024, est_bytes * 5 // 4)))
    return pltpu.CompilerParams(dimension_semantics=sem, vmem_limit_bytes=limit, flags=flags)


def _mm_kernel(*refs, n_pairs, n_extra, epilogue):
    a_refs = refs[:n_pairs]
    b_refs = refs[n_pairs:2 * n_pairs]
    extras = refs[2 * n_pairs:2 * n_pairs + n_extra]
    outs = refs[2 * n_pairs + n_extra:]
    acc = jnp.dot(a_refs[0][...], b_refs[0][...].astype(BF16), preferred_element_type=F32)
    for a, b in zip(a_refs[1:], b_refs[1:]):
        acc = acc + jnp.dot(a[...], b[...].astype(BF16), preferred_element_type=F32)
    epilogue(acc, extras, outs)


def _matmul(pairs, epilogue, out_dtypes, row_extras=(), tn=512, name="matmul", cols=None):
    m = pairs[0][0].shape[0]
    c0, n = cols if cols is not None else (0, pairs[0][1].shape[1])
    tm = _pick(m, (1024, 512, 256, 128, 64, 32, 16, 8))
    tn = min(tn, n)
    assert n % tn == 0 and c0 % tn == 0
    jb = c0 // tn
    in_specs = []
    est = 0
    for a, _ in pairs:
        in_specs.append(pl.BlockSpec((tm, a.shape[1]), lambda i, j: (i, 0)))
        est += 2 * tm * a.shape[1] * a.dtype.itemsize
    for _, b in pairs:
        in_specs.append(pl.BlockSpec((b.shape[0], tn), lambda i, j: (0, jb + j)))
        est += 2 * b.shape[0] * tn * b.dtype.itemsize + b.shape[0] * tn * 2
    for e in row_extras:
        in_specs.append(pl.BlockSpec((tm, e.shape[1]), lambda i, j: (i, 0)))
        est += 2 * tm * e.shape[1] * 4
    out_specs = [pl.BlockSpec((tm, tn), lambda i, j: (i, j)) for _ in out_dtypes]
    out_shape = [jax.ShapeDtypeStruct((m, n), d) for d in out_dtypes]
    est += sum(2 * tm * tn * jnp.dtype(d).itemsize for d in out_dtypes) + 3 * tm * tn * 4
    kern = functools.partial(_mm_kernel, n_pairs=len(pairs), n_extra=len(row_extras), epilogue=epilogue)
    return pl.pallas_call(
        kern,
        grid=(m // tm, n // tn),
        in_specs=in_specs,
        out_specs=out_specs,
        out_shape=out_shape,
        compiler_params=_params(("parallel", "arbitrary"), est),
        name=name,
    )(*[a for a, _ in pairs], *[b for _, b in pairs], *row_extras)


def _rope_tile(acc, cos, sin_signed):
    outs = []
    for g in range(acc.shape[1] // LANES):
        xg = acc[:, g * LANES:(g + 1) * LANES]
        outs.append(xg * cos + pltpu.roll(xg, LANES // 2, 1) * sin_signed)
    return outs[0] if len(outs) == 1 else jnp.concatenate(outs, axis=1)


def _epi_q(acc, extras, outs):
    r = _rope_tile(acc, extras[0][...], extras[1][...])
    outs[0][...] = (r * (HD_DIFF ** -0.5 * math.log2(math.e))).astype(BF16)


def _epi_k(acc, extras, outs):
    r = _rope_tile(acc, extras[0][...], extras[1][...])
    outs[0][...] = r
    outs[1][...] = r.astype(BF16)


def _epi_f32_bf16(acc, extras, outs):
    outs[0][...] = acc
    outs[1][...] = acc.astype(BF16)


def _epi_f32(acc, extras, outs):
    outs[0][...] = acc


def _lambda_value(lam_ref, lam_init):
    lq1 = lam_ref[0:1, :]
    lk1 = lam_ref[1:2, :]
    lq2 = lam_ref[2:3, :]
    lk2 = lam_ref[3:4, :]
    return (jnp.exp(jnp.sum(lq1 * lk1, axis=1, keepdims=True))
            - jnp.exp(jnp.sum(lq2 * lk2, axis=1, keepdims=True)) + lam_init)


def _diff_finalize(o1, o2, lam_ref, g_ref, lam_init):
    lam = _lambda_value(lam_ref, lam_init)
    o = o1 - lam * o2
    ms = jnp.mean(o * o, axis=1, keepdims=True)
    return (o * lax.rsqrt(ms + EPS)) * g_ref[...] * (1.0 - lam_init)


def _attn_prompt_kernel(qi_ref, kj_ref, q_ref, k_ref, v_ref, lam_ref, g_ref, o_ref,
                        m1, l1, a1, m2, l2, a2, *, tq, lam_init):
    p = pl.program_id(2)
    i = qi_ref[p]
    j = kj_ref[p]
    half_t = tq // 2

    @pl.when(j == 0)
    def _():
        m1[...] = jnp.full_like(m1, NEG_INF)
        m2[...] = jnp.full_like(m2, NEG_INF)
        l1[...] = jnp.zeros_like(l1)
        l2[...] = jnp.zeros_like(l2)
        a1[...] = jnp.zeros_like(a1)
        a2[...] = jnp.zeros_like(a2)

    def update(r0, nr, nc, masked):
        if nr > ROW_BLOCK:
            for rr in range(r0, r0 + nr, ROW_BLOCK):
                update(rr, ROW_BLOCK, nc, masked)
            return
        rows = slice(r0, r0 + nr)
        q = q_ref[rows, :]
        k = k_ref[0:nc, :]
        v = v_ref[0:nc, :]
        if masked:
            qc = (r0 + lax.broadcasted_iota(jnp.int32, (nr, nc), 0)) // CHUNK
            kc = lax.broadcasted_iota(jnp.int32, (nr, nc), 1) // CHUNK
            mask = qc >= kc
        for half, (m_ref, l_ref, a_ref) in enumerate(((m1, l1, a1), (m2, l2, a2))):
            qh = q[:, half * HD_DIFF:(half + 1) * HD_DIFF]
            kh = k[:, half * HD_DIFF:(half + 1) * HD_DIFF]
            s = lax.dot_general(qh, kh, NT_DIMS, preferred_element_type=F32)
            if masked:
                s = jnp.where(mask, s, NEG_INF)
            m_old = m_ref[rows, :]
            m_new = jnp.maximum(m_old, jnp.max(s, axis=1, keepdims=True))
            alpha = jnp.exp2(m_old - m_new)
            pr = jnp.exp2(s - m_new)
            l_ref[rows, :] = alpha * l_ref[rows, :] + jnp.sum(pr, axis=1, keepdims=True)
            a_ref[rows, :] = alpha * a_ref[rows, :] + jnp.dot(pr.astype(BF16), v, preferred_element_type=F32)
            m_ref[rows, :] = m_new

    @pl.when(j < i)
    def _():
        update(0, tq, tq, False)

    @pl.when(j == i)
    def _():
        if half_t % CHUNK == 0:
            update(0, half_t, half_t, True)
            update(half_t, half_t, tq, True)
        else:
            update(0, tq, tq, True)
        o = _diff_finalize(a1[...] / l1[...], a2[...] / l2[...], lam_ref, g_ref, lam_init)
        o_ref[...] = o.astype(o_ref.dtype)


def _attn_prompt(q, k, v, lam_vecs, g, batch, seq, lam_init):
    tq = _pick(seq, (1024, 512, 256, 128, 64))
    nq = seq // tq
    hw = 2 * HD_DIFF
    pairs = [(i, j) for i in range(nq) for j in range(i + 1)]
    qi = jnp.asarray([p[0] for p in pairs], jnp.int32)
    kj = jnp.asarray([p[1] for p in pairs], jnp.int32)

    def q_map(b, h, p, qi_ref, kj_ref):
        return (b * nq + qi_ref[p], h)

    def kv_map(b, h, p, qi_ref, kj_ref):
        return (b * nq + kj_ref[p], h)

    kern = functools.partial(_attn_prompt_kernel, tq=tq, lam_init=lam_init)
    est = 2 * 4 * tq * hw * 2 + 2 * tq * hw * 4 + 4 * tq * LANES * 4 + 8 * tq * tq * 4
    return pl.pallas_call(
        kern,
        grid_spec=pltpu.PrefetchScalarGridSpec(
            num_scalar_prefetch=2,
            grid=(batch, H_DIFF, len(pairs)),
            in_specs=[
                pl.BlockSpec((tq, hw), q_map),
                pl.BlockSpec((tq, hw), kv_map),
                pl.BlockSpec((tq, hw), kv_map),
                pl.BlockSpec((4, HD_DIFF), lambda b, h, p, qi_ref, kj_ref: (0, 0)),
                pl.BlockSpec((1, hw), lambda b, h, p, qi_ref, kj_ref: (0, 0)),
            ],
            out_specs=pl.BlockSpec((tq, hw), q_map),
            scratch_shapes=[
                pltpu.VMEM((tq, 1), F32), pltpu.VMEM((tq, 1), F32), pltpu.VMEM((tq, hw), F32),
                pltpu.VMEM((tq, 1), F32), pltpu.VMEM((tq, 1), F32), pltpu.VMEM((tq, hw), F32),
            ],
        ),
        out_shape=jax.ShapeDtypeStruct((batch * seq, DIFF_W), BF16),
        compiler_params=_params(("parallel", "parallel", "arbitrary"), est),
        name="diff_attn_prompt",
    )(qi, kj, q, k, v, lam_vecs, g)


def _attn_cached_kernel(q_ref, kn_ref, vn_ref, kc_ref, vc_ref, lam_ref, g_ref, o_ref, *, past, lam_init):
    q = q_ref[...]
    kn = kn_ref[...]
    vn = vn_ref[...]
    kc = kc_ref[0].astype(BF16)
    vc = vc_ref[0].astype(BF16)
    ts = q.shape[0]
    qchunk_c = (past + lax.broadcasted_iota(jnp.int32, (ts, past), 0)) // CHUNK
    kchunk_c = lax.broadcasted_iota(jnp.int32, (ts, past), 1) // CHUNK
    qchunk_n = (past + lax.broadcasted_iota(jnp.int32, (ts, ts), 0)) // CHUNK
    kchunk_n = (past + lax.broadcasted_iota(jnp.int32, (ts, ts), 1)) // CHUNK
    outs = []
    for half in range(2):
        sl = slice(half * HD_DIFF, (half + 1) * HD_DIFF)
        sc = lax.dot_general(q[:, sl], kc[:, sl], NT_DIMS, preferred_element_type=F32)
        sn = lax.dot_general(q[:, sl], kn[:, sl], NT_DIMS, preferred_element_type=F32)
        sc = jnp.where(qchunk_c >= kchunk_c, sc, NEG_INF)
        sn = jnp.where(qchunk_n >= kchunk_n, sn, NEG_INF)
        m = jnp.maximum(jnp.max(sc, axis=1, keepdims=True), jnp.max(sn, axis=1, keepdims=True))
        pc = jnp.exp2(sc - m)
        pn = jnp.exp2(sn - m)
        l = jnp.sum(pc, axis=1, keepdims=True) + jnp.sum(pn, axis=1, keepdims=True)
        acc = (jnp.dot(pc.astype(BF16), vc, preferred_element_type=F32)
               + jnp.dot(pn.astype(BF16), vn, preferred_element_type=F32))
        outs.append(acc / l)
    o = _diff_finalize(outs[0], outs[1], lam_ref, g_ref, lam_init)
    o_ref[...] = o.astype(o_ref.dtype)


def _attn_cached(q, kn, vn, cache_k, cache_v, lam_vecs, g, batch, seq, lam_init):
    past = cache_k.shape[1]
    hw = 2 * HD_DIFF
    ck = cache_k.reshape(batch, past, DIFF_W)
    cv = cache_v.reshape(batch, past, DIFF_W)
    kern = functools.partial(_attn_cached_kernel, past=past, lam_init=lam_init)
    est = 2 * 2 * past * hw * 4 + 2 * past * hw * 2 + 8 * seq * past * 4
    return pl.pallas_call(
        kern,
        grid=(batch, H_DIFF),
        in_specs=[
            pl.BlockSpec((seq, hw), lambda b, h: (b, h)),
            pl.BlockSpec((seq, hw), lambda b, h: (b, h)),
            pl.BlockSpec((seq, hw), lambda b, h: (b, h)),
            pl.BlockSpec((1, past, hw), lambda b, h: (b, 0, h)),
            pl.BlockSpec((1, past, hw), lambda b, h: (b, 0, h)),
            pl.BlockSpec((4, HD_DIFF), lambda b, h: (0, 0)),
            pl.BlockSpec((1, hw), lambda b, h: (0, 0)),
        ],
        out_specs=pl.BlockSpec((seq, hw), lambda b, h: (b, h)),
        out_shape=jax.ShapeDtypeStruct((batch * seq, DIFF_W), BF16),
        compiler_params=_params(("parallel", "parallel"), est),
        name="diff_attn_cached",
    )(q, kn, vn, ck, cv, lam_vecs, g)


def _split3(x):
    hi = x.astype(BF16)
    r1 = x - hi.astype(F32)
    mid = r1.astype(BF16)
    lo = (r1 - mid.astype(F32)).astype(BF16)
    return hi, mid, lo


def _gla_kernel(*refs, chunk, n_chunks, has_s0):
    if has_s0:
        q_ref, k_ref, v_ref, r_ref, z_ref, w2_ref, bg_ref, gn_ref, s0_ref, o_ref, s_ref, st = refs
    else:
        q_ref, k_ref, v_ref, r_ref, z_ref, w2_ref, bg_ref, gn_ref, o_ref, s_ref, st = refs
    i = pl.program_id(2)

    @pl.when(i == 0)
    def _():
        if has_s0:
            st[...] = s0_ref[0, 0].T
        else:
            st[...] = jnp.zeros_like(st)

    scale = GLA_DK ** -0.5
    zz = jnp.dot(z_ref[...].astype(BF16), w2_ref[...], preferred_element_type=F32) + bg_ref[...]
    logf = jax.nn.log_sigmoid(zz) / GLA_TAU
    row = lax.broadcasted_iota(jnp.int32, (chunk, chunk), 0)
    col = lax.broadcasted_iota(jnp.int32, (chunk, chunk), 1)
    causal = row >= col
    tri = jnp.where(causal, 1.0, 0.0).astype(BF16)
    for c in range(n_chunks):
        rows = slice(c * chunk, (c + 1) * chunk)
        g = logf[rows]
        bcum = sum(jnp.dot(tri, part, preferred_element_type=F32) for part in _split3(g))
        b_last = bcum[chunk - 1:chunk, :]
        qc = q_ref[rows, :]
        kc = k_ref[rows, :]
        vc = v_ref[rows, :].astype(BF16)
        q_in = (qc * jnp.exp(bcum) * scale).astype(BF16)
        k_in = (kc * jnp.exp(-bcum)).astype(BF16)
        k_end = (kc * jnp.exp(b_last - bcum)).astype(BF16)
        a = lax.dot_general(q_in, k_in, NT_DIMS, preferred_element_type=F32)
        a = jnp.where(causal, a, 0.0)
        s_old = st[...]
        o = (jnp.dot(a.astype(BF16), vc, preferred_element_type=F32)
             + lax.dot_general(q_in, s_old.astype(BF16), NT_DIMS, preferred_element_type=F32))
        st[...] = s_old * jnp.exp(b_last) + lax.dot_general(vc, k_end, TN_DIMS, preferred_element_type=F32)
        ms = jnp.mean(o * o, axis=1, keepdims=True)
        on = (o * lax.rsqrt(ms + EPS)) * gn_ref[...]
        gate = r_ref[rows, :]
        o_ref[rows, :] = (on * (gate * jax.nn.sigmoid(gate))).astype(o_ref.dtype)

    @pl.when(i == pl.num_programs(2) - 1)
    def _():
        s_ref[0, 0] = st[...].T


def _gla(gp, gz, w2, bg, gn, s0, batch, seq, chunk):
    tb = chunk * max(1, min(8, seq // chunk))
    while seq % tb:
        tb -= chunk
    nb = seq // tb
    kq = GLA_QK // GLA_DK
    kv = (2 * GLA_QK) // GLA_DV
    kr = kv + GLA_V // GLA_DV
    in_specs = [
        pl.BlockSpec((tb, GLA_DK), lambda b, h, i: (b * nb + i, h)),
        pl.BlockSpec((tb, GLA_DK), lambda b, h, i: (b * nb + i, kq + h)),
        pl.BlockSpec((tb, GLA_DV), lambda b, h, i: (b * nb + i, kv + h)),
        pl.BlockSpec((tb, GLA_DV), lambda b, h, i: (b * nb + i, kr + h)),
        pl.BlockSpec((tb, LANES), lambda b, h, i: (b * nb + i, 0)),
        pl.BlockSpec((LANES, GLA_DK), lambda b, h, i: (0, h)),
        pl.BlockSpec((1, GLA_DK), lambda b, h, i: (0, h)),
        pl.BlockSpec((1, GLA_DV), lambda b, h, i: (0, 0)),
    ]
    args = [gp, gp, gp, gp, gz, w2, bg, gn]
    if s0 is not None:
        in_specs.append(pl.BlockSpec((1, 1, GLA_DK, GLA_DV), lambda b, h, i: (b, h, 0, 0)))
        args.append(s0)
    kern = functools.partial(_gla_kernel, chunk=chunk, n_chunks=tb // chunk, has_s0=s0 is not None)
    est = 2 * tb * (2 * GLA_DK + 2 * GLA_DV + LANES) * 4 + 2 * tb * GLA_DV * 2 + 7 * GLA_DK * GLA_DV * 4
    return pl.pallas_call(
        kern,
        grid=(batch, H_GLA, nb),
        in_specs=in_specs,
        out_specs=[
            pl.BlockSpec((tb, GLA_DV), lambda b, h, i: (b * nb + i, h)),
            pl.BlockSpec((1, 1, GLA_DK, GLA_DV), lambda b, h, i: (b, h, 0, 0)),
        ],
        out_shape=[
            jax.ShapeDtypeStruct((batch * seq, GLA_V), BF16),
            jax.ShapeDtypeStruct((batch, H_GLA, GLA_DK, GLA_DV), F32),
        ],
        scratch_shapes=[pltpu.VMEM((GLA_DV, GLA_DK), F32)],
        compiler_params=_params(("parallel", "parallel", "arbitrary"), est),
        name="gla",
    )(*args)


def _ln_kernel(x_ref, y_ref, g_ref, b_ref, *outs, alpha, n_plain):
    h = alpha * x_ref[...] + y_ref[...].astype(F32)
    mu = jnp.mean(h, axis=1, keepdims=True)
    d = h - mu
    var = jnp.mean(d * d, axis=1, keepdims=True)
    r = (d * lax.rsqrt(var + EPS)) * g_ref[...] + b_ref[...]
    for o in outs[:n_plain]:
        o[...] = r.astype(o.dtype)
    for o in outs[n_plain:]:
        o[...] = r.T.astype(o.dtype)


def _res_layernorm(x, y, g, b, alpha, out_dtypes, transposed_dtypes=()):
    m, d = x.shape
    tm = _pick(m, (128,)) if transposed_dtypes else _pick(m, (128, 64, 32, 16, 8))
    assert m % tm == 0
    sizes = [jnp.dtype(t).itemsize for t in (*out_dtypes, *transposed_dtypes)]
    est = 2 * tm * d * (x.dtype.itemsize + y.dtype.itemsize + sum(sizes)) + 5 * tm * d * 4
    return pl.pallas_call(
        functools.partial(_ln_kernel, alpha=alpha, n_plain=len(out_dtypes)),
        grid=(m // tm,),
        in_specs=[
            pl.BlockSpec((tm, d), lambda i: (i, 0)),
            pl.BlockSpec((tm, d), lambda i: (i, 0)),
            pl.BlockSpec((1, d), lambda i: (0, 0)),
            pl.BlockSpec((1, d), lambda i: (0, 0)),
        ],
        out_specs=[pl.BlockSpec((tm, d), lambda i: (i, 0)) for _ in out_dtypes]
        + [pl.BlockSpec((d, tm), lambda i: (0, i)) for _ in transposed_dtypes],
        out_shape=[jax.ShapeDtypeStruct((m, d), t) for t in out_dtypes]
        + [jax.ShapeDtypeStruct((d, m), t) for t in transposed_dtypes],
        compiler_params=_params(("parallel",), est),
        name="res_layernorm",
    )(x, y, g, b)


def _top_values(s, count):
    vals = []
    cur = s
    for t in range(count):
        m = jnp.max(cur, axis=0, keepdims=True)
        vals.append(m)
        if t + 1 < count:
            cur = jnp.where(cur == m, -jnp.inf, cur)
    return vals


MIX_FLAGS = None
NOT_RANKED = float(N_KEYS - 1)


def _peer_select_kernel(q_ref, k1_ref, k2_ref, cnt_ref, c1_ref, r2_ref, e2_ref, cand_ref):
    pairs = [(i, j) for i in range(PEER_TOPK) for j in range(PEER_TOPK) if (i + 1) * (j + 1) <= PEER_TOPK]
    for h in range(PEER_HEADS):
        base = h * 2 * PEER_HALF
        q1 = q_ref[:, base:base + PEER_HALF].astype(BF16)
        q2 = q_ref[:, base + PEER_HALF:base + 2 * PEER_HALF].astype(BF16)
        s1 = lax.dot_general(k1_ref[h], q1, NT_DIMS, preferred_element_type=F32)
        s2 = lax.dot_general(k2_ref[h], q2, NT_DIMS, preferred_element_type=F32)
        t1 = _top_values(s1, PEER_TOPK)
        t2 = _top_values(s2, PEER_TOPK)
        cand_ref[...] = jnp.full_like(cand_ref, -jnp.inf)
        for r, (i, j) in enumerate(pairs):
            cand_ref[r:r + 1, :] = t1[i] + t2[j]
        best = _top_values(cand_ref[...], PEER_TOPK)
        tau = best[PEER_TOPK - 1]
        z = sum(jnp.exp(bv - best[0]) for bv in best)
        counts = [sum(jnp.where(t1[i] + t2[j] >= tau, 1.0, 0.0) for j in range(PEER_TOPK) if (i, j) in pairs)
                  for i in range(PEER_TOPK)]
        cnt = jnp.zeros_like(s1)
        rank2 = jnp.full_like(s2, NOT_RANKED)
        for i in reversed(range(PEER_TOPK)):
            cnt = jnp.where(s1 == t1[i], counts[i], cnt)
            rank2 = jnp.where(s2 == t2[i], float(i), rank2)
        cnt_ref[h] = cnt
        c1_ref[h] = jnp.exp(s1 - t1[0]) / z
        r2_ref[h] = rank2
        e2_ref[h] = jnp.exp(s2 - t2[0])


def _peer_select(q, keys1, keys2):
    m = q.shape[0]
    tm = _pick(m, (256, 128))
    assert m % tm == 0
    n_pairs = sum((i + 1) * (j + 1) <= PEER_TOPK for i in range(PEER_TOPK) for j in range(PEER_TOPK))
    cand_rows = -(-n_pairs // 8) * 8
    spec = pl.BlockSpec((PEER_HEADS, N_KEYS, tm), lambda i: (0, 0, i))
    est = 2 * tm * q.shape[1] * 4 + 2 * 4 * PEER_HEADS * N_KEYS * tm * 4 + (8 << 20)
    return pl.pallas_call(
        _peer_select_kernel,
        grid=(m // tm,),
        in_specs=[
            pl.BlockSpec((tm, q.shape[1]), lambda i: (i, 0)),
            pl.BlockSpec(keys1.shape, lambda i: (0, 0, 0)),
            pl.BlockSpec(keys2.shape, lambda i: (0, 0, 0)),
        ],
        out_specs=[spec, spec, spec, spec],
        out_shape=[jax.ShapeDtypeStruct((PEER_HEADS, N_KEYS, m), F32)] * 4,
        scratch_shapes=[pltpu.VMEM((cand_rows, tm), F32)],
        compiler_params=_params(("parallel",), est),
        name="peer_select",
    )(q, keys1, keys2)


def _peer_mix_kernel(xt_ref, u_ref, v_ref, cnt_ref, c1_ref, r2_ref, e2_ref, y_ref,
                     coef_a, coef_b, act_ref, *, te, n_blocks):
    j = pl.program_id(1)
    d, tm = xt_ref.shape
    groups = te // N_KEYS
    jb = jnp.minimum(j, n_blocks - 1)
    pieces = [(g, c) for c in range(tm // LANES) for g in range(groups)]
    col_w = 2 * LANES
    n_vchunks = d // col_w

    @pl.when(j == 0)
    def _():
        y_ref[...] = jnp.zeros_like(y_ref)
        coef_b[...] = jnp.zeros_like(coef_b)

    def select_weights(g, c):
        a = jb * groups + g
        lanes = slice(c * LANES, (c + 1) * LANES)
        cnt = [jnp.broadcast_to(cnt_ref[h, pl.ds(a, 1), :][:, lanes], (SUBLANES, LANES)) for h in range(PEER_HEADS)]
        c1a = [jnp.broadcast_to(c1_ref[h, pl.ds(a, 1), :][:, lanes], (SUBLANES, LANES)) for h in range(PEER_HEADS)]
        out = []
        for r in range(N_KEYS // SUBLANES):
            bs = slice(r * SUBLANES, (r + 1) * SUBLANES)
            acc = None
            for h in range(PEER_HEADS):
                wh = jnp.where(r2_ref[h, bs, lanes] < cnt[h], e2_ref[h, bs, lanes], 0.0) * c1a[h]
                acc = wh if acc is None else acc + wh
            out.append(acc)
        return jnp.concatenate(out, axis=0)

    def finish_piece(cur, g, c):
        rows = slice(g * N_KEYS, (g + 1) * N_KEYS)
        lanes = slice(c * LANES, (c + 1) * LANES)
        x = act_ref[rows, lanes]
        x = 0.5 * x * (1.0 + lax.erf(x * (2.0 ** -0.5)))
        cur[lanes, rows] = (x * select_weights(g, c)).T.astype(BF16)

    def step(cur, prev):
        act_ref[...] = jnp.dot(u_ref[...], xt_ref[...], preferred_element_type=F32)
        for n in range(n_vchunks):
            cs = slice(n * col_w, (n + 1) * col_w)
            y_ref[:, cs] += jnp.dot(prev[...], v_ref[:, cs], preferred_element_type=F32)
            for g, c in pieces[n::n_vchunks]:
                finish_piece(cur, g, c)

    @pl.when(j % 2 == 0)
    def _():
        step(coef_a, coef_b)

    @pl.when(j % 2 == 1)
    def _():
        step(coef_b, coef_a)


def _peer_mix(xt, u, v, cnt, c1, r2, e2):
    d, m = xt.shape
    ne = v.shape[0]
    tm = _pick(m, (512, 256, 128))
    te = 512
    assert m % tm == 0 and ne % te == 0 and te % N_KEYS == 0
    nj = ne // te
    once = pl.Buffered(1)
    sel_spec = pl.BlockSpec((PEER_HEADS, N_KEYS, tm), lambda i, j: (0, 0, i), pipeline_mode=once)
    est = (tm * d * 2 + 2 * 2 * te * d * 2 + 4 * PEER_HEADS * N_KEYS * tm * 4
           + 2 * tm * d * 4 + 2 * tm * te * 2 + 6 * te * tm * 4)
    return pl.pallas_call(
        functools.partial(_peer_mix_kernel, te=te, n_blocks=nj),
        grid=(m // tm, nj + 1),
        in_specs=[
            pl.BlockSpec((d, tm), lambda i, j: (0, i), pipeline_mode=once),
            pl.BlockSpec((te, d), lambda i, j: (jnp.minimum(j, nj - 1), 0)),
            pl.BlockSpec((te, d), lambda i, j: (jnp.maximum(j - 1, 0), 0)),
            sel_spec, sel_spec, sel_spec, sel_spec,
        ],
        out_specs=pl.BlockSpec((tm, d), lambda i, j: (i, 0)),
        out_shape=jax.ShapeDtypeStruct((m, d), F32),
        scratch_shapes=[pltpu.VMEM((tm, te), BF16), pltpu.VMEM((tm, te), BF16), pltpu.VMEM((te, tm), F32)],
        compiler_params=_params(("parallel", "arbitrary"), est, MIX_FLAGS),
        name="peer_mix",
    )(xt, u, v, cnt, c1, r2, e2)


def _rope_tables(pos, batch):
    inv = ROPE_THETA ** (-jnp.arange(0, HD_DIFF, 2, dtype=F32) / HD_DIFF)
    ang = pos.astype(F32)[:, None] * inv[None, :]
    ang = jnp.concatenate([ang, ang], axis=-1)
    sign = jnp.where(jnp.arange(HD_DIFF) < HD_DIFF // 2, -1.0, 1.0).astype(F32)
    cos = jnp.tile(jnp.cos(ang), (batch, 1))
    sin_signed = jnp.tile(jnp.sin(ang) * sign[None, :], (batch, 1))
    return cos, sin_signed


def _prep_weights(w_in, w_gate2, b_gate, lam_q1, lam_k1, lam_q2, lam_k2, diff_norm_g, gla_norm_g, w_out,
                  ln1_g, ln1_b, ln2_g, ln2_b, peer_wq, peer_keys1, peer_keys2, peer_u, peer_v):
    wz = jnp.pad(w_in[:, IN_GATE:IN_GATE + GATE_RANK], ((0, 0), (0, LANES - GATE_RANK)))
    return dict(
        w_in=w_in,
        wz=wz.astype(BF16),
        w2=jnp.pad(w_gate2, ((0, LANES - GATE_RANK), (0, 0))).astype(BF16),
        bg=b_gate.reshape(1, GLA_QK),
        lam=jnp.stack([lam_q1, lam_k1, lam_q2, lam_k2]),
        dng=diff_norm_g.reshape(1, 2 * HD_DIFF),
        gng=gla_norm_g.reshape(1, GLA_DV),
        wo_a=w_out[:DIFF_W].astype(BF16),
        wo_b=w_out[DIFF_W:].astype(BF16),
        ln1_g=ln1_g.reshape(1, -1), ln1_b=ln1_b.reshape(1, -1),
        ln2_g=ln2_g.reshape(1, -1), ln2_b=ln2_b.reshape(1, -1),
        pwq=peer_wq.astype(BF16),
        pk1=peer_keys1.astype(BF16), pk2=peer_keys2.astype(BF16),
        pu=peer_u.astype(BF16), pv=peer_v.astype(BF16),
    )


def _layer(x, pos, past_k, past_v, s0, chunk, lam_init, alpha, w):
    batch, seq, d = x.shape
    m = batch * seq
    x2d = x.reshape(m, d)
    xb = x2d.astype(BF16)
    cos, sin_signed = _rope_tables(pos, batch)

    proj = [(xb, w["w_in"])]
    (q,) = _matmul(proj, _epi_q, [BF16], (cos, sin_signed), name="proj_q", cols=(0, DIFF_W))
    k32, kb = _matmul(proj, _epi_k, [F32, BF16], (cos, sin_signed), name="proj_k", cols=(DIFF_W, DIFF_W))
    v32, vb = _matmul(proj, _epi_f32_bf16, [F32, BF16], name="proj_v", cols=(2 * DIFF_W, DIFF_W))
    (gp,) = _matmul(proj, _epi_f32, [F32], name="proj_gla", cols=(IN_GLA, IN_GATE - IN_GLA))
    (gz,) = _matmul([(xb, w["wz"])], _epi_f32, [F32], tn=LANES, name="proj_gate")

    if past_k is None:
        o_diff = _attn_prompt(q, kb, vb, w["lam"], w["dng"], batch, seq, lam_init)
    else:
        o_diff = _attn_cached(q, kb, vb, past_k, past_v, w["lam"], w["dng"], batch, seq, lam_init)
    o_gla, s_new = _gla(gp, gz, w["w2"], w["bg"], w["gng"], s0, batch, seq, chunk)

    (mix,) = _matmul([(o_diff, w["wo_a"]), (o_gla, w["wo_b"])], _epi_f32, [F32], tn=1024, name="out_proj")
    x1, x1b, x1t = _res_layernorm(x2d, mix, w["ln1_g"], w["ln1_b"], alpha, [F32, BF16], [BF16])
    new_k = k32.reshape(batch, seq, H_DIFF, 2 * HD_DIFF)
    new_v = v32.reshape(batch, seq, H_DIFF, 2 * HD_DIFF)

    (pq,) = _matmul([(x1b, w["pwq"])], _epi_f32, [F32], name="peer_query")
    cnt, c1, r2, e2 = _peer_select(pq, w["pk1"], w["pk2"])
    y = _peer_mix(x1t, w["pu"], w["pv"], cnt, c1, r2, e2)
    (x2,) = _res_layernorm(x1, y, w["ln2_g"], w["ln2_b"], alpha, [F32])
    return x2.reshape(batch, seq, d), new_k, new_v, s_new


def kernel(x_prompt, x_sample, cache_diff_k, cache_diff_v, state_gla, w_in, w_gate2, b_gate, lam_q1, lam_k1, lam_q2, lam_k2, diff_norm_g, gla_norm_g, w_out, ln1_g, ln1_b, ln2_g, ln2_b, peer_wq, peer_keys1, peer_keys2, peer_u, peer_v):
    depth = w_in.shape[0]
    tp = x_prompt.shape[1]
    ts = x_sample.shape[1]
    past = cache_diff_k.shape[2]
    alpha = (2 * depth) ** 0.25
    pos_p = jnp.arange(tp, dtype=jnp.int32)
    pos_s = past + jnp.arange(ts, dtype=jnp.int32)
    y_p, y_s = x_prompt, x_sample
    outs = [[] for _ in range(6)]
    for l in range(depth):
        lam_init = 0.8 - 0.6 * math.exp(-0.3 * l)
        w = _prep_weights(w_in[l], w_gate2[l], b_gate[l], lam_q1[l], lam_k1[l], lam_q2[l], lam_k2[l],
                          diff_norm_g[l], gla_norm_g[l], w_out[l], ln1_g[l], ln1_b[l], ln2_g[l], ln2_b[l],
                          peer_wq[l], peer_keys1[l], peer_keys2[l], peer_u[l], peer_v[l])
        y_p, kp, vp, sp = _layer(y_p, pos_p, None, None, None, CHUNK, lam_init, alpha, w)
        y_s, k_s, v_s, s_s = _layer(y_s, pos_s, cache_diff_k[l], cache_diff_v[l], state_gla[l],
                                    ts, lam_init, alpha, w)
        for lst, val in zip(outs, (kp, vp, sp, k_s, v_s, s_s)):
            lst.append(val)
    return (y_p, y_s) + tuple(jnp.stack(o) for o in outs)
```

```python
import functools
import math

import jax
import jax.numpy as jnp
from jax import lax
from jax.experimental import pallas as pl
from jax.experimental.pallas import tpu as pltpu

CHUNK = 64
HD_DIFF = 128
H_DIFF = 8
H_GLA = 4
GLA_DK = 256
GLA_DV = 512
GATE_RANK = 16
GLA_TAU = 16.0
ROPE_THETA = 10000.0
N_KEYS = 128
PEER_HEADS = 8
PEER_HALF = 128
PEER_TOPK = 16
EPS = 1e-5
NEG_INF = -1e30
ROW_BLOCK = 256

DIFF_W = H_DIFF * 2 * HD_DIFF
GLA_QK = H_GLA * GLA_DK
GLA_V = H_GLA * GLA_DV
IN_GLA = 3 * DIFF_W
IN_GATE = IN_GLA + 2 * GLA_QK + 2 * GLA_V

LANES = 128
SUBLANES = 8
VMEM_CAP = 60000 * 1024

BF16 = jnp.bfloat16
F32 = jnp.float32

NT_DIMS = (((1,), (1,)), ((), ()))
TN_DIMS = (((0,), (0,)), ((), ()))


def _pick(n, cands):
    for c in cands:
        if n % c == 0:
            return c
    return n


def _params(sem, est_bytes):
    limit = int(min(VMEM_CAP, max(32 * 1024 * 1024, est_bytes * 5 // 4)))
    return pltpu.CompilerParams(dimension_semantics=sem, vmem_limit_bytes=limit)


def _mm_kernel(*refs, n_pairs, n_extra, epilogue):
    a_refs = refs[:n_pairs]
    b_refs = refs[n_pairs:2 * n_pairs]
    extras = refs[2 * n_pairs:2 * n_pairs + n_extra]
    outs = refs[2 * n_pairs + n_extra:]
    acc = jnp.dot(a_refs[0][...], b_refs[0][...], preferred_element_type=F32)
    for a, b in zip(a_refs[1:], b_refs[1:]):
        acc = acc + jnp.dot(a[...], b[...], preferred_element_type=F32)
    epilogue(acc, extras, outs)


def _matmul(pairs, epilogue, out_dtypes, row_extras=(), tn=512, name="matmul", cols=None):
    m = pairs[0][0].shape[0]
    c0, n = cols if cols is not None else (0, pairs[0][1].shape[1])
    tm = _pick(m, (1024, 512, 256, 128, 64, 32, 16, 8))
    tn = min(tn, n)
    assert n % tn == 0 and c0 % tn == 0
    jb = c0 // tn
    in_specs = []
    est = 0
    for a, _ in pairs:
        in_specs.append(pl.BlockSpec((tm, a.shape[1]), lambda i, j: (i, 0)))
        est += 2 * tm * a.shape[1] * a.dtype.itemsize
    for _, b in pairs:
        in_specs.append(pl.BlockSpec((b.shape[0], tn), lambda i, j: (0, jb + j)))
        est += 2 * b.shape[0] * tn * b.dtype.itemsize
    for e in row_extras:
        in_specs.append(pl.BlockSpec((tm, e.shape[1]), lambda i, j: (i, 0)))
        est += 2 * tm * e.shape[1] * 4
    out_specs = [pl.BlockSpec((tm, tn), lambda i, j: (i, j)) for _ in out_dtypes]
    out_shape = [jax.ShapeDtypeStruct((m, n), d) for d in out_dtypes]
    est += sum(2 * tm * tn * jnp.dtype(d).itemsize for d in out_dtypes) + 3 * tm * tn * 4
    kern = functools.partial(_mm_kernel, n_pairs=len(pairs), n_extra=len(row_extras), epilogue=epilogue)
    return pl.pallas_call(
        kern,
        grid=(m // tm, n // tn),
        in_specs=in_specs,
        out_specs=out_specs,
        out_shape=out_shape,
        compiler_params=_params(("parallel", "arbitrary"), est),
        name=name,
    )(*[a for a, _ in pairs], *[b for _, b in pairs], *row_extras)


def _rope_tile(acc, cos, sin_signed):
    outs = []
    for g in range(acc.shape[1] // LANES):
        xg = acc[:, g * LANES:(g + 1) * LANES]
        outs.append(xg * cos + pltpu.roll(xg, LANES // 2, 1) * sin_signed)
    return outs[0] if len(outs) == 1 else jnp.concatenate(outs, axis=1)


def _epi_q(acc, extras, outs):
    r = _rope_tile(acc, extras[0][...], extras[1][...])
    outs[0][...] = (r * (HD_DIFF ** -0.5 * math.log2(math.e))).astype(BF16)


def _epi_k(acc, extras, outs):
    r = _rope_tile(acc, extras[0][...], extras[1][...])
    outs[0][...] = r
    outs[1][...] = r.astype(BF16)


def _epi_f32_bf16(acc, extras, outs):
    outs[0][...] = acc
    outs[1][...] = acc.astype(BF16)


def _epi_f32(acc, extras, outs):
    outs[0][...] = acc


def _lambda_value(lam_ref, lam_init):
    lq1 = lam_ref[0:1, :]
    lk1 = lam_ref[1:2, :]
    lq2 = lam_ref[2:3, :]
    lk2 = lam_ref[3:4, :]
    return (jnp.exp(jnp.sum(lq1 * lk1, axis=1, keepdims=True))
            - jnp.exp(jnp.sum(lq2 * lk2, axis=1, keepdims=True)) + lam_init)


def _diff_finalize(o1, o2, lam_ref, g_ref, lam_init):
    lam = _lambda_value(lam_ref, lam_init)
    o = o1 - lam * o2
    ms = jnp.mean(o * o, axis=1, keepdims=True)
    return (o * lax.rsqrt(ms + EPS)) * g_ref[...] * (1.0 - lam_init)


def _attn_prompt_kernel(qi_ref, kj_ref, q_ref, k_ref, v_ref, lam_ref, g_ref, o_ref,
                        m1, l1, a1, m2, l2, a2, *, tq, lam_init):
    p = pl.program_id(2)
    i = qi_ref[p]
    j = kj_ref[p]
    half_t = tq // 2

    @pl.when(j == 0)
    def _():
        m1[...] = jnp.full_like(m1, NEG_INF)
        m2[...] = jnp.full_like(m2, NEG_INF)
        l1[...] = jnp.zeros_like(l1)
        l2[...] = jnp.zeros_like(l2)
        a1[...] = jnp.zeros_like(a1)
        a2[...] = jnp.zeros_like(a2)

    def update(r0, nr, nc, masked):
        if nr > ROW_BLOCK:
            for rr in range(r0, r0 + nr, ROW_BLOCK):
                update(rr, ROW_BLOCK, nc, masked)
            return
        rows = slice(r0, r0 + nr)
        q = q_ref[rows, :]
        k = k_ref[0:nc, :]
        v = v_ref[0:nc, :]
        if masked:
            qc = (r0 + lax.broadcasted_iota(jnp.int32, (nr, nc), 0)) // CHUNK
            kc = lax.broadcasted_iota(jnp.int32, (nr, nc), 1) // CHUNK
            mask = qc >= kc
        for half, (m_ref, l_ref, a_ref) in enumerate(((m1, l1, a1), (m2, l2, a2))):
            qh = q[:, half * HD_DIFF:(half + 1) * HD_DIFF]
            kh = k[:, half * HD_DIFF:(half + 1) * HD_DIFF]
            s = lax.dot_general(qh, kh, NT_DIMS, preferred_element_type=F32)
            if masked:
                s = jnp.where(mask, s, NEG_INF)
            m_old = m_ref[rows, :]
            m_new = jnp.maximum(m_old, jnp.max(s, axis=1, keepdims=True))
            alpha = jnp.exp2(m_old - m_new)
            pr = jnp.exp2(s - m_new)
            l_ref[rows, :] = alpha * l_ref[rows, :] + jnp.sum(pr, axis=1, keepdims=True)
            a_ref[rows, :] = alpha * a_ref[rows, :] + jnp.dot(pr.astype(BF16), v, preferred_element_type=F32)
            m_ref[rows, :] = m_new

    @pl.when(j < i)
    def _():
        update(0, tq, tq, False)

    @pl.when(j == i)
    def _():
        if half_t % CHUNK == 0:
            update(0, half_t, half_t, True)
            update(half_t, half_t, tq, True)
        else:
            update(0, tq, tq, True)
        o = _diff_finalize(a1[...] / l1[...], a2[...] / l2[...], lam_ref, g_ref, lam_init)
        o_ref[...] = o.astype(o_ref.dtype)


def _attn_prompt(q, k, v, lam_vecs, g, batch, seq, lam_init):
    tq = _pick(seq, (1024, 512, 256, 128, 64))
    nq = seq // tq
    hw = 2 * HD_DIFF
    pairs = [(i, j) for i in range(nq) for j in range(i + 1)]
    qi = jnp.asarray([p[0] for p in pairs], jnp.int32)
    kj = jnp.asarray([p[1] for p in pairs], jnp.int32)

    def q_map(b, h, p, qi_ref, kj_ref):
        return (b * nq + qi_ref[p], h)

    def kv_map(b, h, p, qi_ref, kj_ref):
        return (b * nq + kj_ref[p], h)

    kern = functools.partial(_attn_prompt_kernel, tq=tq, lam_init=lam_init)
    est = 2 * 4 * tq * hw * 2 + 2 * tq * hw * 4 + 4 * tq * LANES * 4 + 8 * tq * tq * 4
    return pl.pallas_call(
        kern,
        grid_spec=pltpu.PrefetchScalarGridSpec(
            num_scalar_prefetch=2,
            grid=(batch, H_DIFF, len(pairs)),
            in_specs=[
                pl.BlockSpec((tq, hw), q_map),
                pl.BlockSpec((tq, hw), kv_map),
                pl.BlockSpec((tq, hw), kv_map),
                pl.BlockSpec((4, HD_DIFF), lambda b, h, p, qi_ref, kj_ref: (0, 0)),
                pl.BlockSpec((1, hw), lambda b, h, p, qi_ref, kj_ref: (0, 0)),
            ],
            out_specs=pl.BlockSpec((tq, hw), q_map),
            scratch_shapes=[
                pltpu.VMEM((tq, 1), F32), pltpu.VMEM((tq, 1), F32), pltpu.VMEM((tq, hw), F32),
                pltpu.VMEM((tq, 1), F32), pltpu.VMEM((tq, 1), F32), pltpu.VMEM((tq, hw), F32),
            ],
        ),
        out_shape=jax.ShapeDtypeStruct((batch * seq, DIFF_W), BF16),
        compiler_params=_params(("parallel", "parallel", "arbitrary"), est),
        name="diff_attn_prompt",
    )(qi, kj, q, k, v, lam_vecs, g)


def _attn_cached_kernel(q_ref, kn_ref, vn_ref, kc_ref, vc_ref, bc_ref, bn_ref, lam_ref, g_ref, o_ref,
                        *, past, lam_init):
    ts = q_ref.shape[0]
    hw = 2 * HD_DIFF
    kc = kc_ref[0].reshape(past * H_DIFF, hw).astype(BF16)
    vc = vc_ref[0].reshape(past * H_DIFF, hw).astype(BF16)
    kn = kn_ref[0].reshape(ts * H_DIFF, hw).astype(BF16)
    vn = vn_ref[0].reshape(ts * H_DIFF, hw).astype(BF16)
    halves = []
    for half in range(2):
        sl = slice(half * HD_DIFF, (half + 1) * HD_DIFF)
        qh = jnp.concatenate([q_ref[:, h * hw + half * HD_DIFF:h * hw + (half + 1) * HD_DIFF]
                              for h in range(H_DIFF)], axis=0)
        sc = lax.dot_general(kc[:, sl], qh, NT_DIMS, preferred_element_type=F32) + bc_ref[...]
        sn = lax.dot_general(kn[:, sl], qh, NT_DIMS, preferred_element_type=F32) + bn_ref[...]
        m = jnp.maximum(jnp.max(sc, axis=0, keepdims=True), jnp.max(sn, axis=0, keepdims=True))
        pc = jnp.exp2(sc - m)
        pn = jnp.exp2(sn - m)
        l = jnp.sum(pc, axis=0, keepdims=True) + jnp.sum(pn, axis=0, keepdims=True)
        acc = (jnp.dot(pc.T.astype(BF16), vc, preferred_element_type=F32)
               + jnp.dot(pn.T.astype(BF16), vn, preferred_element_type=F32))
        l_col = jnp.broadcast_to(l, (l.shape[1], l.shape[1])).T
        halves.append(acc / jnp.concatenate([l_col] * (hw // l.shape[1]), axis=1))
    for h in range(H_DIFF):
        rows = slice(h * ts, (h + 1) * ts)
        o = _diff_finalize(halves[0][rows], halves[1][rows], lam_ref, g_ref, lam_init)
        o_ref[:, h * hw:(h + 1) * hw] = o.astype(o_ref.dtype)


def _attn_cached(q, kn, vn, cache_k, cache_v, lam_vecs, g, batch, seq, lam_init):
    past = cache_k.shape[1]
    hw = 2 * HD_DIFF
    assert H_DIFF * seq == LANES
    col_head = jnp.arange(LANES) // seq
    col_chunk = (past + jnp.arange(LANES) % seq) // CHUNK

    def bias(key_pos):
        row_head = jnp.tile(jnp.arange(H_DIFF), key_pos.shape[0])
        row_chunk = jnp.repeat(key_pos, H_DIFF) // CHUNK
        ok = (row_head[:, None] == col_head[None, :]) & (col_chunk[None, :] >= row_chunk[:, None])
        return jnp.where(ok, 0.0, NEG_INF).astype(F32)

    bias_c = bias(jnp.arange(past))
    bias_n = bias(past + jnp.arange(seq))
    kern = functools.partial(_attn_cached_kernel, past=past, lam_init=lam_init)
    est = 2 * past * DIFF_W * 4 + 2 * past * DIFF_W * 2 + 8 * past * H_DIFF * LANES * 4
    once = pl.Buffered(1)
    cache_spec = pl.BlockSpec((1, past, H_DIFF, hw), lambda b: (b, 0, 0, 0), pipeline_mode=once)
    new_spec = pl.BlockSpec((1, seq, H_DIFF, hw), lambda b: (b, 0, 0, 0))
    row_spec = pl.BlockSpec((seq, DIFF_W), lambda b: (b, 0))
    return pl.pallas_call(
        kern,
        grid=(batch,),
        in_specs=[
            row_spec, new_spec, new_spec, cache_spec, cache_spec,
            pl.BlockSpec(bias_c.shape, lambda b: (0, 0), pipeline_mode=once),
            pl.BlockSpec(bias_n.shape, lambda b: (0, 0)),
            pl.BlockSpec((4, HD_DIFF), lambda b: (0, 0)),
            pl.BlockSpec((1, hw), lambda b: (0, 0)),
        ],
        out_specs=row_spec,
        out_shape=jax.ShapeDtypeStruct((batch * seq, DIFF_W), BF16),
        compiler_params=_params(("parallel",), est),
        name="diff_attn_cached",
    )(q, kn, vn, cache_k, cache_v, bias_c, bias_n, lam_vecs, g)


def _split3(x):
    hi = x.astype(BF16)
    r1 = x - hi.astype(F32)
    mid = r1.astype(BF16)
    lo = (r1 - mid.astype(F32)).astype(BF16)
    return hi, mid, lo


def _gla_kernel(*refs, chunk, n_chunks, has_s0):
    if has_s0:
        q_ref, k_ref, v_ref, r_ref, z_ref, w2_ref, bg_ref, gn_ref, s0_ref, o_ref, s_ref, st = refs
    else:
        q_ref, k_ref, v_ref, r_ref, z_ref, w2_ref, bg_ref, gn_ref, o_ref, s_ref, st = refs
    i = pl.program_id(2)

    @pl.when(i == 0)
    def _():
        if has_s0:
            st[...] = s0_ref[0, 0].T
        else:
            st[...] = jnp.zeros_like(st)

    scale = GLA_DK ** -0.5
    zz = jnp.dot(z_ref[...].astype(BF16), w2_ref[...], preferred_element_type=F32) + bg_ref[...]
    logf = jax.nn.log_sigmoid(zz) / GLA_TAU
    row = lax.broadcasted_iota(jnp.int32, (chunk, chunk), 0)
    col = lax.broadcasted_iota(jnp.int32, (chunk, chunk), 1)
    causal = row >= col
    tri = jnp.where(causal, 1.0, 0.0).astype(BF16)
    for c in range(n_chunks):
        rows = slice(c * chunk, (c + 1) * chunk)
        g = logf[rows]
        bcum = sum(jnp.dot(tri, part, preferred_element_type=F32) for part in _split3(g))
        b_last = bcum[chunk - 1:chunk, :]
        qc = q_ref[rows, :]
        kc = k_ref[rows, :]
        vc = v_ref[rows, :].astype(BF16)
        q_in = (qc * jnp.exp(bcum) * scale).astype(BF16)
        k_in = (kc * jnp.exp(-bcum)).astype(BF16)
        k_end = (kc * jnp.exp(b_last - bcum)).astype(BF16)
        a = lax.dot_general(q_in, k_in, NT_DIMS, preferred_element_type=F32)
        a = jnp.where(causal, a, 0.0)
        s_old = st[...]
        o = (jnp.dot(a.astype(BF16), vc, preferred_element_type=F32)
             + lax.dot_general(q_in, s_old.astype(BF16), NT_DIMS, preferred_element_type=F32))
        st[...] = s_old * jnp.exp(b_last) + lax.dot_general(vc, k_end, TN_DIMS, preferred_element_type=F32)
        ms = jnp.mean(o * o, axis=1, keepdims=True)
        on = (o * lax.rsqrt(ms + EPS)) * gn_ref[...]
        gate = r_ref[rows, :]
        o_ref[rows, :] = (on * (gate * jax.nn.sigmoid(gate))).astype(o_ref.dtype)

    @pl.when(i == pl.num_programs(2) - 1)
    def _():
        s_ref[0, 0] = st[...].T


def _gla(gp, gz, w2, bg, gn, s0, batch, seq, chunk):
    tb = chunk * max(1, min(8, seq // chunk))
    while seq % tb:
        tb -= chunk
    nb = seq // tb
    kq = GLA_QK // GLA_DK
    kv = (2 * GLA_QK) // GLA_DV
    kr = kv + GLA_V // GLA_DV
    in_specs = [
        pl.BlockSpec((tb, GLA_DK), lambda b, h, i: (b * nb + i, h)),
        pl.BlockSpec((tb, GLA_DK), lambda b, h, i: (b * nb + i, kq + h)),
        pl.BlockSpec((tb, GLA_DV), lambda b, h, i: (b * nb + i, kv + h)),
        pl.BlockSpec((tb, GLA_DV), lambda b, h, i: (b * nb + i, kr + h)),
        pl.BlockSpec((tb, LANES), lambda b, h, i: (b * nb + i, 0)),
        pl.BlockSpec((LANES, GLA_DK), lambda b, h, i: (0, h)),
        pl.BlockSpec((1, GLA_DK), lambda b, h, i: (0, h)),
        pl.BlockSpec((1, GLA_DV), lambda b, h, i: (0, 0)),
    ]
    args = [gp, gp, gp, gp, gz, w2, bg, gn]
    if s0 is not None:
        in_specs.append(pl.BlockSpec((1, 1, GLA_DK, GLA_DV), lambda b, h, i: (b, h, 0, 0)))
        args.append(s0)
    kern = functools.partial(_gla_kernel, chunk=chunk, n_chunks=tb // chunk, has_s0=s0 is not None)
    est = 2 * tb * (2 * GLA_DK + 2 * GLA_DV + LANES) * 4 + 2 * tb * GLA_DV * 2 + 7 * GLA_DK * GLA_DV * 4
    return pl.pallas_call(
        kern,
        grid=(batch, H_GLA, nb),
        in_specs=in_specs,
        out_specs=[
            pl.BlockSpec((tb, GLA_DV), lambda b, h, i: (b * nb + i, h)),
            pl.BlockSpec((1, 1, GLA_DK, GLA_DV), lambda b, h, i: (b, h, 0, 0)),
        ],
        out_shape=[
            jax.ShapeDtypeStruct((batch * seq, GLA_V), BF16),
            jax.ShapeDtypeStruct((batch, H_GLA, GLA_DK, GLA_DV), F32),
        ],
        scratch_shapes=[pltpu.VMEM((GLA_DV, GLA_DK), F32)],
        compiler_params=_params(("parallel", "parallel", "arbitrary"), est),
        name="gla",
    )(*args)


def _ln_kernel(x_ref, y_ref, g_ref, b_ref, *outs, alpha, n_plain):
    h = alpha * x_ref[...] + y_ref[...].astype(F32)
    mu = jnp.mean(h, axis=1, keepdims=True)
    d = h - mu
    var = jnp.mean(d * d, axis=1, keepdims=True)
    r = (d * lax.rsqrt(var + EPS)) * g_ref[...] + b_ref[...]
    for o in outs[:n_plain]:
        o[...] = r.astype(o.dtype)
    for o in outs[n_plain:]:
        o[...] = r.T.astype(o.dtype)


def _res_layernorm(x, y, g, b, alpha, out_dtypes, transposed_dtypes=()):
    m, d = x.shape
    tm = _pick(m, (128,)) if transposed_dtypes else _pick(m, (128, 64, 32, 16, 8))
    assert m % tm == 0
    sizes = [jnp.dtype(t).itemsize for t in (*out_dtypes, *transposed_dtypes)]
    est = 2 * tm * d * (x.dtype.itemsize + y.dtype.itemsize + sum(sizes)) + 5 * tm * d * 4
    return pl.pallas_call(
        functools.partial(_ln_kernel, alpha=alpha, n_plain=len(out_dtypes)),
        grid=(m // tm,),
        in_specs=[
            pl.BlockSpec((tm, d), lambda i: (i, 0)),
            pl.BlockSpec((tm, d), lambda i: (i, 0)),
            pl.BlockSpec((1, d), lambda i: (0, 0)),
            pl.BlockSpec((1, d), lambda i: (0, 0)),
        ],
        out_specs=[pl.BlockSpec((tm, d), lambda i: (i, 0)) for _ in out_dtypes]
        + [pl.BlockSpec((d, tm), lambda i: (0, i)) for _ in transposed_dtypes],
        out_shape=[jax.ShapeDtypeStruct((m, d), t) for t in out_dtypes]
        + [jax.ShapeDtypeStruct((d, m), t) for t in transposed_dtypes],
        compiler_params=_params(("parallel",), est),
        name="res_layernorm",
    )(x, y, g, b)


NOT_RANKED = float(N_KEYS - 1)


def _top_values(s, count):
    vals = []
    cur = s
    for t in range(count):
        m = jnp.max(cur, axis=0, keepdims=True)
        vals.append(m)
        if t + 1 < count:
            cur = jnp.where(cur == m, -jnp.inf, cur)
    return vals


def _peer_select_kernel(q_ref, k1_ref, k2_ref, cnt_ref, c1_ref, r2_ref, e2_ref, cand_ref):
    pairs = [(i, j) for i in range(PEER_TOPK) for j in range(PEER_TOPK) if (i + 1) * (j + 1) <= PEER_TOPK]
    for h in range(PEER_HEADS):
        base = h * 2 * PEER_HALF
        q1 = q_ref[:, base:base + PEER_HALF].astype(BF16)
        q2 = q_ref[:, base + PEER_HALF:base + 2 * PEER_HALF].astype(BF16)
        s1 = lax.dot_general(k1_ref[h], q1, NT_DIMS, preferred_element_type=F32)
        s2 = lax.dot_general(k2_ref[h], q2, NT_DIMS, preferred_element_type=F32)
        t1 = _top_values(s1, PEER_TOPK)
        t2 = _top_values(s2, PEER_TOPK)
        cand_ref[...] = jnp.full_like(cand_ref, -jnp.inf)
        for r, (i, j) in enumerate(pairs):
            cand_ref[r:r + 1, :] = t1[i] + t2[j]
        best = _top_values(cand_ref[...], PEER_TOPK)
        tau = best[PEER_TOPK - 1]
        z = sum(jnp.exp(bv - best[0]) for bv in best)
        counts = [sum(jnp.where(t1[i] + t2[j] >= tau, 1.0, 0.0) for j in range(PEER_TOPK) if (i, j) in pairs)
                  for i in range(PEER_TOPK)]
        cnt = jnp.zeros_like(s1)
        rank2 = jnp.full_like(s2, NOT_RANKED)
        for i in range(PEER_TOPK):
            cnt = jnp.where(s1 == t1[i], counts[i], cnt)
            rank2 = jnp.where(s2 == t2[i], float(i), rank2)
        cnt_ref[h] = cnt
        c1_ref[h] = jnp.exp(s1 - t1[0]) / z
        r2_ref[h] = rank2
        e2_ref[h] = jnp.exp(s2 - t2[0])


def _peer_select(q, keys1, keys2):
    m = q.shape[0]
    tm = _pick(m, (256, 128))
    assert m % tm == 0
    n_pairs = sum((i + 1) * (j + 1) <= PEER_TOPK for i in range(PEER_TOPK) for j in range(PEER_TOPK))
    cand_rows = -(-n_pairs // 8) * 8
    spec = pl.BlockSpec((PEER_HEADS, N_KEYS, tm), lambda i: (0, 0, i))
    est = 2 * tm * q.shape[1] * 4 + 2 * 4 * PEER_HEADS * N_KEYS * tm * 4 + (8 << 20)
    return pl.pallas_call(
        _peer_select_kernel,
        grid=(m // tm,),
        in_specs=[
            pl.BlockSpec((tm, q.shape[1]), lambda i: (i, 0)),
            pl.BlockSpec(keys1.shape, lambda i: (0, 0, 0)),
            pl.BlockSpec(keys2.shape, lambda i: (0, 0, 0)),
        ],
        out_specs=[spec, spec, spec, spec],
        out_shape=[jax.ShapeDtypeStruct((PEER_HEADS, N_KEYS, m), F32)] * 4,
        scratch_shapes=[pltpu.VMEM((cand_rows, tm), F32)],
        compiler_params=_params(("parallel",), est),
        name="peer_select",
    )(q, keys1, keys2)


def _peer_mix_kernel(xt_ref, u_ref, v_ref, cnt_ref, c1_ref, r2_ref, e2_ref, o_ref,
                     y_ref, coef_a, coef_b, act_ref, *, te, n_blocks):
    j = pl.program_id(1)
    d, tm = xt_ref.shape
    groups = te // N_KEYS
    jb = jnp.minimum(j, n_blocks - 1)
    pieces = [(g, c) for c in range(tm // LANES) for g in range(groups)]
    col_w = 2 * LANES
    n_vchunks = d // col_w

    @pl.when(j == 0)
    def _():
        y_ref[...] = jnp.zeros_like(y_ref)
        coef_b[...] = jnp.zeros_like(coef_b)

    def select_weights(g, c):
        a = jb * groups + g
        lanes = slice(c * LANES, (c + 1) * LANES)
        cnt = [jnp.broadcast_to(cnt_ref[h, pl.ds(a, 1), :][:, lanes], (SUBLANES, LANES)) for h in range(PEER_HEADS)]
        c1a = [jnp.broadcast_to(c1_ref[h, pl.ds(a, 1), :][:, lanes], (SUBLANES, LANES)) for h in range(PEER_HEADS)]
        out = []
        for r in range(N_KEYS // SUBLANES):
            bs = slice(r * SUBLANES, (r + 1) * SUBLANES)
            acc = None
            for h in range(PEER_HEADS):
                wh = jnp.where(r2_ref[h, bs, lanes] < cnt[h], e2_ref[h, bs, lanes], 0.0) * c1a[h]
                acc = wh if acc is None else acc + wh
            out.append(acc)
        return jnp.concatenate(out, axis=0)

    def finish_piece(cur, g, c):
        rows = slice(g * N_KEYS, (g + 1) * N_KEYS)
        lanes = slice(c * LANES, (c + 1) * LANES)
        x = act_ref[rows, lanes]
        x = 0.5 * x * (1.0 + lax.erf(x * (2.0 ** -0.5)))
        cur[lanes, rows] = (x * select_weights(g, c)).T.astype(BF16)

    def step(cur, prev):
        act_ref[...] = jnp.dot(u_ref[...], xt_ref[...], preferred_element_type=F32)
        for n in range(n_vchunks):
            cs = slice(n * col_w, (n + 1) * col_w)
            y_ref[:, cs] += jnp.dot(prev[...], v_ref[:, cs], preferred_element_type=F32)
            for g, c in pieces[n::n_vchunks]:
                finish_piece(cur, g, c)

    @pl.when(j % 2 == 0)
    def _():
        step(coef_a, coef_b)

    @pl.when(j % 2 == 1)
    def _():
        step(coef_b, coef_a)

    @pl.when(j == n_blocks)
    def _():
        o_ref[...] = y_ref[...].astype(o_ref.dtype)


def _peer_mix(xt, u, v, cnt, c1, r2, e2):
    d, m = xt.shape
    ne = v.shape[0]
    tm = _pick(m, (512, 256, 128))
    te = 512
    assert m % tm == 0 and tm % LANES == 0 and ne % te == 0 and te % N_KEYS == 0
    nj = ne // te
    once = pl.Buffered(1)
    sel_spec = pl.BlockSpec((PEER_HEADS, N_KEYS, tm), lambda i, j: (0, 0, i), pipeline_mode=once)
    est = (tm * d * 2 + 2 * 2 * te * d * 2 + 4 * PEER_HEADS * N_KEYS * tm * 4
           + 2 * tm * d * 4 + 2 * tm * te * 2 + 6 * te * tm * 4)
    return pl.pallas_call(
        functools.partial(_peer_mix_kernel, te=te, n_blocks=nj),
        grid=(m // tm, nj + 1),
        in_specs=[
            pl.BlockSpec((d, tm), lambda i, j: (0, i), pipeline_mode=once),
            pl.BlockSpec((te, d), lambda i, j: (jnp.minimum(j, nj - 1), 0)),
            pl.BlockSpec((te, d), lambda i, j: (jnp.maximum(j - 1, 0), 0)),
            sel_spec, sel_spec, sel_spec, sel_spec,
        ],
        out_specs=pl.BlockSpec((tm, d), lambda i, j: (i, 0)),
        out_shape=jax.ShapeDtypeStruct((m, d), BF16),
        scratch_shapes=[pltpu.VMEM((tm, d), F32), pltpu.VMEM((tm, te), BF16), pltpu.VMEM((tm, te), BF16),
                        pltpu.VMEM((te, tm), F32)],
        compiler_params=_params(("parallel", "arbitrary"), est),
        name="peer_mix",
    )(xt, u, v, cnt, c1, r2, e2)


def _rope_tables(pos, batch):
    inv = ROPE_THETA ** (-jnp.arange(0, HD_DIFF, 2, dtype=F32) / HD_DIFF)
    ang = pos.astype(F32)[:, None] * inv[None, :]
    ang = jnp.concatenate([ang, ang], axis=-1)
    sign = jnp.where(jnp.arange(HD_DIFF) < HD_DIFF // 2, -1.0, 1.0).astype(F32)
    cos = jnp.tile(jnp.cos(ang), (batch, 1))
    sin_signed = jnp.tile(jnp.sin(ang) * sign[None, :], (batch, 1))
    return cos, sin_signed


def _prep_weights(w_in, w_gate2, b_gate, lam_q1, lam_k1, lam_q2, lam_k2, diff_norm_g, gla_norm_g, w_out,
                  ln1_g, ln1_b, ln2_g, ln2_b, peer_wq, peer_keys1, peer_keys2, peer_u, peer_v):
    wz = jnp.pad(w_in[:, IN_GATE:IN_GATE + GATE_RANK], ((0, 0), (0, LANES - GATE_RANK)))
    return dict(
        w_in=w_in.astype(BF16),
        wz=wz.astype(BF16),
        w2=jnp.pad(w_gate2, ((0, LANES - GATE_RANK), (0, 0))).astype(BF16),
        bg=b_gate.reshape(1, GLA_QK),
        lam=jnp.stack([lam_q1, lam_k1, lam_q2, lam_k2]),
        dng=diff_norm_g.reshape(1, 2 * HD_DIFF),
        gng=gla_norm_g.reshape(1, GLA_DV),
        wo_a=w_out[:DIFF_W].astype(BF16),
        wo_b=w_out[DIFF_W:].astype(BF16),
        ln1_g=ln1_g.reshape(1, -1), ln1_b=ln1_b.reshape(1, -1),
        ln2_g=ln2_g.reshape(1, -1), ln2_b=ln2_b.reshape(1, -1),
        pwq=peer_wq.astype(BF16),
        pk1=peer_keys1.astype(BF16), pk2=peer_keys2.astype(BF16),
        pu=peer_u.astype(BF16), pv=peer_v.astype(BF16),
    )


def _layer(x, pos, past_k, past_v, s0, chunk, lam_init, alpha, w):
    batch, seq, d = x.shape
    m = batch * seq
    x2d = x.reshape(m, d)
    xb = x2d.astype(BF16)
    cos, sin_signed = _rope_tables(pos, batch)

    proj = [(xb, w["w_in"])]
    (q,) = _matmul(proj, _epi_q, [BF16], (cos, sin_signed), name="proj_q", cols=(0, DIFF_W))
    k32, kb = _matmul(proj, _epi_k, [F32, BF16], (cos, sin_signed), name="proj_k", cols=(DIFF_W, DIFF_W))
    v32, vb = _matmul(proj, _epi_f32_bf16, [F32, BF16], name="proj_v", cols=(2 * DIFF_W, DIFF_W))
    (gp,) = _matmul(proj, _epi_f32, [F32], name="proj_gla", cols=(IN_GLA, IN_GATE - IN_GLA))
    (gz,) = _matmul([(xb, w["wz"])], _epi_f32, [F32], tn=LANES, name="proj_gate")

    new_k = k32.reshape(batch, seq, H_DIFF, 2 * HD_DIFF)
    new_v = v32.reshape(batch, seq, H_DIFF, 2 * HD_DIFF)
    if past_k is None:
        o_diff = _attn_prompt(q, kb, vb, w["lam"], w["dng"], batch, seq, lam_init)
    else:
        o_diff = _attn_cached(q, new_k, new_v, past_k, past_v, w["lam"], w["dng"], batch, seq, lam_init)
    o_gla, s_new = _gla(gp, gz, w["w2"], w["bg"], w["gng"], s0, batch, seq, chunk)

    (mix,) = _matmul([(o_diff, w["wo_a"]), (o_gla, w["wo_b"])], _epi_f32, [F32], tn=1024, name="out_proj")
    x1, x1b, x1t = _res_layernorm(x2d, mix, w["ln1_g"], w["ln1_b"], alpha, [F32, BF16], [BF16])

    (pq,) = _matmul([(x1b, w["pwq"])], _epi_f32, [F32], name="peer_query")
    cnt, c1, r2, e2 = _peer_select(pq, w["pk1"], w["pk2"])
    y = _peer_mix(x1t, w["pu"], w["pv"], cnt, c1, r2, e2)
    (x2,) = _res_layernorm(x1, y, w["ln2_g"], w["ln2_b"], alpha, [F32])
    return x2.reshape(batch, seq, d), new_k, new_v, s_new


def kernel(x_prompt, x_sample, cache_diff_k, cache_diff_v, state_gla, w_in, w_gate2, b_gate, lam_q1, lam_k1, lam_q2, lam_k2, diff_norm_g, gla_norm_g, w_out, ln1_g, ln1_b, ln2_g, ln2_b, peer_wq, peer_keys1, peer_keys2, peer_u, peer_v):
    depth = w_in.shape[0]
    tp = x_prompt.shape[1]
    ts = x_sample.shape[1]
    past = cache_diff_k.shape[2]
    alpha = (2 * depth) ** 0.25
    pos_p = jnp.arange(tp, dtype=jnp.int32)
    pos_s = past + jnp.arange(ts, dtype=jnp.int32)
    y_p, y_s = x_prompt, x_sample
    outs = [[] for _ in range(6)]
    for l in range(depth):
        lam_init = 0.8 - 0.6 * math.exp(-0.3 * l)
        w = _prep_weights(w_in[l], w_gate2[l], b_gate[l], lam_q1[l], lam_k1[l], lam_q2[l], lam_k2[l],
                          diff_norm_g[l], gla_norm_g[l], w_out[l], ln1_g[l], ln1_b[l], ln2_g[l], ln2_b[l],
                          peer_wq[l], peer_keys1[l], peer_keys2[l], peer_u[l], peer_v[l])
        y_p, kp, vp, sp = _layer(y_p, pos_p, None, None, None, CHUNK, lam_init, alpha, w)
        y_s, k_s, v_s, s_s = _layer(y_s, pos_s, cache_diff_k[l], cache_diff_v[l], state_gla[l],
                                    ts, lam_init, alpha, w)
        for lst, val in zip(outs, (kp, vp, sp, k_s, v_s, s_s)):
            lst.append(val)
    return (y_p, y_s) + tuple(jnp.stack(o) for o in outs)
```

```python
import functools
import math

import jax
import jax.numpy as jnp
from jax import lax
from jax.experimental import pallas as pl
from jax.experimental.pallas import tpu as pltpu

CHUNK = 64
HD_DIFF = 128
H_DIFF = 8
H_GLA = 4
GLA_DK = 256
GLA_DV = 512
GATE_RANK = 16
GLA_TAU = 16.0
ROPE_THETA = 10000.0
N_KEYS = 128
PEER_HEADS = 8
PEER_HALF = 128
PEER_TOPK = 16
EPS = 1e-5
NEG_INF = -1e30
ROW_BLOCK = 256

DIFF_W = H_DIFF * 2 * HD_DIFF
GLA_QK = H_GLA * GLA_DK
GLA_V = H_GLA * GLA_DV
IN_GLA = 3 * DIFF_W
IN_GATE = IN_GLA + 2 * GLA_QK + 2 * GLA_V

LANES = 128
SUBLANES = 8
BF16_ROWS = 2 * SUBLANES
VMEM_CAP = 60000 * 1024

BF16 = jnp.bfloat16
F32 = jnp.float32

NT_DIMS = (((1,), (1,)), ((), ()))
TN_DIMS = (((0,), (0,)), ((), ()))


def _pick(n, cands):
    for c in cands:
        if n % c == 0:
            return c
    return n


def _params(sem, est_bytes):
    limit = int(min(VMEM_CAP, max(32 * 1024 * 1024, est_bytes * 5 // 4)))
    return pltpu.CompilerParams(dimension_semantics=sem, vmem_limit_bytes=limit)


def _mm_kernel(*refs, n_pairs, n_extra, epilogue):
    a_refs = refs[:n_pairs]
    b_refs = refs[n_pairs:2 * n_pairs]
    extras = refs[2 * n_pairs:2 * n_pairs + n_extra]
    outs = refs[2 * n_pairs + n_extra:]
    acc = jnp.dot(a_refs[0][...], b_refs[0][...], preferred_element_type=F32)
    for a, b in zip(a_refs[1:], b_refs[1:]):
        acc = acc + jnp.dot(a[...], b[...], preferred_element_type=F32)
    epilogue(acc, extras, outs)


def _matmul(pairs, epilogue, out_dtypes, row_extras=(), tn=512, name="matmul", cols=None):
    m = pairs[0][0].shape[0]
    c0, n = cols if cols is not None else (0, pairs[0][1].shape[1])
    tm = _pick(m, (1024, 512, 256, 128, 64, 32, 16, 8))
    tn = min(tn, n)
    assert n % tn == 0 and c0 % tn == 0
    jb = c0 // tn
    in_specs = []
    est = 0
    for a, _ in pairs:
        in_specs.append(pl.BlockSpec((tm, a.shape[1]), lambda i, j: (i, 0)))
        est += 2 * tm * a.shape[1] * a.dtype.itemsize
    for _, b in pairs:
        in_specs.append(pl.BlockSpec((b.shape[0], tn), lambda i, j: (0, jb + j)))
        est += 2 * b.shape[0] * tn * b.dtype.itemsize
    for e in row_extras:
        in_specs.append(pl.BlockSpec((tm, e.shape[1]), lambda i, j: (i, 0)))
        est += 2 * tm * e.shape[1] * 4
    out_specs = [pl.BlockSpec((tm, tn), lambda i, j: (i, j)) for _ in out_dtypes]
    out_shape = [jax.ShapeDtypeStruct((m, n), d) for d in out_dtypes]
    est += sum(2 * tm * tn * jnp.dtype(d).itemsize for d in out_dtypes) + 3 * tm * tn * 4
    kern = functools.partial(_mm_kernel, n_pairs=len(pairs), n_extra=len(row_extras), epilogue=epilogue)
    return pl.pallas_call(
        kern,
        grid=(m // tm, n // tn),
        in_specs=in_specs,
        out_specs=out_specs,
        out_shape=out_shape,
        compiler_params=_params(("parallel", "arbitrary"), est),
        name=name,
    )(*[a for a, _ in pairs], *[b for _, b in pairs], *row_extras)


def _rope_tile(acc, cos, sin_signed):
    outs = []
    for g in range(acc.shape[1] // LANES):
        xg = acc[:, g * LANES:(g + 1) * LANES]
        outs.append(xg * cos + pltpu.roll(xg, LANES // 2, 1) * sin_signed)
    return outs[0] if len(outs) == 1 else jnp.concatenate(outs, axis=1)


def _epi_q(acc, extras, outs):
    r = _rope_tile(acc, extras[0][...], extras[1][...])
    outs[0][...] = (r * (HD_DIFF ** -0.5 * math.log2(math.e))).astype(BF16)


def _epi_k(acc, extras, outs):
    r = _rope_tile(acc, extras[0][...], extras[1][...])
    outs[0][...] = r
    outs[1][...] = r.astype(BF16)


def _epi_f32_bf16(acc, extras, outs):
    outs[0][...] = acc
    outs[1][...] = acc.astype(BF16)


def _epi_f32(acc, extras, outs):
    outs[0][...] = acc


def _lambda_value(lam_ref, lam_init):
    lq1 = lam_ref[0:1, :]
    lk1 = lam_ref[1:2, :]
    lq2 = lam_ref[2:3, :]
    lk2 = lam_ref[3:4, :]
    return (jnp.exp(jnp.sum(lq1 * lk1, axis=1, keepdims=True))
            - jnp.exp(jnp.sum(lq2 * lk2, axis=1, keepdims=True)) + lam_init)


def _diff_finalize(o1, o2, lam_ref, g_ref, lam_init):
    lam = _lambda_value(lam_ref, lam_init)
    o = o1 - lam * o2
    ms = jnp.mean(o * o, axis=1, keepdims=True)
    return (o * lax.rsqrt(ms + EPS)) * g_ref[...] * (1.0 - lam_init)


def _attn_prompt_kernel(qi_ref, kj_ref, q_ref, k_ref, v_ref, lam_ref, g_ref, o_ref,
                        m1, l1, a1, m2, l2, a2, *, tq, lam_init):
    p = pl.program_id(2)
    i = qi_ref[p]
    j = kj_ref[p]
    half_t = tq // 2

    @pl.when(j == 0)
    def _():
        m1[...] = jnp.full_like(m1, NEG_INF)
        m2[...] = jnp.full_like(m2, NEG_INF)
        l1[...] = jnp.zeros_like(l1)
        l2[...] = jnp.zeros_like(l2)
        a1[...] = jnp.zeros_like(a1)
        a2[...] = jnp.zeros_like(a2)

    def update(r0, nr, nc, masked):
        if nr > ROW_BLOCK:
            for rr in range(r0, r0 + nr, ROW_BLOCK):
                update(rr, ROW_BLOCK, nc, masked)
            return
        rows = slice(r0, r0 + nr)
        q = q_ref[rows, :]
        k = k_ref[0:nc, :]
        v = v_ref[0:nc, :]
        if masked:
            qc = (r0 + lax.broadcasted_iota(jnp.int32, (nr, nc), 0)) // CHUNK
            kc = lax.broadcasted_iota(jnp.int32, (nr, nc), 1) // CHUNK
            mask = qc >= kc
        for half, (m_ref, l_ref, a_ref) in enumerate(((m1, l1, a1), (m2, l2, a2))):
            qh = q[:, half * HD_DIFF:(half + 1) * HD_DIFF]
            kh = k[:, half * HD_DIFF:(half + 1) * HD_DIFF]
            s = lax.dot_general(qh, kh, NT_DIMS, preferred_element_type=F32)
            if masked:
                s = jnp.where(mask, s, NEG_INF)
            m_old = m_ref[rows, :]
            m_new = jnp.maximum(m_old, jnp.max(s, axis=1, keepdims=True))
            alpha = jnp.exp2(m_old - m_new)
            pr = jnp.exp2(s - m_new)
            l_ref[rows, :] = alpha * l_ref[rows, :] + jnp.sum(pr, axis=1, keepdims=True)
            a_ref[rows, :] = alpha * a_ref[rows, :] + jnp.dot(pr.astype(BF16), v, preferred_element_type=F32)
            m_ref[rows, :] = m_new

    @pl.when(j < i)
    def _():
        update(0, tq, tq, False)

    @pl.when(j == i)
    def _():
        if half_t % CHUNK == 0:
            update(0, half_t, half_t, True)
            update(half_t, half_t, tq, True)
        else:
            update(0, tq, tq, True)
        o = _diff_finalize(a1[...] / l1[...], a2[...] / l2[...], lam_ref, g_ref, lam_init)
        o_ref[...] = o.astype(o_ref.dtype)


def _attn_prompt(q, k, v, lam_vecs, g, batch, seq, lam_init):
    tq = _pick(seq, (1024, 512, 256, 128, 64))
    nq = seq // tq
    hw = 2 * HD_DIFF
    pairs = [(i, j) for i in range(nq) for j in range(i + 1)]
    qi = jnp.asarray([p[0] for p in pairs], jnp.int32)
    kj = jnp.asarray([p[1] for p in pairs], jnp.int32)

    def q_map(b, h, p, qi_ref, kj_ref):
        return (b * nq + qi_ref[p], h)

    def kv_map(b, h, p, qi_ref, kj_ref):
        return (b * nq + kj_ref[p], h)

    kern = functools.partial(_attn_prompt_kernel, tq=tq, lam_init=lam_init)
    est = 2 * 4 * tq * hw * 2 + 2 * tq * hw * 4 + 4 * tq * LANES * 4 + 8 * tq * tq * 4
    return pl.pallas_call(
        kern,
        grid_spec=pltpu.PrefetchScalarGridSpec(
            num_scalar_prefetch=2,
            grid=(batch, H_DIFF, len(pairs)),
            in_specs=[
                pl.BlockSpec((tq, hw), q_map),
                pl.BlockSpec((tq, hw), kv_map),
                pl.BlockSpec((tq, hw), kv_map),
                pl.BlockSpec((4, HD_DIFF), lambda b, h, p, qi_ref, kj_ref: (0, 0)),
                pl.BlockSpec((1, hw), lambda b, h, p, qi_ref, kj_ref: (0, 0)),
            ],
            out_specs=pl.BlockSpec((tq, hw), q_map),
            scratch_shapes=[
                pltpu.VMEM((tq, 1), F32), pltpu.VMEM((tq, 1), F32), pltpu.VMEM((tq, hw), F32),
                pltpu.VMEM((tq, 1), F32), pltpu.VMEM((tq, 1), F32), pltpu.VMEM((tq, hw), F32),
            ],
        ),
        out_shape=jax.ShapeDtypeStruct((batch * seq, DIFF_W), BF16),
        compiler_params=_params(("parallel", "parallel", "arbitrary"), est),
        name="diff_attn_prompt",
    )(qi, kj, q, k, v, lam_vecs, g)


def _attn_cached_kernel(q_ref, kn_ref, vn_ref, kc_ref, vc_ref, bc_ref, bn_ref, lam_ref, g_ref, o_ref,
                        *, past, lam_init):
    ts = q_ref.shape[0]
    hw = 2 * HD_DIFF
    kc = kc_ref[0].reshape(past * H_DIFF, hw).astype(BF16)
    vc = vc_ref[0].reshape(past * H_DIFF, hw).astype(BF16)
    kn = kn_ref[0].reshape(ts * H_DIFF, hw).astype(BF16)
    vn = vn_ref[0].reshape(ts * H_DIFF, hw).astype(BF16)
    halves = []
    for half in range(2):
        sl = slice(half * HD_DIFF, (half + 1) * HD_DIFF)
        qh = jnp.concatenate([q_ref[:, h * hw + half * HD_DIFF:h * hw + (half + 1) * HD_DIFF]
                              for h in range(H_DIFF)], axis=0)
        sc = lax.dot_general(kc[:, sl], qh, NT_DIMS, preferred_element_type=F32) + bc_ref[...]
        sn = lax.dot_general(kn[:, sl], qh, NT_DIMS, preferred_element_type=F32) + bn_ref[...]
        m = jnp.maximum(jnp.max(sc, axis=0, keepdims=True), jnp.max(sn, axis=0, keepdims=True))
        pc = jnp.exp2(sc - m)
        pn = jnp.exp2(sn - m)
        l = jnp.sum(pc, axis=0, keepdims=True) + jnp.sum(pn, axis=0, keepdims=True)
        acc = (jnp.dot(pc.T.astype(BF16), vc, preferred_element_type=F32)
               + jnp.dot(pn.T.astype(BF16), vn, preferred_element_type=F32))
        l_col = jnp.broadcast_to(l, (l.shape[1], l.shape[1])).T
        halves.append(acc / jnp.concatenate([l_col] * (hw // l.shape[1]), axis=1))
    for h in range(H_DIFF):
        rows = slice(h * ts, (h + 1) * ts)
        o = _diff_finalize(halves[0][rows], halves[1][rows], lam_ref, g_ref, lam_init)
        o_ref[:, h * hw:(h + 1) * hw] = o.astype(o_ref.dtype)


def _attn_cached(q, kn, vn, cache_k, cache_v, lam_vecs, g, batch, seq, lam_init):
    past = cache_k.shape[1]
    hw = 2 * HD_DIFF
    assert H_DIFF * seq == LANES
    col_head = jnp.arange(LANES) // seq
    col_chunk = (past + jnp.arange(LANES) % seq) // CHUNK

    def bias(key_pos):
        row_head = jnp.tile(jnp.arange(H_DIFF), key_pos.shape[0])
        row_chunk = jnp.repeat(key_pos, H_DIFF) // CHUNK
        ok = (row_head[:, None] == col_head[None, :]) & (col_chunk[None, :] >= row_chunk[:, None])
        return jnp.where(ok, 0.0, NEG_INF).astype(F32)

    bias_c = bias(jnp.arange(past))
    bias_n = bias(past + jnp.arange(seq))
    kern = functools.partial(_attn_cached_kernel, past=past, lam_init=lam_init)
    est = 2 * past * DIFF_W * 4 + 2 * past * DIFF_W * 2 + 8 * past * H_DIFF * LANES * 4
    once = pl.Buffered(1)
    cache_spec = pl.BlockSpec((1, past, H_DIFF, hw), lambda b: (b, 0, 0, 0), pipeline_mode=once)
    new_spec = pl.BlockSpec((1, seq, H_DIFF, hw), lambda b: (b, 0, 0, 0))
    row_spec = pl.BlockSpec((seq, DIFF_W), lambda b: (b, 0))
    return pl.pallas_call(
        kern,
        grid=(batch,),
        in_specs=[
            row_spec, new_spec, new_spec, cache_spec, cache_spec,
            pl.BlockSpec(bias_c.shape, lambda b: (0, 0), pipeline_mode=once),
            pl.BlockSpec(bias_n.shape, lambda b: (0, 0)),
            pl.BlockSpec((4, HD_DIFF), lambda b: (0, 0)),
            pl.BlockSpec((1, hw), lambda b: (0, 0)),
        ],
        out_specs=row_spec,
        out_shape=jax.ShapeDtypeStruct((batch * seq, DIFF_W), BF16),
        compiler_params=_params(("parallel",), est),
        name="diff_attn_cached",
    )(q, kn, vn, cache_k, cache_v, bias_c, bias_n, lam_vecs, g)


def _split3(x):
    hi = x.astype(BF16)
    r1 = x - hi.astype(F32)
    mid = r1.astype(BF16)
    lo = (r1 - mid.astype(F32)).astype(BF16)
    return hi, mid, lo


def _gla_kernel(*refs, chunk, n_chunks, has_s0):
    if has_s0:
        q_ref, k_ref, v_ref, r_ref, z_ref, w2_ref, bg_ref, gn_ref, s0_ref, o_ref, s_ref, st = refs
    else:
        q_ref, k_ref, v_ref, r_ref, z_ref, w2_ref, bg_ref, gn_ref, o_ref, s_ref, st = refs
    i = pl.program_id(2)

    @pl.when(i == 0)
    def _():
        if has_s0:
            st[...] = s0_ref[0, 0].T
        else:
            st[...] = jnp.zeros_like(st)

    scale = GLA_DK ** -0.5
    zz = jnp.dot(z_ref[...].astype(BF16), w2_ref[...], preferred_element_type=F32) + bg_ref[...]
    logf = jax.nn.log_sigmoid(zz) / GLA_TAU
    row = lax.broadcasted_iota(jnp.int32, (chunk, chunk), 0)
    col = lax.broadcasted_iota(jnp.int32, (chunk, chunk), 1)
    causal = row >= col
    tri = jnp.where(causal, 1.0, 0.0).astype(BF16)
    for c in range(n_chunks):
        rows = slice(c * chunk, (c + 1) * chunk)
        g = logf[rows]
        bcum = sum(jnp.dot(tri, part, preferred_element_type=F32) for part in _split3(g))
        b_last = bcum[chunk - 1:chunk, :]
        qc = q_ref[rows, :]
        kc = k_ref[rows, :]
        vc = v_ref[rows, :].astype(BF16)
        q_in = (qc * jnp.exp(bcum) * scale).astype(BF16)
        k_in = (kc * jnp.exp(-bcum)).astype(BF16)
        k_end = (kc * jnp.exp(b_last - bcum)).astype(BF16)
        a = lax.dot_general(q_in, k_in, NT_DIMS, preferred_element_type=F32)
        a = jnp.where(causal, a, 0.0)
        s_old = st[...]
        o = (jnp.dot(a.astype(BF16), vc, preferred_element_type=F32)
             + lax.dot_general(q_in, s_old.astype(BF16), NT_DIMS, preferred_element_type=F32))
        st[...] = s_old * jnp.exp(b_last) + lax.dot_general(vc, k_end, TN_DIMS, preferred_element_type=F32)
        ms = jnp.mean(o * o, axis=1, keepdims=True)
        on = (o * lax.rsqrt(ms + EPS)) * gn_ref[...]
        gate = r_ref[rows, :]
        o_ref[rows, :] = (on * (gate * jax.nn.sigmoid(gate))).astype(o_ref.dtype)

    @pl.when(i == pl.num_programs(2) - 1)
    def _():
        s_ref[0, 0] = st[...].T


def _gla(gp, gz, w2, bg, gn, s0, batch, seq, chunk):
    tb = chunk * max(1, min(8, seq // chunk))
    while seq % tb:
        tb -= chunk
    nb = seq // tb
    kq = GLA_QK // GLA_DK
    kv = (2 * GLA_QK) // GLA_DV
    kr = kv + GLA_V // GLA_DV
    in_specs = [
        pl.BlockSpec((tb, GLA_DK), lambda b, h, i: (b * nb + i, h)),
        pl.BlockSpec((tb, GLA_DK), lambda b, h, i: (b * nb + i, kq + h)),
        pl.BlockSpec((tb, GLA_DV), lambda b, h, i: (b * nb + i, kv + h)),
        pl.BlockSpec((tb, GLA_DV), lambda b, h, i: (b * nb + i, kr + h)),
        pl.BlockSpec((tb, LANES), lambda b, h, i: (b * nb + i, 0)),
        pl.BlockSpec((LANES, GLA_DK), lambda b, h, i: (0, h)),
        pl.BlockSpec((1, GLA_DK), lambda b, h, i: (0, h)),
        pl.BlockSpec((1, GLA_DV), lambda b, h, i: (0, 0)),
    ]
    args = [gp, gp, gp, gp, gz, w2, bg, gn]
    if s0 is not None:
        in_specs.append(pl.BlockSpec((1, 1, GLA_DK, GLA_DV), lambda b, h, i: (b, h, 0, 0)))
        args.append(s0)
    kern = functools.partial(_gla_kernel, chunk=chunk, n_chunks=tb // chunk, has_s0=s0 is not None)
    est = 2 * tb * (2 * GLA_DK + 2 * GLA_DV + LANES) * 4 + 2 * tb * GLA_DV * 2 + 7 * GLA_DK * GLA_DV * 4
    return pl.pallas_call(
        kern,
        grid=(batch, H_GLA, nb),
        in_specs=in_specs,
        out_specs=[
            pl.BlockSpec((tb, GLA_DV), lambda b, h, i: (b * nb + i, h)),
            pl.BlockSpec((1, 1, GLA_DK, GLA_DV), lambda b, h, i: (b, h, 0, 0)),
        ],
        out_shape=[
            jax.ShapeDtypeStruct((batch * seq, GLA_V), BF16),
            jax.ShapeDtypeStruct((batch, H_GLA, GLA_DK, GLA_DV), F32),
        ],
        scratch_shapes=[pltpu.VMEM((GLA_DV, GLA_DK), F32)],
        compiler_params=_params(("parallel", "parallel", "arbitrary"), est),
        name="gla",
    )(*args)


def _ln_kernel(x_ref, y_ref, g_ref, b_ref, *outs, alpha, n_plain):
    h = alpha * x_ref[...] + y_ref[...].astype(F32)
    mu = jnp.mean(h, axis=1, keepdims=True)
    d = h - mu
    var = jnp.mean(d * d, axis=1, keepdims=True)
    r = (d * lax.rsqrt(var + EPS)) * g_ref[...] + b_ref[...]
    for o in outs[:n_plain]:
        o[...] = r.astype(o.dtype)
    for o in outs[n_plain:]:
        o[...] = r.T.astype(o.dtype)


def _res_layernorm(x, y, g, b, alpha, out_dtypes, transposed_dtypes=()):
    m, d = x.shape
    tm = _pick(m, (128,)) if transposed_dtypes else _pick(m, (128, 64, 32, 16, 8))
    assert m % tm == 0
    sizes = [jnp.dtype(t).itemsize for t in (*out_dtypes, *transposed_dtypes)]
    est = 2 * tm * d * (x.dtype.itemsize + y.dtype.itemsize + sum(sizes)) + 5 * tm * d * 4
    return pl.pallas_call(
        functools.partial(_ln_kernel, alpha=alpha, n_plain=len(out_dtypes)),
        grid=(m // tm,),
        in_specs=[
            pl.BlockSpec((tm, d), lambda i: (i, 0)),
            pl.BlockSpec((tm, d), lambda i: (i, 0)),
            pl.BlockSpec((1, d), lambda i: (0, 0)),
            pl.BlockSpec((1, d), lambda i: (0, 0)),
        ],
        out_specs=[pl.BlockSpec((tm, d), lambda i: (i, 0)) for _ in out_dtypes]
        + [pl.BlockSpec((d, tm), lambda i: (0, i)) for _ in transposed_dtypes],
        out_shape=[jax.ShapeDtypeStruct((m, d), t) for t in out_dtypes]
        + [jax.ShapeDtypeStruct((d, m), t) for t in transposed_dtypes],
        compiler_params=_params(("parallel",), est),
        name="res_layernorm",
    )(x, y, g, b)


NOT_RANKED = float(N_KEYS - 1)


def _top_values(s, count):
    vals = []
    cur = s
    for t in range(count):
        m = jnp.max(cur, axis=0, keepdims=True)
        vals.append(m)
        if t + 1 < count:
            cur = jnp.where(cur == m, -jnp.inf, cur)
    return vals


def _peer_select_kernel(q_ref, k1_ref, k2_ref, cnt_ref, c1_ref, r2_ref, e2_ref, cand_ref):
    pairs = [(i, j) for i in range(PEER_TOPK) for j in range(PEER_TOPK) if (i + 1) * (j + 1) <= PEER_TOPK]
    for h in range(PEER_HEADS):
        base = h * 2 * PEER_HALF
        q1 = q_ref[:, base:base + PEER_HALF].astype(BF16)
        q2 = q_ref[:, base + PEER_HALF:base + 2 * PEER_HALF].astype(BF16)
        s1 = lax.dot_general(k1_ref[h], q1, NT_DIMS, preferred_element_type=F32)
        s2 = lax.dot_general(k2_ref[h], q2, NT_DIMS, preferred_element_type=F32)
        t1 = _top_values(s1, PEER_TOPK)
        t2 = _top_values(s2, PEER_TOPK)
        cand_ref[...] = jnp.full_like(cand_ref, -jnp.inf)
        for r, (i, j) in enumerate(pairs):
            cand_ref[r:r + 1, :] = t1[i] + t2[j]
        best = _top_values(cand_ref[...], PEER_TOPK)
        tau = best[PEER_TOPK - 1]
        z = sum(jnp.exp(bv - best[0]) for bv in best)
        counts = [sum(jnp.where(t1[i] + t2[j] >= tau, 1.0, 0.0) for j in range(PEER_TOPK) if (i, j) in pairs)
                  for i in range(PEER_TOPK)]
        cnt = jnp.zeros_like(s1)
        rank2 = jnp.full_like(s2, NOT_RANKED)
        for i in range(PEER_TOPK):
            cnt = jnp.where(s1 == t1[i], counts[i], cnt)
            rank2 = jnp.where(s2 == t2[i], float(i), rank2)
        cnt_ref[h] = cnt
        c1_ref[h] = jnp.exp(s1 - t1[0]) / z
        r2_ref[h] = rank2
        e2_ref[h] = jnp.exp(s2 - t2[0])


def _peer_select(q, keys1, keys2):
    m = q.shape[0]
    tm = _pick(m, (256, 128))
    assert m % tm == 0
    n_pairs = sum((i + 1) * (j + 1) <= PEER_TOPK for i in range(PEER_TOPK) for j in range(PEER_TOPK))
    cand_rows = -(-n_pairs // 8) * 8
    spec = pl.BlockSpec((PEER_HEADS, N_KEYS, tm), lambda i: (0, 0, i))
    est = 2 * tm * q.shape[1] * 4 + 2 * 4 * PEER_HEADS * N_KEYS * tm * 4 + (8 << 20)
    return pl.pallas_call(
        _peer_select_kernel,
        grid=(m // tm,),
        in_specs=[
            pl.BlockSpec((tm, q.shape[1]), lambda i: (i, 0)),
            pl.BlockSpec(keys1.shape, lambda i: (0, 0, 0)),
            pl.BlockSpec(keys2.shape, lambda i: (0, 0, 0)),
        ],
        out_specs=[spec, spec, spec, spec],
        out_shape=[jax.ShapeDtypeStruct((PEER_HEADS, N_KEYS, m), F32)] * 4,
        scratch_shapes=[pltpu.VMEM((cand_rows, tm), F32)],
        compiler_params=_params(("parallel",), est),
        name="peer_select",
    )(q, keys1, keys2)


def _peer_mix_kernel(xt_ref, u_ref, v_ref, cnt_ref, c1_ref, r2_ref, e2_ref, o_ref,
                     y_ref, coef_a, coef_b, act_ref, r2b_ref, e2b_ref, *, te, n_blocks):
    j = pl.program_id(1)
    d, tm = xt_ref.shape
    groups = te // N_KEYS
    jb = jnp.minimum(j, n_blocks - 1)
    pieces = [(g, c) for c in range(tm // LANES) for g in range(groups)]
    col_w = 2 * LANES
    n_vchunks = d // col_w

    @pl.when(j == 0)
    def _():
        y_ref[...] = jnp.zeros_like(y_ref)
        coef_b[...] = jnp.zeros_like(coef_b)
        r2b_ref[...] = r2_ref[...].astype(BF16)
        e2b_ref[...] = e2_ref[...].astype(BF16)

    def select_weights(g, c):
        a = jb * groups + g
        lanes = slice(c * LANES, (c + 1) * LANES)
        rows = (BF16_ROWS, LANES)
        cnt = [jnp.broadcast_to(cnt_ref[h, pl.ds(a, 1), :][:, lanes], rows).astype(BF16) for h in range(PEER_HEADS)]
        c1a = [jnp.broadcast_to(c1_ref[h, pl.ds(a, 1), :][:, lanes], rows).astype(BF16) for h in range(PEER_HEADS)]
        out = []
        for r in range(N_KEYS // BF16_ROWS):
            bs = slice(r * BF16_ROWS, (r + 1) * BF16_ROWS)
            acc = None
            for h in range(PEER_HEADS):
                wh = jnp.where(r2b_ref[h, bs, lanes] < cnt[h], e2b_ref[h, bs, lanes], 0.0) * c1a[h]
                acc = wh if acc is None else acc + wh
            out.append(acc)
        return jnp.concatenate(out, axis=0).astype(F32)

    def finish_piece(cur, g, c):
        rows = slice(g * N_KEYS, (g + 1) * N_KEYS)
        lanes = slice(c * LANES, (c + 1) * LANES)
        x = act_ref[rows, lanes]
        x = 0.5 * x * (1.0 + lax.erf(x * (2.0 ** -0.5)))
        cur[lanes, rows] = (x * select_weights(g, c)).T.astype(BF16)

    def step(cur, prev):
        if cur is not None:
            act_ref[...] = jnp.dot(u_ref[...], xt_ref[...], preferred_element_type=F32)
        for n in range(n_vchunks):
            cs = slice(n * col_w, (n + 1) * col_w)
            y_ref[:, cs] += jnp.dot(prev[...], v_ref[:, cs], preferred_element_type=F32)
            if cur is not None:
                for g, c in pieces[n::n_vchunks]:
                    finish_piece(cur, g, c)

    @pl.when(jnp.logical_and(j % 2 == 0, j < n_blocks))
    def _():
        step(coef_a, coef_b)

    @pl.when(jnp.logical_and(j % 2 == 1, j < n_blocks))
    def _():
        step(coef_b, coef_a)

    @pl.when(j == n_blocks)
    def _():
        step(None, coef_a if n_blocks % 2 else coef_b)
        o_ref[...] = y_ref[...].astype(o_ref.dtype)


def _peer_mix(xt, u, v, cnt, c1, r2, e2):
    d, m = xt.shape
    ne = v.shape[0]
    tm = _pick(m, (512, 256, 128))
    te = 512
    assert m % tm == 0 and tm % LANES == 0 and ne % te == 0 and te % N_KEYS == 0
    nj = ne // te
    once = pl.Buffered(1)
    sel_spec = pl.BlockSpec((PEER_HEADS, N_KEYS, tm), lambda i, j: (0, 0, i), pipeline_mode=once)
    est = (tm * d * 2 + 2 * 2 * te * d * 2 + 4 * PEER_HEADS * N_KEYS * tm * 4
           + 2 * tm * d * 4 + 2 * tm * te * 2 + 6 * te * tm * 4)
    return pl.pallas_call(
        functools.partial(_peer_mix_kernel, te=te, n_blocks=nj),
        grid=(m // tm, nj + 1),
        in_specs=[
            pl.BlockSpec((d, tm), lambda i, j: (0, i), pipeline_mode=once),
            pl.BlockSpec((te, d), lambda i, j: (jnp.minimum(j, nj - 1), 0)),
            pl.BlockSpec((te, d), lambda i, j: (jnp.maximum(j - 1, 0), 0)),
            sel_spec, sel_spec, sel_spec, sel_spec,
        ],
        out_specs=pl.BlockSpec((tm, d), lambda i, j: (i, 0)),
        out_shape=jax.ShapeDtypeStruct((m, d), BF16),
        scratch_shapes=[pltpu.VMEM((tm, d), F32), pltpu.VMEM((tm, te), BF16), pltpu.VMEM((tm, te), BF16),
                        pltpu.VMEM((te, tm), F32),
                        pltpu.VMEM((PEER_HEADS, N_KEYS, tm), BF16), pltpu.VMEM((PEER_HEADS, N_KEYS, tm), BF16)],
        compiler_params=_params(("parallel", "arbitrary"), est),
        name="peer_mix",
    )(xt, u, v, cnt, c1, r2, e2)


def _rope_tables(pos, batch):
    inv = ROPE_THETA ** (-jnp.arange(0, HD_DIFF, 2, dtype=F32) / HD_DIFF)
    ang = pos.astype(F32)[:, None] * inv[None, :]
    ang = jnp.concatenate([ang, ang], axis=-1)
    sign = jnp.where(jnp.arange(HD_DIFF) < HD_DIFF // 2, -1.0, 1.0).astype(F32)
    cos = jnp.tile(jnp.cos(ang), (batch, 1))
    sin_signed = jnp.tile(jnp.sin(ang) * sign[None, :], (batch, 1))
    return cos, sin_signed


def _prep_weights(w_in, w_gate2, b_gate, lam_q1, lam_k1, lam_q2, lam_k2, diff_norm_g, gla_norm_g, w_out,
                  ln1_g, ln1_b, ln2_g, ln2_b, peer_wq, peer_keys1, peer_keys2, peer_u, peer_v):
    wz = jnp.pad(w_in[:, IN_GATE:IN_GATE + GATE_RANK], ((0, 0), (0, LANES - GATE_RANK)))
    return dict(
        w_in=w_in.astype(BF16),
        wz=wz.astype(BF16),
        w2=jnp.pad(w_gate2, ((0, LANES - GATE_RANK), (0, 0))).astype(BF16),
        bg=b_gate.reshape(1, GLA_QK),
        lam=jnp.stack([lam_q1, lam_k1, lam_q2, lam_k2]),
        dng=diff_norm_g.reshape(1, 2 * HD_DIFF),
        gng=gla_norm_g.reshape(1, GLA_DV),
        wo_a=w_out[:DIFF_W].astype(BF16),
        wo_b=w_out[DIFF_W:].astype(BF16),
        ln1_g=ln1_g.reshape(1, -1), ln1_b=ln1_b.reshape(1, -1),
        ln2_g=ln2_g.reshape(1, -1), ln2_b=ln2_b.reshape(1, -1),
        pwq=peer_wq.astype(BF16),
        pk1=peer_keys1.astype(BF16), pk2=peer_keys2.astype(BF16),
        pu=peer_u.astype(BF16), pv=peer_v.astype(BF16),
    )


def _layer(x, pos, past_k, past_v, s0, chunk, lam_init, alpha, w):
    batch, seq, d = x.shape
    m = batch * seq
    x2d = x.reshape(m, d)
    xb = x2d.astype(BF16)
    cos, sin_signed = _rope_tables(pos, batch)

    proj = [(xb, w["w_in"])]
    (q,) = _matmul(proj, _epi_q, [BF16], (cos, sin_signed), name="proj_q", cols=(0, DIFF_W))
    k32, kb = _matmul(proj, _epi_k, [F32, BF16], (cos, sin_signed), name="proj_k", cols=(DIFF_W, DIFF_W))
    v32, vb = _matmul(proj, _epi_f32_bf16, [F32, BF16], name="proj_v", cols=(2 * DIFF_W, DIFF_W))
    (gp,) = _matmul(proj, _epi_f32, [F32], name="proj_gla", cols=(IN_GLA, IN_GATE - IN_GLA))
    (gz,) = _matmul([(xb, w["wz"])], _epi_f32, [F32], tn=LANES, name="proj_gate")

    new_k = k32.reshape(batch, seq, H_DIFF, 2 * HD_DIFF)
    new_v = v32.reshape(batch, seq, H_DIFF, 2 * HD_DIFF)
    if past_k is None:
        o_diff = _attn_prompt(q, kb, vb, w["lam"], w["dng"], batch, seq, lam_init)
    else:
        o_diff = _attn_cached(q, new_k, new_v, past_k, past_v, w["lam"], w["dng"], batch, seq, lam_init)
    o_gla, s_new = _gla(gp, gz, w["w2"], w["bg"], w["gng"], s0, batch, seq, chunk)

    (mix,) = _matmul([(o_diff, w["wo_a"]), (o_gla, w["wo_b"])], _epi_f32, [F32], tn=1024, name="out_proj")
    x1, x1b, x1t = _res_layernorm(x2d, mix, w["ln1_g"], w["ln1_b"], alpha, [F32, BF16], [BF16])

    (pq,) = _matmul([(x1b, w["pwq"])], _epi_f32, [F32], name="peer_query")
    cnt, c1, r2, e2 = _peer_select(pq, w["pk1"], w["pk2"])
    y = _peer_mix(x1t, w["pu"], w["pv"], cnt, c1, r2, e2)
    (x2,) = _res_layernorm(x1, y, w["ln2_g"], w["ln2_b"], alpha, [F32])
    return x2.reshape(batch, seq, d), new_k, new_v, s_new


def kernel(x_prompt, x_sample, cache_diff_k, cache_diff_v, state_gla, w_in, w_gate2, b_gate, lam_q1, lam_k1, lam_q2, lam_k2, diff_norm_g, gla_norm_g, w_out, ln1_g, ln1_b, ln2_g, ln2_b, peer_wq, peer_keys1, peer_keys2, peer_u, peer_v):
    depth = w_in.shape[0]
    tp = x_prompt.shape[1]
    ts = x_sample.shape[1]
    past = cache_diff_k.shape[2]
    alpha = (2 * depth) ** 0.25
    pos_p = jnp.arange(tp, dtype=jnp.int32)
    pos_s = past + jnp.arange(ts, dtype=jnp.int32)
    y_p, y_s = x_prompt, x_sample
    outs = [[] for _ in range(6)]
    for l in range(depth):
        lam_init = 0.8 - 0.6 * math.exp(-0.3 * l)
        w = _prep_weights(w_in[l], w_gate2[l], b_gate[l], lam_q1[l], lam_k1[l], lam_q2[l], lam_k2[l],
                          diff_norm_g[l], gla_norm_g[l], w_out[l], ln1_g[l], ln1_b[l], ln2_g[l], ln2_b[l],
                          peer_wq[l], peer_keys1[l], peer_keys2[l], peer_u[l], peer_v[l])
        y_p, kp, vp, sp = _layer(y_p, pos_p, None, None, None, CHUNK, lam_init, alpha, w)
        y_s, k_s, v_s, s_s = _layer(y_s, pos_s, cache_diff_k[l], cache_diff_v[l], state_gla[l],
                                    ts, lam_init, alpha, w)
        for lst, val in zip(outs, (kp, vp, sp, k_s, v_s, s_s)):
            lst.append(val)
    return (y_p, y_s) + tuple(jnp.stack(o) for o in outs)
```

```python
import functools
import math

import jax
import jax.numpy as jnp
from jax import lax
from jax.experimental import pallas as pl
from jax.experimental.pallas import tpu as pltpu

CHUNK = 64
HD_DIFF = 128
H_DIFF = 8
H_GLA = 4
GLA_DK = 256
GLA_DV = 512
GATE_RANK = 16
GLA_TAU = 16.0
ROPE_THETA = 10000.0
N_KEYS = 128
PEER_HEADS = 8
PEER_HALF = 128
PEER_TOPK = 16
EPS = 1e-5
NEG_INF = -1e30
ROW_BLOCK = 256

DIFF_W = H_DIFF * 2 * HD_DIFF
GLA_QK = H_GLA * GLA_DK
GLA_V = H_GLA * GLA_DV
IN_GLA = 3 * DIFF_W
IN_GATE = IN_GLA + 2 * GLA_QK + 2 * GLA_V

LANES = 128
SUBLANES = 8
BF16_ROWS = 2 * SUBLANES
VMEM_CAP = 60000 * 1024

BF16 = jnp.bfloat16
F32 = jnp.float32

NT_DIMS = (((1,), (1,)), ((), ()))
TN_DIMS = (((0,), (0,)), ((), ()))


def _pick(n, cands):
    for c in cands:
        if n % c == 0:
            return c
    return n


def _params(sem, est_bytes):
    limit = int(min(VMEM_CAP, max(32 * 1024 * 1024, est_bytes * 5 // 4)))
    return pltpu.CompilerParams(dimension_semantics=sem, vmem_limit_bytes=limit)


def _mm_kernel(*refs, n_pairs, n_extra, epilogue):
    a_refs = refs[:n_pairs]
    b_refs = refs[n_pairs:2 * n_pairs]
    extras = refs[2 * n_pairs:2 * n_pairs + n_extra]
    outs = refs[2 * n_pairs + n_extra:]
    acc = jnp.dot(a_refs[0][...], b_refs[0][...], preferred_element_type=F32)
    for a, b in zip(a_refs[1:], b_refs[1:]):
        acc = acc + jnp.dot(a[...], b[...], preferred_element_type=F32)
    epilogue(acc, extras, outs)


def _matmul(pairs, epilogue, out_dtypes, row_extras=(), tn=512, name="matmul"):
    m = pairs[0][0].shape[0]
    n = pairs[0][1].shape[1]
    tm = _pick(m, (1024, 512, 256, 128, 64, 32, 16, 8))
    tn = min(tn, n)
    assert n % tn == 0
    in_specs = []
    est = 0
    for a, _ in pairs:
        in_specs.append(pl.BlockSpec((tm, a.shape[1]), lambda i, j: (i, 0)))
        est += 2 * tm * a.shape[1] * a.dtype.itemsize
    for _, b in pairs:
        in_specs.append(pl.BlockSpec((b.shape[0], tn), lambda i, j: (0, j)))
        est += 2 * b.shape[0] * tn * b.dtype.itemsize
    for e in row_extras:
        in_specs.append(pl.BlockSpec((tm, e.shape[1]), lambda i, j: (i, 0)))
        est += 2 * tm * e.shape[1] * 4
    out_specs = [pl.BlockSpec((tm, tn), lambda i, j: (i, j)) for _ in out_dtypes]
    out_shape = [jax.ShapeDtypeStruct((m, n), d) for d in out_dtypes]
    est += sum(2 * tm * tn * jnp.dtype(d).itemsize for d in out_dtypes) + 3 * tm * tn * 4
    kern = functools.partial(_mm_kernel, n_pairs=len(pairs), n_extra=len(row_extras), epilogue=epilogue)
    return pl.pallas_call(
        kern,
        grid=(m // tm, n // tn),
        in_specs=in_specs,
        out_specs=out_specs,
        out_shape=out_shape,
        compiler_params=_params(("parallel", "arbitrary"), est),
        name=name,
    )(*[a for a, _ in pairs], *[b for _, b in pairs], *row_extras)


def _rope_tile(acc, cos, sin_signed):
    outs = []
    for g in range(acc.shape[1] // LANES):
        xg = acc[:, g * LANES:(g + 1) * LANES]
        outs.append(xg * cos + pltpu.roll(xg, LANES // 2, 1) * sin_signed)
    return outs[0] if len(outs) == 1 else jnp.concatenate(outs, axis=1)


def _epi_q(acc, extras, outs):
    r = _rope_tile(acc, extras[0][...], extras[1][...])
    outs[0][...] = (r * (HD_DIFF ** -0.5 * math.log2(math.e))).astype(BF16)


def _epi_k(acc, extras, outs):
    r = _rope_tile(acc, extras[0][...], extras[1][...])
    outs[0][...] = r
    outs[1][...] = r.astype(BF16)


def _epi_f32_bf16(acc, extras, outs):
    outs[0][...] = acc
    outs[1][...] = acc.astype(BF16)


def _epi_f32(acc, extras, outs):
    outs[0][...] = acc


def _proj_kernel(x_ref, w_ref, cos_ref, sin_ref, q_ref, k_ref, kb_ref, v_ref, vb_ref, g_ref, *, bounds):
    j = pl.program_id(1)
    acc = jnp.dot(x_ref[...], w_ref[...], preferred_element_type=F32)
    extras = (cos_ref, sin_ref)
    groups = ((_epi_q, (q_ref,)), (_epi_k, (k_ref, kb_ref)), (_epi_f32_bf16, (v_ref, vb_ref)), (_epi_f32, (g_ref,)))
    for (lo, hi), (epilogue, outs) in zip(bounds, groups):
        @pl.when(jnp.logical_and(j >= lo, j < hi))
        def _(epilogue=epilogue, outs=outs):
            epilogue(acc, extras, outs)


def _project(xb, w_in, cos, sin_signed, tn=512):
    m, d = xb.shape
    tm = _pick(m, (1024, 512, 256, 128, 64, 32, 16, 8))
    widths = (DIFF_W, DIFF_W, DIFF_W, IN_GATE - IN_GLA)
    assert all(wd % tn == 0 for wd in widths)
    starts = [sum(widths[:g]) // tn for g in range(len(widths))]
    bounds = [(s, s + wd // tn) for s, wd in zip(starts, widths)]

    def out_spec(g):
        lo, hi = bounds[g]
        return pl.BlockSpec((tm, tn), lambda i, j: (i, jnp.clip(j - lo, 0, hi - lo - 1)))

    row_spec = pl.BlockSpec((tm, LANES), lambda i, j: (i, 0))
    out_groups = (0, 1, 1, 2, 2, 3)
    out_dtypes = (BF16, F32, BF16, F32, BF16, F32)
    est = 2 * tm * d * 2 + 2 * d * tn * 2 + 4 * tm * LANES * 4 + 2 * tm * tn * (2 + 4 + 2 + 4 + 2 + 4) + 3 * tm * tn * 4
    return pl.pallas_call(
        functools.partial(_proj_kernel, bounds=bounds),
        grid=(m // tm, bounds[-1][1]),
        in_specs=[
            pl.BlockSpec((tm, d), lambda i, j: (i, 0)),
            pl.BlockSpec((d, tn), lambda i, j: (0, j)),
            row_spec, row_spec,
        ],
        out_specs=[out_spec(g) for g in out_groups],
        out_shape=[jax.ShapeDtypeStruct((m, widths[g]), t) for g, t in zip(out_groups, out_dtypes)],
        compiler_params=_params(("parallel", "arbitrary"), est),
        name="proj",
    )(xb, w_in, cos, sin_signed)


def _lambda_value(lam_ref, lam_init):
    lq1 = lam_ref[0:1, :]
    lk1 = lam_ref[1:2, :]
    lq2 = lam_ref[2:3, :]
    lk2 = lam_ref[3:4, :]
    return (jnp.exp(jnp.sum(lq1 * lk1, axis=1, keepdims=True))
            - jnp.exp(jnp.sum(lq2 * lk2, axis=1, keepdims=True)) + lam_init)


def _diff_finalize(o1, o2, lam_ref, g_ref, lam_init):
    lam = _lambda_value(lam_ref, lam_init)
    o = o1 - lam * o2
    ms = jnp.mean(o * o, axis=1, keepdims=True)
    return (o * lax.rsqrt(ms + EPS)) * g_ref[...] * (1.0 - lam_init)


def _attn_prompt_kernel(qi_ref, kj_ref, q_ref, k_ref, v_ref, lam_ref, g_ref, o_ref,
                        m1, l1, a1, m2, l2, a2, *, tq, lam_init):
    p = pl.program_id(2)
    i = qi_ref[p]
    j = kj_ref[p]
    half_t = tq // 2

    @pl.when(j == 0)
    def _():
        m1[...] = jnp.full_like(m1, NEG_INF)
        m2[...] = jnp.full_like(m2, NEG_INF)
        l1[...] = jnp.zeros_like(l1)
        l2[...] = jnp.zeros_like(l2)
        a1[...] = jnp.zeros_like(a1)
        a2[...] = jnp.zeros_like(a2)

    def update(r0, nr, nc, masked):
        if nr > ROW_BLOCK:
            for rr in range(r0, r0 + nr, ROW_BLOCK):
                update(rr, ROW_BLOCK, nc, masked)
            return
        rows = slice(r0, r0 + nr)
        q = q_ref[rows, :]
        k = k_ref[0:nc, :]
        v = v_ref[0:nc, :]
        if masked:
            qc = (r0 + lax.broadcasted_iota(jnp.int32, (nr, nc), 0)) // CHUNK
            kc = lax.broadcasted_iota(jnp.int32, (nr, nc), 1) // CHUNK
            mask = qc >= kc
        for half, (m_ref, l_ref, a_ref) in enumerate(((m1, l1, a1), (m2, l2, a2))):
            qh = q[:, half * HD_DIFF:(half + 1) * HD_DIFF]
            kh = k[:, half * HD_DIFF:(half + 1) * HD_DIFF]
            s = lax.dot_general(qh, kh, NT_DIMS, preferred_element_type=F32)
            if masked:
                s = jnp.where(mask, s, NEG_INF)
            m_old = m_ref[rows, :]
            m_new = jnp.maximum(m_old, jnp.max(s, axis=1, keepdims=True))
            alpha = jnp.exp2(m_old - m_new)
            pr = jnp.exp2(s - m_new)
            l_ref[rows, :] = alpha * l_ref[rows, :] + jnp.sum(pr, axis=1, keepdims=True)
            a_ref[rows, :] = alpha * a_ref[rows, :] + jnp.dot(pr.astype(BF16), v, preferred_element_type=F32)
            m_ref[rows, :] = m_new

    @pl.when(j < i)
    def _():
        update(0, tq, tq, False)

    @pl.when(j == i)
    def _():
        if half_t % CHUNK == 0:
            update(0, half_t, half_t, True)
            update(half_t, half_t, tq, True)
        else:
            update(0, tq, tq, True)
        o = _diff_finalize(a1[...] / l1[...], a2[...] / l2[...], lam_ref, g_ref, lam_init)
        o_ref[...] = o.astype(o_ref.dtype)


def _attn_prompt(q, k, v, lam_vecs, g, batch, seq, lam_init):
    tq = _pick(seq, (1024, 512, 256, 128, 64))
    nq = seq // tq
    hw = 2 * HD_DIFF
    pairs = [(i, j) for i in range(nq) for j in range(i + 1)]
    qi = jnp.asarray([p[0] for p in pairs], jnp.int32)
    kj = jnp.asarray([p[1] for p in pairs], jnp.int32)

    def q_map(b, h, p, qi_ref, kj_ref):
        return (b * nq + qi_ref[p], h)

    def kv_map(b, h, p, qi_ref, kj_ref):
        return (b * nq + kj_ref[p], h)

    kern = functools.partial(_attn_prompt_kernel, tq=tq, lam_init=lam_init)
    est = 2 * 4 * tq * hw * 2 + 2 * tq * hw * 4 + 4 * tq * LANES * 4 + 8 * tq * tq * 4
    return pl.pallas_call(
        kern,
        grid_spec=pltpu.PrefetchScalarGridSpec(
            num_scalar_prefetch=2,
            grid=(batch, H_DIFF, len(pairs)),
            in_specs=[
                pl.BlockSpec((tq, hw), q_map),
                pl.BlockSpec((tq, hw), kv_map),
                pl.BlockSpec((tq, hw), kv_map),
                pl.BlockSpec((4, HD_DIFF), lambda b, h, p, qi_ref, kj_ref: (0, 0)),
                pl.BlockSpec((1, hw), lambda b, h, p, qi_ref, kj_ref: (0, 0)),
            ],
            out_specs=pl.BlockSpec((tq, hw), q_map),
            scratch_shapes=[
                pltpu.VMEM((tq, 1), F32), pltpu.VMEM((tq, 1), F32), pltpu.VMEM((tq, hw), F32),
                pltpu.VMEM((tq, 1), F32), pltpu.VMEM((tq, 1), F32), pltpu.VMEM((tq, hw), F32),
            ],
        ),
        out_shape=jax.ShapeDtypeStruct((batch * seq, DIFF_W), BF16),
        compiler_params=_params(("parallel", "parallel", "arbitrary"), est),
        name="diff_attn_prompt",
    )(qi, kj, q, k, v, lam_vecs, g)


def _attn_cached_kernel(q_ref, kn_ref, vn_ref, kc_ref, vc_ref, bc_ref, bn_ref, lam_ref, g_ref, o_ref,
                        *, past, lam_init):
    ts = q_ref.shape[0]
    hw = 2 * HD_DIFF
    kc = kc_ref[0].reshape(past * H_DIFF, hw).astype(BF16)
    vc = vc_ref[0].reshape(past * H_DIFF, hw).astype(BF16)
    kn = kn_ref[0].reshape(ts * H_DIFF, hw).astype(BF16)
    vn = vn_ref[0].reshape(ts * H_DIFF, hw).astype(BF16)
    halves = []
    for half in range(2):
        sl = slice(half * HD_DIFF, (half + 1) * HD_DIFF)
        qh = jnp.concatenate([q_ref[:, h * hw + half * HD_DIFF:h * hw + (half + 1) * HD_DIFF]
                              for h in range(H_DIFF)], axis=0)
        sc = lax.dot_general(kc[:, sl], qh, NT_DIMS, preferred_element_type=F32) + bc_ref[...]
        sn = lax.dot_general(kn[:, sl], qh, NT_DIMS, preferred_element_type=F32) + bn_ref[...]
        m = jnp.maximum(jnp.max(sc, axis=0, keepdims=True), jnp.max(sn, axis=0, keepdims=True))
        pc = jnp.exp2(sc - m)
        pn = jnp.exp2(sn - m)
        l = jnp.sum(pc, axis=0, keepdims=True) + jnp.sum(pn, axis=0, keepdims=True)
        acc = (jnp.dot(pc.T.astype(BF16), vc, preferred_element_type=F32)
               + jnp.dot(pn.T.astype(BF16), vn, preferred_element_type=F32))
        l_col = jnp.broadcast_to(l, (l.shape[1], l.shape[1])).T
        halves.append(acc / jnp.concatenate([l_col] * (hw // l.shape[1]), axis=1))
    for h in range(H_DIFF):
        rows = slice(h * ts, (h + 1) * ts)
        o = _diff_finalize(halves[0][rows], halves[1][rows], lam_ref, g_ref, lam_init)
        o_ref[:, h * hw:(h + 1) * hw] = o.astype(o_ref.dtype)


def _attn_cached(q, kn, vn, cache_k, cache_v, lam_vecs, g, batch, seq, lam_init):
    past = cache_k.shape[1]
    hw = 2 * HD_DIFF
    assert H_DIFF * seq == LANES
    col_head = jnp.arange(LANES) // seq
    col_chunk = (past + jnp.arange(LANES) % seq) // CHUNK

    def bias(key_pos):
        row_head = jnp.tile(jnp.arange(H_DIFF), key_pos.shape[0])
        row_chunk = jnp.repeat(key_pos, H_DIFF) // CHUNK
        ok = (row_head[:, None] == col_head[None, :]) & (col_chunk[None, :] >= row_chunk[:, None])
        return jnp.where(ok, 0.0, NEG_INF).astype(F32)

    bias_c = bias(jnp.arange(past))
    bias_n = bias(past + jnp.arange(seq))
    kern = functools.partial(_attn_cached_kernel, past=past, lam_init=lam_init)
    est = 2 * past * DIFF_W * 4 + 2 * past * DIFF_W * 2 + 8 * past * H_DIFF * LANES * 4
    once = pl.Buffered(1)
    cache_spec = pl.BlockSpec((1, past, H_DIFF, hw), lambda b: (b, 0, 0, 0), pipeline_mode=once)
    new_spec = pl.BlockSpec((1, seq, H_DIFF, hw), lambda b: (b, 0, 0, 0))
    row_spec = pl.BlockSpec((seq, DIFF_W), lambda b: (b, 0))
    return pl.pallas_call(
        kern,
        grid=(batch,),
        in_specs=[
            row_spec, new_spec, new_spec, cache_spec, cache_spec,
            pl.BlockSpec(bias_c.shape, lambda b: (0, 0), pipeline_mode=once),
            pl.BlockSpec(bias_n.shape, lambda b: (0, 0)),
            pl.BlockSpec((4, HD_DIFF), lambda b: (0, 0)),
            pl.BlockSpec((1, hw), lambda b: (0, 0)),
        ],
        out_specs=row_spec,
        out_shape=jax.ShapeDtypeStruct((batch * seq, DIFF_W), BF16),
        compiler_params=_params(("parallel",), est),
        name="diff_attn_cached",
    )(q, kn, vn, cache_k, cache_v, bias_c, bias_n, lam_vecs, g)


def _split3(x):
    hi = x.astype(BF16)
    r1 = x - hi.astype(F32)
    mid = r1.astype(BF16)
    lo = (r1 - mid.astype(F32)).astype(BF16)
    return hi, mid, lo


def _gla_kernel(*refs, chunk, n_chunks, has_s0):
    if has_s0:
        q_ref, k_ref, v_ref, r_ref, z_ref, w2_ref, bg_ref, gn_ref, s0_ref, o_ref, s_ref, st = refs
    else:
        q_ref, k_ref, v_ref, r_ref, z_ref, w2_ref, bg_ref, gn_ref, o_ref, s_ref, st = refs
    i = pl.program_id(2)

    @pl.when(i == 0)
    def _():
        if has_s0:
            st[...] = s0_ref[0, 0].T
        else:
            st[...] = jnp.zeros_like(st)

    scale = GLA_DK ** -0.5
    zz = jnp.dot(z_ref[...].astype(BF16), w2_ref[...], preferred_element_type=F32) + bg_ref[...]
    logf = jax.nn.log_sigmoid(zz) / GLA_TAU
    row = lax.broadcasted_iota(jnp.int32, (chunk, chunk), 0)
    col = lax.broadcasted_iota(jnp.int32, (chunk, chunk), 1)
    causal = row >= col
    tri = jnp.where(causal, 1.0, 0.0).astype(BF16)
    for c in range(n_chunks):
        rows = slice(c * chunk, (c + 1) * chunk)
        g = logf[rows]
        bcum = sum(jnp.dot(tri, part, preferred_element_type=F32) for part in _split3(g))
        b_last = bcum[chunk - 1:chunk, :]
        qc = q_ref[rows, :]
        kc = k_ref[rows, :]
        vc = v_ref[rows, :].astype(BF16)
        q_in = (qc * jnp.exp(bcum) * scale).astype(BF16)
        k_in = (kc * jnp.exp(-bcum)).astype(BF16)
        k_end = (kc * jnp.exp(b_last - bcum)).astype(BF16)
        a = lax.dot_general(q_in, k_in, NT_DIMS, preferred_element_type=F32)
        a = jnp.where(causal, a, 0.0)
        s_old = st[...]
        o = (jnp.dot(a.astype(BF16), vc, preferred_element_type=F32)
             + lax.dot_general(q_in, s_old.astype(BF16), NT_DIMS, preferred_element_type=F32))
        st[...] = s_old * jnp.exp(b_last) + lax.dot_general(vc, k_end, TN_DIMS, preferred_element_type=F32)
        ms = jnp.mean(o * o, axis=1, keepdims=True)
        on = (o * lax.rsqrt(ms + EPS)) * gn_ref[...]
        gate = r_ref[rows, :]
        o_ref[rows, :] = (on * (gate * jax.nn.sigmoid(gate))).astype(o_ref.dtype)

    @pl.when(i == pl.num_programs(2) - 1)
    def _():
        s_ref[0, 0] = st[...].T


def _gla(gp, gz, w2, bg, gn, s0, batch, seq, chunk):
    tb = chunk * max(1, min(8, seq // chunk))
    while seq % tb:
        tb -= chunk
    nb = seq // tb
    kq = GLA_QK // GLA_DK
    kv = (2 * GLA_QK) // GLA_DV
    kr = kv + GLA_V // GLA_DV
    in_specs = [
        pl.BlockSpec((tb, GLA_DK), lambda b, h, i: (b * nb + i, h)),
        pl.BlockSpec((tb, GLA_DK), lambda b, h, i: (b * nb + i, kq + h)),
        pl.BlockSpec((tb, GLA_DV), lambda b, h, i: (b * nb + i, kv + h)),
        pl.BlockSpec((tb, GLA_DV), lambda b, h, i: (b * nb + i, kr + h)),
        pl.BlockSpec((tb, LANES), lambda b, h, i: (b * nb + i, 0)),
        pl.BlockSpec((LANES, GLA_DK), lambda b, h, i: (0, h)),
        pl.BlockSpec((1, GLA_DK), lambda b, h, i: (0, h)),
        pl.BlockSpec((1, GLA_DV), lambda b, h, i: (0, 0)),
    ]
    args = [gp, gp, gp, gp, gz, w2, bg, gn]
    if s0 is not None:
        in_specs.append(pl.BlockSpec((1, 1, GLA_DK, GLA_DV), lambda b, h, i: (b, h, 0, 0)))
        args.append(s0)
    kern = functools.partial(_gla_kernel, chunk=chunk, n_chunks=tb // chunk, has_s0=s0 is not None)
    est = 2 * tb * (2 * GLA_DK + 2 * GLA_DV + LANES) * 4 + 2 * tb * GLA_DV * 2 + 7 * GLA_DK * GLA_DV * 4
    return pl.pallas_call(
        kern,
        grid=(batch, H_GLA, nb),
        in_specs=in_specs,
        out_specs=[
            pl.BlockSpec((tb, GLA_DV), lambda b, h, i: (b * nb + i, h)),
            pl.BlockSpec((1, 1, GLA_DK, GLA_DV), lambda b, h, i: (b, h, 0, 0)),
        ],
        out_shape=[
            jax.ShapeDtypeStruct((batch * seq, GLA_V), BF16),
            jax.ShapeDtypeStruct((batch, H_GLA, GLA_DK, GLA_DV), F32),
        ],
        scratch_shapes=[pltpu.VMEM((GLA_DV, GLA_DK), F32)],
        compiler_params=_params(("parallel", "parallel", "arbitrary"), est),
        name="gla",
    )(*args)


def _ln_kernel(x_ref, y_ref, g_ref, b_ref, *outs, alpha, n_plain):
    h = alpha * x_ref[...] + y_ref[...].astype(F32)
    mu = jnp.mean(h, axis=1, keepdims=True)
    d = h - mu
    var = jnp.mean(d * d, axis=1, keepdims=True)
    r = (d * lax.rsqrt(var + EPS)) * g_ref[...] + b_ref[...]
    for o in outs[:n_plain]:
        o[...] = r.astype(o.dtype)
    for o in outs[n_plain:]:
        o[...] = r.T.astype(o.dtype)


def _res_layernorm(x, y, g, b, alpha, out_dtypes, transposed_dtypes=()):
    m, d = x.shape
    tm = _pick(m, (128,)) if transposed_dtypes else _pick(m, (128, 64, 32, 16, 8))
    assert m % tm == 0
    sizes = [jnp.dtype(t).itemsize for t in (*out_dtypes, *transposed_dtypes)]
    est = 2 * tm * d * (x.dtype.itemsize + y.dtype.itemsize + sum(sizes)) + 5 * tm * d * 4
    return pl.pallas_call(
        functools.partial(_ln_kernel, alpha=alpha, n_plain=len(out_dtypes)),
        grid=(m // tm,),
        in_specs=[
            pl.BlockSpec((tm, d), lambda i: (i, 0)),
            pl.BlockSpec((tm, d), lambda i: (i, 0)),
            pl.BlockSpec((1, d), lambda i: (0, 0)),
            pl.BlockSpec((1, d), lambda i: (0, 0)),
        ],
        out_specs=[pl.BlockSpec((tm, d), lambda i: (i, 0)) for _ in out_dtypes]
        + [pl.BlockSpec((d, tm), lambda i: (0, i)) for _ in transposed_dtypes],
        out_shape=[jax.ShapeDtypeStruct((m, d), t) for t in out_dtypes]
        + [jax.ShapeDtypeStruct((d, m), t) for t in transposed_dtypes],
        compiler_params=_params(("parallel",), est),
        name="res_layernorm",
    )(x, y, g, b)


NOT_RANKED = float(N_KEYS - 1)


def _top_values(s, count):
    vals = []
    cur = s
    for t in range(count):
        m = jnp.max(cur, axis=0, keepdims=True)
        vals.append(m)
        if t + 1 < count:
            cur = jnp.where(cur == m, -jnp.inf, cur)
    return vals


def _peer_select_kernel(q_ref, k1_ref, k2_ref, cnt_ref, c1_ref, r2_ref, e2_ref, cand_ref):
    pairs = [(i, j) for i in range(PEER_TOPK) for j in range(PEER_TOPK) if (i + 1) * (j + 1) <= PEER_TOPK]
    for h in range(PEER_HEADS):
        base = h * 2 * PEER_HALF
        q1 = q_ref[:, base:base + PEER_HALF].astype(BF16)
        q2 = q_ref[:, base + PEER_HALF:base + 2 * PEER_HALF].astype(BF16)
        s1 = lax.dot_general(k1_ref[h], q1, NT_DIMS, preferred_element_type=F32)
        s2 = lax.dot_general(k2_ref[h], q2, NT_DIMS, preferred_element_type=F32)
        t1 = _top_values(s1, PEER_TOPK)
        t2 = _top_values(s2, PEER_TOPK)
        cand_ref[...] = jnp.full_like(cand_ref, -jnp.inf)
        for r, (i, j) in enumerate(pairs):
            cand_ref[r:r + 1, :] = t1[i] + t2[j]
        best = _top_values(cand_ref[...], PEER_TOPK)
        tau = best[PEER_TOPK - 1]
        z = sum(jnp.exp(bv - best[0]) for bv in best)
        counts = [sum(jnp.where(t1[i] + t2[j] >= tau, 1.0, 0.0) for j in range(PEER_TOPK) if (i, j) in pairs)
                  for i in range(PEER_TOPK)]
        cnt = jnp.zeros_like(s1)
        rank2 = jnp.full_like(s2, NOT_RANKED)
        for i in range(PEER_TOPK):
            cnt = jnp.where(s1 == t1[i], counts[i], cnt)
            rank2 = jnp.where(s2 == t2[i], float(i), rank2)
        cnt_ref[h] = cnt
        c1_ref[h] = jnp.exp(s1 - t1[0]) / z
        r2_ref[h] = rank2
        e2_ref[h] = jnp.exp(s2 - t2[0])


def _peer_select(q, keys1, keys2):
    m = q.shape[0]
    tm = _pick(m, (256, 128))
    assert m % tm == 0
    n_pairs = sum((i + 1) * (j + 1) <= PEER_TOPK for i in range(PEER_TOPK) for j in range(PEER_TOPK))
    cand_rows = -(-n_pairs // 8) * 8
    spec = pl.BlockSpec((PEER_HEADS, N_KEYS, tm), lambda i: (0, 0, i))
    est = 2 * tm * q.shape[1] * 4 + 2 * 4 * PEER_HEADS * N_KEYS * tm * 4 + (8 << 20)
    return pl.pallas_call(
        _peer_select_kernel,
        grid=(m // tm,),
        in_specs=[
            pl.BlockSpec((tm, q.shape[1]), lambda i: (i, 0)),
            pl.BlockSpec(keys1.shape, lambda i: (0, 0, 0)),
            pl.BlockSpec(keys2.shape, lambda i: (0, 0, 0)),
        ],
        out_specs=[spec, spec, spec, spec],
        out_shape=[jax.ShapeDtypeStruct((PEER_HEADS, N_KEYS, m), F32)] * 4,
        scratch_shapes=[pltpu.VMEM((cand_rows, tm), F32)],
        compiler_params=_params(("parallel",), est),
        name="peer_select",
    )(q, keys1, keys2)


def _peer_mix_kernel(xt_ref, u_ref, v_ref, cnt_ref, c1_ref, r2_ref, e2_ref, o_ref,
                     y_ref, coef_a, coef_b, act_ref, r2b_ref, e2b_ref, *, te, n_blocks):
    j = pl.program_id(1)
    d, tm = xt_ref.shape
    groups = te // N_KEYS
    jb = jnp.minimum(j, n_blocks - 1)
    pieces = [(g, c) for c in range(tm // LANES) for g in range(groups)]
    col_w = 2 * LANES
    n_vchunks = d // col_w

    @pl.when(j == 0)
    def _():
        y_ref[...] = jnp.zeros_like(y_ref)
        coef_b[...] = jnp.zeros_like(coef_b)
        r2b_ref[...] = r2_ref[...].astype(BF16)
        e2b_ref[...] = e2_ref[...].astype(BF16)

    def select_weights(g, c):
        a = jb * groups + g
        lanes = slice(c * LANES, (c + 1) * LANES)
        rows = (BF16_ROWS, LANES)
        cnt = [jnp.broadcast_to(cnt_ref[h, pl.ds(a, 1), :][:, lanes], rows).astype(BF16) for h in range(PEER_HEADS)]
        c1a = [jnp.broadcast_to(c1_ref[h, pl.ds(a, 1), :][:, lanes], rows).astype(BF16) for h in range(PEER_HEADS)]
        out = []
        for r in range(N_KEYS // BF16_ROWS):
            bs = slice(r * BF16_ROWS, (r + 1) * BF16_ROWS)
            acc = None
            for h in range(PEER_HEADS):
                wh = jnp.where(r2b_ref[h, bs, lanes] < cnt[h], e2b_ref[h, bs, lanes], 0.0) * c1a[h]
                acc = wh if acc is None else acc + wh
            out.append(acc)
        return jnp.concatenate(out, axis=0).astype(F32)

    def finish_piece(cur, g, c):
        rows = slice(g * N_KEYS, (g + 1) * N_KEYS)
        lanes = slice(c * LANES, (c + 1) * LANES)
        x = act_ref[rows, lanes]
        x = 0.5 * x * (1.0 + lax.erf(x * (2.0 ** -0.5)))
        cur[lanes, rows] = (x * select_weights(g, c)).T.astype(BF16)

    def step(cur, prev):
        if cur is not None:
            act_ref[...] = jnp.dot(u_ref[...], xt_ref[...], preferred_element_type=F32)
        y_ref[...] += jnp.dot(prev[...], v_ref[...], preferred_element_type=F32)
        if cur is not None:
            for g, c in pieces:
                finish_piece(cur, g, c)

    @pl.when(jnp.logical_and(j % 2 == 0, j < n_blocks))
    def _():
        step(coef_a, coef_b)

    @pl.when(jnp.logical_and(j % 2 == 1, j < n_blocks))
    def _():
        step(coef_b, coef_a)

    @pl.when(j == n_blocks)
    def _():
        step(None, coef_a if n_blocks % 2 else coef_b)
        o_ref[...] = y_ref[...].astype(o_ref.dtype)


def _peer_mix(xt, u, v, cnt, c1, r2, e2):
    d, m = xt.shape
    ne = v.shape[0]
    tm = _pick(m, (512, 256, 128))
    te = 512
    assert m % tm == 0 and tm % LANES == 0 and ne % te == 0 and te % N_KEYS == 0
    nj = ne // te
    once = pl.Buffered(1)
    sel_spec = pl.BlockSpec((PEER_HEADS, N_KEYS, tm), lambda i, j: (0, 0, i), pipeline_mode=once)
    est = (tm * d * 2 + 2 * 2 * te * d * 2 + 4 * PEER_HEADS * N_KEYS * tm * 4
           + 2 * tm * d * 4 + 2 * tm * te * 2 + 6 * te * tm * 4)
    return pl.pallas_call(
        functools.partial(_peer_mix_kernel, te=te, n_blocks=nj),
        grid=(m // tm, nj + 1),
        in_specs=[
            pl.BlockSpec((d, tm), lambda i, j: (0, i), pipeline_mode=once),
            pl.BlockSpec((te, d), lambda i, j: (jnp.minimum(j, nj - 1), 0)),
            pl.BlockSpec((te, d), lambda i, j: (jnp.maximum(j - 1, 0), 0)),
            sel_spec, sel_spec, sel_spec, sel_spec,
        ],
        out_specs=pl.BlockSpec((tm, d), lambda i, j: (i, 0)),
        out_shape=jax.ShapeDtypeStruct((m, d), BF16),
        scratch_shapes=[pltpu.VMEM((tm, d), F32), pltpu.VMEM((tm, te), BF16), pltpu.VMEM((tm, te), BF16),
                        pltpu.VMEM((te, tm), F32),
                        pltpu.VMEM((PEER_HEADS, N_KEYS, tm), BF16), pltpu.VMEM((PEER_HEADS, N_KEYS, tm), BF16)],
        compiler_params=_params(("parallel", "arbitrary"), est),
        name="peer_mix",
    )(xt, u, v, cnt, c1, r2, e2)


def _rope_tables(pos, batch):
    inv = ROPE_THETA ** (-jnp.arange(0, HD_DIFF, 2, dtype=F32) / HD_DIFF)
    ang = pos.astype(F32)[:, None] * inv[None, :]
    ang = jnp.concatenate([ang, ang], axis=-1)
    sign = jnp.where(jnp.arange(HD_DIFF) < HD_DIFF // 2, -1.0, 1.0).astype(F32)
    cos = jnp.tile(jnp.cos(ang), (batch, 1))
    sin_signed = jnp.tile(jnp.sin(ang) * sign[None, :], (batch, 1))
    return cos, sin_signed


def _prep_weights(w_in, w_gate2, b_gate, lam_q1, lam_k1, lam_q2, lam_k2, diff_norm_g, gla_norm_g, w_out,
                  ln1_g, ln1_b, ln2_g, ln2_b, peer_wq, peer_keys1, peer_keys2, peer_u, peer_v):
    wz = jnp.pad(w_in[:, IN_GATE:IN_GATE + GATE_RANK], ((0, 0), (0, LANES - GATE_RANK)))
    return dict(
        w_in=w_in.astype(BF16),
        wz=wz.astype(BF16),
        w2=jnp.pad(w_gate2, ((0, LANES - GATE_RANK), (0, 0))).astype(BF16),
        bg=b_gate.reshape(1, GLA_QK),
        lam=jnp.stack([lam_q1, lam_k1, lam_q2, lam_k2]),
        dng=diff_norm_g.reshape(1, 2 * HD_DIFF),
        gng=gla_norm_g.reshape(1, GLA_DV),
        wo_a=w_out[:DIFF_W].astype(BF16),
        wo_b=w_out[DIFF_W:].astype(BF16),
        ln1_g=ln1_g.reshape(1, -1), ln1_b=ln1_b.reshape(1, -1),
        ln2_g=ln2_g.reshape(1, -1), ln2_b=ln2_b.reshape(1, -1),
        pwq=peer_wq.astype(BF16),
        pk1=peer_keys1.astype(BF16), pk2=peer_keys2.astype(BF16),
        pu=peer_u.astype(BF16), pv=peer_v.astype(BF16),
    )


def _layer(x, pos, past_k, past_v, s0, chunk, lam_init, alpha, w):
    batch, seq, d = x.shape
    m = batch * seq
    x2d = x.reshape(m, d)
    xb = x2d.astype(BF16)
    cos, sin_signed = _rope_tables(pos, batch)

    q, k32, kb, v32, vb, gp = _project(xb, w["w_in"], cos, sin_signed)
    (gz,) = _matmul([(xb, w["wz"])], _epi_f32, [F32], tn=LANES, name="proj_gate")

    new_k = k32.reshape(batch, seq, H_DIFF, 2 * HD_DIFF)
    new_v = v32.reshape(batch, seq, H_DIFF, 2 * HD_DIFF)
    if past_k is None:
        o_diff = _attn_prompt(q, kb, vb, w["lam"], w["dng"], batch, seq, lam_init)
    else:
        o_diff = _attn_cached(q, new_k, new_v, past_k, past_v, w["lam"], w["dng"], batch, seq, lam_init)
    o_gla, s_new = _gla(gp, gz, w["w2"], w["bg"], w["gng"], s0, batch, seq, chunk)

    (mix,) = _matmul([(o_diff, w["wo_a"]), (o_gla, w["wo_b"])], _epi_f32, [F32], tn=1024, name="out_proj")
    x1, x1b, x1t = _res_layernorm(x2d, mix, w["ln1_g"], w["ln1_b"], alpha, [F32, BF16], [BF16])

    (pq,) = _matmul([(x1b, w["pwq"])], _epi_f32, [F32], name="peer_query")
    cnt, c1, r2, e2 = _peer_select(pq, w["pk1"], w["pk2"])
    y = _peer_mix(x1t, w["pu"], w["pv"], cnt, c1, r2, e2)
    (x2,) = _res_layernorm(x1, y, w["ln2_g"], w["ln2_b"], alpha, [F32])
    return x2.reshape(batch, seq, d), new_k, new_v, s_new


def kernel(x_prompt, x_sample, cache_diff_k, cache_diff_v, state_gla, w_in, w_gate2, b_gate, lam_q1, lam_k1, lam_q2, lam_k2, diff_norm_g, gla_norm_g, w_out, ln1_g, ln1_b, ln2_g, ln2_b, peer_wq, peer_keys1, peer_keys2, peer_u, peer_v):
    depth = w_in.shape[0]
    tp = x_prompt.shape[1]
    ts = x_sample.shape[1]
    past = cache_diff_k.shape[2]
    alpha = (2 * depth) ** 0.25
    pos_p = jnp.arange(tp, dtype=jnp.int32)
    pos_s = past + jnp.arange(ts, dtype=jnp.int32)
    y_p, y_s = x_prompt, x_sample
    outs = [[] for _ in range(6)]
    for l in range(depth):
        lam_init = 0.8 - 0.6 * math.exp(-0.3 * l)
        w = _prep_weights(w_in[l], w_gate2[l], b_gate[l], lam_q1[l], lam_k1[l], lam_q2[l], lam_k2[l],
                          diff_norm_g[l], gla_norm_g[l], w_out[l], ln1_g[l], ln1_b[l], ln2_g[l], ln2_b[l],
                          peer_wq[l], peer_keys1[l], peer_keys2[l], peer_u[l], peer_v[l])
        y_p, kp, vp, sp = _layer(y_p, pos_p, None, None, None, CHUNK, lam_init, alpha, w)
        y_s, k_s, v_s, s_s = _layer(y_s, pos_s, cache_diff_k[l], cache_diff_v[l], state_gla[l],
                                    ts, lam_init, alpha, w)
        for lst, val in zip(outs, (kp, vp, sp, k_s, v_s, s_s)):
            lst.append(val)
    return (y_p, y_s) + tuple(jnp.stack(o) for o in outs)
```

```python
import functools
import math

import jax
import jax.numpy as jnp
from jax import lax
from jax.experimental import pallas as pl
from jax.experimental.pallas import tpu as pltpu

CHUNK = 64
HD_DIFF = 128
H_DIFF = 8
H_GLA = 4
GLA_DK = 256
GLA_DV = 512
GATE_RANK = 16
GLA_TAU = 16.0
ROPE_THETA = 10000.0
N_KEYS = 128
PEER_HEADS = 8
PEER_HALF = 128
PEER_TOPK = 16
EPS = 1e-5
NEG_INF = -1e30
ROW_BLOCK = 256

DIFF_W = H_DIFF * 2 * HD_DIFF
GLA_QK = H_GLA * GLA_DK
GLA_V = H_GLA * GLA_DV
IN_GLA = 3 * DIFF_W
IN_GATE = IN_GLA + 2 * GLA_QK + 2 * GLA_V

LANES = 128
SUBLANES = 8
BF16_ROWS = 2 * SUBLANES
VMEM_CAP = 60000 * 1024

BF16 = jnp.bfloat16
F32 = jnp.float32

NT_DIMS = (((1,), (1,)), ((), ()))
TN_DIMS = (((0,), (0,)), ((), ()))


def _pick(n, cands):
    for c in cands:
        if n % c == 0:
            return c
    return n


def _params(sem, est_bytes):
    limit = int(min(VMEM_CAP, max(32 * 1024 * 1024, est_bytes * 5 // 4)))
    return pltpu.CompilerParams(dimension_semantics=sem, vmem_limit_bytes=limit)


def _mm_kernel(*refs, n_pairs, n_extra, epilogue):
    a_refs = refs[:n_pairs]
    b_refs = refs[n_pairs:2 * n_pairs]
    extras = refs[2 * n_pairs:2 * n_pairs + n_extra]
    outs = refs[2 * n_pairs + n_extra:]
    acc = jnp.dot(a_refs[0][...], b_refs[0][...], preferred_element_type=F32)
    for a, b in zip(a_refs[1:], b_refs[1:]):
        acc = acc + jnp.dot(a[...], b[...], preferred_element_type=F32)
    epilogue(acc, extras, outs)


def _matmul(pairs, epilogue, out_dtypes, row_extras=(), tn=512, name="matmul", cols=None):
    m = pairs[0][0].shape[0]
    c0, n = cols if cols is not None else (0, pairs[0][1].shape[1])
    tm = _pick(m, (1024, 512, 256, 128, 64, 32, 16, 8))
    tn = min(tn, n)
    assert n % tn == 0 and c0 % tn == 0
    jb = c0 // tn
    in_specs = []
    est = 0
    for a, _ in pairs:
        in_specs.append(pl.BlockSpec((tm, a.shape[1]), lambda i, j: (i, 0)))
        est += 2 * tm * a.shape[1] * a.dtype.itemsize
    for _, b in pairs:
        in_specs.append(pl.BlockSpec((b.shape[0], tn), lambda i, j: (0, jb + j)))
        est += 2 * b.shape[0] * tn * b.dtype.itemsize
    for e in row_extras:
        in_specs.append(pl.BlockSpec((tm, e.shape[1]), lambda i, j: (i, 0)))
        est += 2 * tm * e.shape[1] * 4
    out_specs = [pl.BlockSpec((tm, tn), lambda i, j: (i, j)) for _ in out_dtypes]
    out_shape = [jax.ShapeDtypeStruct((m, n), d) for d in out_dtypes]
    est += sum(2 * tm * tn * jnp.dtype(d).itemsize for d in out_dtypes) + 3 * tm * tn * 4
    kern = functools.partial(_mm_kernel, n_pairs=len(pairs), n_extra=len(row_extras), epilogue=epilogue)
    return pl.pallas_call(
        kern,
        grid=(m // tm, n // tn),
        in_specs=in_specs,
        out_specs=out_specs,
        out_shape=out_shape,
        compiler_params=_params(("parallel", "arbitrary"), est),
        name=name,
    )(*[a for a, _ in pairs], *[b for _, b in pairs], *row_extras)


def _rope_tile(acc, cos, sin_signed):
    outs = []
    for g in range(acc.shape[1] // LANES):
        xg = acc[:, g * LANES:(g + 1) * LANES]
        outs.append(xg * cos + pltpu.roll(xg, LANES // 2, 1) * sin_signed)
    return outs[0] if len(outs) == 1 else jnp.concatenate(outs, axis=1)


def _epi_q(acc, extras, outs):
    r = _rope_tile(acc, extras[0][...], extras[1][...])
    outs[0][...] = (r * (HD_DIFF ** -0.5 * math.log2(math.e))).astype(BF16)


def _epi_k(acc, extras, outs):
    r = _rope_tile(acc, extras[0][...], extras[1][...])
    outs[0][...] = r
    outs[1][...] = r.astype(BF16)


def _epi_f32_bf16(acc, extras, outs):
    outs[0][...] = acc
    outs[1][...] = acc.astype(BF16)


def _epi_f32(acc, extras, outs):
    outs[0][...] = acc


def _lambda_value(lam_ref, lam_init):
    lq1 = lam_ref[0:1, :]
    lk1 = lam_ref[1:2, :]
    lq2 = lam_ref[2:3, :]
    lk2 = lam_ref[3:4, :]
    return (jnp.exp(jnp.sum(lq1 * lk1, axis=1, keepdims=True))
            - jnp.exp(jnp.sum(lq2 * lk2, axis=1, keepdims=True)) + lam_init)


def _diff_finalize(o1, o2, lam_ref, g_ref, lam_init):
    lam = _lambda_value(lam_ref, lam_init)
    o = o1 - lam * o2
    ms = jnp.mean(o * o, axis=1, keepdims=True)
    return (o * lax.rsqrt(ms + EPS)) * g_ref[...] * (1.0 - lam_init)


def _attn_prompt_kernel(qi_ref, kj_ref, q_ref, k_ref, v_ref, lam_ref, g_ref, o_ref,
                        m1, l1, a1, m2, l2, a2, *, tq, lam_init):
    p = pl.program_id(2)
    i = qi_ref[p]
    j = kj_ref[p]
    half_t = tq // 2

    @pl.when(j == 0)
    def _():
        m1[...] = jnp.full_like(m1, NEG_INF)
        m2[...] = jnp.full_like(m2, NEG_INF)
        l1[...] = jnp.zeros_like(l1)
        l2[...] = jnp.zeros_like(l2)
        a1[...] = jnp.zeros_like(a1)
        a2[...] = jnp.zeros_like(a2)

    def update(r0, nr, nc, masked):
        if nr > ROW_BLOCK:
            for rr in range(r0, r0 + nr, ROW_BLOCK):
                update(rr, ROW_BLOCK, nc, masked)
            return
        rows = slice(r0, r0 + nr)
        q = q_ref[rows, :]
        k = k_ref[0:nc, :]
        v = v_ref[0:nc, :]
        if masked:
            qc = (r0 + lax.broadcasted_iota(jnp.int32, (nr, nc), 0)) // CHUNK
            kc = lax.broadcasted_iota(jnp.int32, (nr, nc), 1) // CHUNK
            mask = qc >= kc
        for half, (m_ref, l_ref, a_ref) in enumerate(((m1, l1, a1), (m2, l2, a2))):
            qh = q[:, half * HD_DIFF:(half + 1) * HD_DIFF]
            kh = k[:, half * HD_DIFF:(half + 1) * HD_DIFF]
            s = lax.dot_general(qh, kh, NT_DIMS, preferred_element_type=F32)
            if masked:
                s = jnp.where(mask, s, NEG_INF)
            m_old = m_ref[rows, :]
            m_new = jnp.maximum(m_old, jnp.max(s, axis=1, keepdims=True))
            alpha = jnp.exp2(m_old - m_new)
            pr = jnp.exp2(s - m_new)
            l_ref[rows, :] = alpha * l_ref[rows, :] + jnp.sum(pr, axis=1, keepdims=True)
            a_ref[rows, :] = alpha * a_ref[rows, :] + jnp.dot(pr.astype(BF16), v, preferred_element_type=F32)
            m_ref[rows, :] = m_new

    @pl.when(j < i)
    def _():
        update(0, tq, tq, False)

    @pl.when(j == i)
    def _():
        if half_t % CHUNK == 0:
            update(0, half_t, half_t, True)
            update(half_t, half_t, tq, True)
        else:
            update(0, tq, tq, True)
        o = _diff_finalize(a1[...] / l1[...], a2[...] / l2[...], lam_ref, g_ref, lam_init)
        o_ref[...] = o.astype(o_ref.dtype)


def _attn_prompt(q, k, v, lam_vecs, g, batch, seq, lam_init):
    tq = _pick(seq, (2048, 1024, 512, 256, 128, 64))
    nq = seq // tq
    hw = 2 * HD_DIFF
    pairs =[(i, j) for i in range(nq) for j in range(i + 1)]
    qi = jnp.asarray([p[0] for p in pairs], jnp.int32)
    kj = jnp.asarray([p[1] for p in pairs], jnp.int32)

    def q_map(b, h, p, qi_ref, kj_ref):
        return (b * nq + qi_ref[p], h)

    def kv_map(b, h, p, qi_ref, kj_ref):
        return (b * nq + kj_ref[p], h)

    kern = functools.partial(_attn_prompt_kernel, tq=tq, lam_init=lam_init)
    est = 2 * 4 * tq * hw * 2 + 2 * tq * hw * 4 + 4 * tq * LANES * 4 + 8 * tq * tq * 4
    return pl.pallas_call(
        kern,
        grid_spec=pltpu.PrefetchScalarGridSpec(
            num_scalar_prefetch=2,
            grid=(batch, H_DIFF, len(pairs)),
            in_specs=[
                pl.BlockSpec((tq, hw), q_map),
                pl.BlockSpec((tq, hw), kv_map),
                pl.BlockSpec((tq, hw), kv_map),
                pl.BlockSpec((4, HD_DIFF), lambda b, h, p, qi_ref, kj_ref: (0, 0)),
                pl.BlockSpec((1, hw), lambda b, h, p, qi_ref, kj_ref: (0, 0)),
            ],
            out_specs=pl.BlockSpec((tq, hw), q_map),
            scratch_shapes=[
                pltpu.VMEM((tq, 1), F32), pltpu.VMEM((tq, 1), F32), pltpu.VMEM((tq, hw), F32),
                pltpu.VMEM((tq, 1), F32), pltpu.VMEM((tq, 1), F32), pltpu.VMEM((tq, hw), F32),
            ],
        ),
        out_shape=jax.ShapeDtypeStruct((batch * seq, DIFF_W), BF16),
        compiler_params=_params(("parallel", "parallel", "arbitrary"), est),
        name="diff_attn_prompt",
    )(qi, kj, q, k, v, lam_vecs, g)


def _attn_cached_kernel(q_ref, kn_ref, vn_ref, kc_ref, vc_ref, bc_ref, bn_ref, lam_ref, g_ref, o_ref,
                        *, past, lam_init):
    ts = q_ref.shape[0]
    hw = 2 * HD_DIFF
    kc = kc_ref[0].reshape(past * H_DIFF, hw).astype(BF16)
    vc = vc_ref[0].reshape(past * H_DIFF, hw).astype(BF16)
    kn = kn_ref[0].reshape(ts * H_DIFF, hw).astype(BF16)
    vn = vn_ref[0].reshape(ts * H_DIFF, hw).astype(BF16)
    halves = []
    for half in range(2):
        sl = slice(half * HD_DIFF, (half + 1) * HD_DIFF)
        qh = jnp.concatenate([q_ref[:, h * hw + half * HD_DIFF:h * hw + (half + 1) * HD_DIFF]
                              for h in range(H_DIFF)], axis=0)
        sc = lax.dot_general(kc[:, sl], qh, NT_DIMS, preferred_element_type=F32) + bc_ref[...]
        sn = lax.dot_general(kn[:, sl], qh, NT_DIMS, preferred_element_type=F32) + bn_ref[...]
        m = jnp.maximum(jnp.max(sc, axis=0, keepdims=True), jnp.max(sn, axis=0, keepdims=True))
        pc = jnp.exp2(sc - m)
        pn = jnp.exp2(sn - m)
        l = jnp.sum(pc, axis=0, keepdims=True) + jnp.sum(pn, axis=0, keepdims=True)
        acc = (jnp.dot(pc.T.astype(BF16), vc, preferred_element_type=F32)
               + jnp.dot(pn.T.astype(BF16), vn, preferred_element_type=F32))
        l_col = jnp.broadcast_to(l, (l.shape[1], l.shape[1])).T
        halves.append(acc / jnp.concatenate([l_col] * (hw // l.shape[1]), axis=1))
    for h in range(H_DIFF):
        rows = slice(h * ts, (h + 1) * ts)
        o = _diff_finalize(halves[0][rows], halves[1][rows], lam_ref, g_ref, lam_init)
        o_ref[:, h * hw:(h + 1) * hw] = o.astype(o_ref.dtype)


def _attn_cached(q, kn, vn, cache_k, cache_v, lam_vecs, g, batch, seq, lam_init):
    past = cache_k.shape[1]
    hw = 2 * HD_DIFF
    assert H_DIFF * seq == LANES
    col_head = jnp.arange(LANES) // seq
    col_chunk = (past + jnp.arange(LANES) % seq) // CHUNK

    def bias(key_pos):
        row_head = jnp.tile(jnp.arange(H_DIFF), key_pos.shape[0])
        row_chunk = jnp.repeat(key_pos, H_DIFF) // CHUNK
        ok = (row_head[:, None] == col_head[None, :]) & (col_chunk[None, :] >= row_chunk[:, None])
        return jnp.where(ok, 0.0, NEG_INF).astype(F32)

    bias_c = bias(jnp.arange(past))
    bias_n = bias(past + jnp.arange(seq))
    kern = functools.partial(_attn_cached_kernel, past=past, lam_init=lam_init)
    est = 2 * past * DIFF_W * 4 + 2 * past * DIFF_W * 2 + 8 * past * H_DIFF * LANES * 4
    once = pl.Buffered(1)
    cache_spec = pl.BlockSpec((1, past, H_DIFF, hw), lambda b: (b, 0, 0, 0), pipeline_mode=once)
    new_spec = pl.BlockSpec((1, seq, H_DIFF, hw), lambda b: (b, 0, 0, 0))
    row_spec = pl.BlockSpec((seq, DIFF_W), lambda b: (b, 0))
    return pl.pallas_call(
        kern,
        grid=(batch,),
        in_specs=[
            row_spec, new_spec, new_spec, cache_spec, cache_spec,
            pl.BlockSpec(bias_c.shape, lambda b: (0, 0), pipeline_mode=once),
            pl.BlockSpec(bias_n.shape, lambda b: (0, 0)),
            pl.BlockSpec((4, HD_DIFF), lambda b: (0, 0)),
            pl.BlockSpec((1, hw), lambda b: (0, 0)),
        ],
        out_specs=row_spec,
        out_shape=jax.ShapeDtypeStruct((batch * seq, DIFF_W), BF16),
        compiler_params=_params(("parallel",), est),
        name="diff_attn_cached",
    )(q, kn, vn, cache_k, cache_v, bias_c, bias_n, lam_vecs, g)


def _split3(x):
    hi = x.astype(BF16)
    r1 = x - hi.astype(F32)
    mid = r1.astype(BF16)
    lo = (r1 - mid.astype(F32)).astype(BF16)
    return hi, mid, lo


def _gla_kernel(*refs, chunk, n_chunks, has_s0):
    if has_s0:
        q_ref, k_ref, v_ref, r_ref, z_ref, w2_ref, bg_ref, gn_ref, s0_ref, o_ref, s_ref, st = refs
    else:
        q_ref, k_ref, v_ref, r_ref, z_ref, w2_ref, bg_ref, gn_ref, o_ref, s_ref, st = refs
    i = pl.program_id(2)

    @pl.when(i == 0)
    def _():
        if has_s0:
            st[...] = s0_ref[0, 0].T
        else:
            st[...] = jnp.zeros_like(st)

    scale = GLA_DK ** -0.5
    zz = jnp.dot(z_ref[...].astype(BF16), w2_ref[...], preferred_element_type=F32) + bg_ref[...]
    logf = jax.nn.log_sigmoid(zz) / GLA_TAU
    row = lax.broadcasted_iota(jnp.int32, (chunk, chunk), 0)
    col = lax.broadcasted_iota(jnp.int32, (chunk, chunk), 1)
    causal = row >= col
    tri = jnp.where(causal, 1.0, 0.0).astype(BF16)
    for c in range(n_chunks):
        rows = slice(c * chunk, (c + 1) * chunk)
        g = logf[rows]
        bcum = sum(jnp.dot(tri, part, preferred_element_type=F32) for part in _split3(g))
        b_last = bcum[chunk - 1:chunk, :]
        qc = q_ref[rows, :]
        kc = k_ref[rows, :]
        vc = v_ref[rows, :].astype(BF16)
        q_in = (qc * jnp.exp(bcum) * scale).astype(BF16)
        k_in = (kc * jnp.exp(-bcum)).astype(BF16)
        k_end = (kc * jnp.exp(b_last - bcum)).astype(BF16)
        a = lax.dot_general(q_in, k_in, NT_DIMS, preferred_element_type=F32)
        a = jnp.where(causal, a, 0.0)
        s_old = st[...]
        o = (jnp.dot(a.astype(BF16), vc, preferred_element_type=F32)
             + lax.dot_general(q_in, s_old.astype(BF16), NT_DIMS, preferred_element_type=F32))
        st[...] = s_old * jnp.exp(b_last) + lax.dot_general(vc, k_end, TN_DIMS, preferred_element_type=F32)
        ms = jnp.mean(o * o, axis=1, keepdims=True)
        on = (o * lax.rsqrt(ms + EPS)) * gn_ref[...]
        gate = r_ref[rows, :]
        o_ref[rows, :] = (on * (gate * jax.nn.sigmoid(gate))).astype(o_ref.dtype)

    @pl.when(i == pl.num_programs(2) - 1)
    def _():
        s_ref[0, 0] = st[...].T


def _gla(gp, gz, w2, bg, gn, s0, batch, seq, chunk):
    tb = chunk * max(1, min(8, seq // chunk))
    while seq % tb:
        tb -= chunk
    nb = seq // tb
    kq = GLA_QK // GLA_DK
    kv = (2 * GLA_QK) // GLA_DV
    kr = kv + GLA_V // GLA_DV
    in_specs = [
        pl.BlockSpec((tb, GLA_DK), lambda b, h, i: (b * nb + i, h)),
        pl.BlockSpec((tb, GLA_DK), lambda b, h, i: (b * nb + i, kq + h)),
        pl.BlockSpec((tb, GLA_DV), lambda b, h, i: (b * nb + i, kv + h)),
        pl.BlockSpec((tb, GLA_DV), lambda b, h, i: (b * nb + i, kr + h)),
        pl.BlockSpec((tb, LANES), lambda b, h, i: (b * nb + i, 0)),
        pl.BlockSpec((LANES, GLA_DK), lambda b, h, i: (0, h)),
        pl.BlockSpec((1, GLA_DK), lambda b, h, i: (0, h)),
        pl.BlockSpec((1, GLA_DV), lambda b, h, i: (0, 0)),
    ]
    args = [gp, gp, gp, gp, gz, w2, bg, gn]
    if s0 is not None:
        in_specs.append(pl.BlockSpec((1, 1, GLA_DK, GLA_DV), lambda b, h, i: (b, h, 0, 0)))
        args.append(s0)
    kern = functools.partial(_gla_kernel, chunk=chunk, n_chunks=tb // chunk, has_s0=s0 is not None)
    est = 2 * tb * (2 * GLA_DK + 2 * GLA_DV + LANES) * 4 + 2 * tb * GLA_DV * 2 + 7 * GLA_DK * GLA_DV * 4
    return pl.pallas_call(
        kern,
        grid=(batch, H_GLA, nb),
        in_specs=in_specs,
        out_specs=[
            pl.BlockSpec((tb, GLA_DV), lambda b, h, i: (b * nb + i, h)),
            pl.BlockSpec((1, 1, GLA_DK, GLA_DV), lambda b, h, i: (b, h, 0, 0)),
        ],
        out_shape=[
            jax.ShapeDtypeStruct((batch * seq, GLA_V), BF16),
            jax.ShapeDtypeStruct((batch, H_GLA, GLA_DK, GLA_DV), F32),
        ],
        scratch_shapes=[pltpu.VMEM((GLA_DV, GLA_DK), F32)],
        compiler_params=_params(("parallel", "parallel", "arbitrary"), est),
        name="gla",
    )(*args)


def _ln_kernel(x_ref, y_ref, g_ref, b_ref, *outs, alpha, n_plain):
    h = alpha * x_ref[...] + y_ref[...].astype(F32)
    mu = jnp.mean(h, axis=1, keepdims=True)
    d = h - mu
    var = jnp.mean(d * d, axis=1, keepdims=True)
    r = (d * lax.rsqrt(var + EPS)) * g_ref[...] + b_ref[...]
    for o in outs[:n_plain]:
        o[...] = r.astype(o.dtype)
    for o in outs[n_plain:]:
        o[...] = r.T.astype(o.dtype)


def _res_layernorm(x, y, g, b, alpha, out_dtypes, transposed_dtypes=()):
    m, d = x.shape
    tm = _pick(m, (128,)) if transposed_dtypes else _pick(m, (128, 64, 32, 16, 8))
    assert m % tm == 0
    sizes = [jnp.dtype(t).itemsize for t in (*out_dtypes, *transposed_dtypes)]
    est = 2 * tm * d * (x.dtype.itemsize + y.dtype.itemsize + sum(sizes)) + 5 * tm * d * 4
    return pl.pallas_call(
        functools.partial(_ln_kernel, alpha=alpha, n_plain=len(out_dtypes)),
        grid=(m // tm,),
        in_specs=[
            pl.BlockSpec((tm, d), lambda i: (i, 0)),
            pl.BlockSpec((tm, d), lambda i: (i, 0)),
            pl.BlockSpec((1, d), lambda i: (0, 0)),
            pl.BlockSpec((1, d), lambda i: (0, 0)),
        ],
        out_specs=[pl.BlockSpec((tm, d), lambda i: (i, 0)) for _ in out_dtypes]
        + [pl.BlockSpec((d, tm), lambda i: (0, i)) for _ in transposed_dtypes],
        out_shape=[jax.ShapeDtypeStruct((m, d), t) for t in out_dtypes]
        + [jax.ShapeDtypeStruct((d, m), t) for t in transposed_dtypes],
        compiler_params=_params(("parallel",), est),
        name="res_layernorm",
    )(x, y, g, b)


NOT_RANKED = float(N_KEYS - 1)


def _top_values(s, count):
    vals = []
    cur = s
    for t in range(count):
        m = jnp.max(cur, axis=0, keepdims=True)
        vals.append(m)
        if t + 1 < count:
            cur = jnp.where(cur == m, -jnp.inf, cur)
    return vals


def _peer_select_kernel(q_ref, k1_ref, k2_ref, cnt_ref, c1_ref, r2_ref, e2_ref, cand_ref):
    pairs = [(i, j) for i in range(PEER_TOPK) for j in range(PEER_TOPK) if (i + 1) * (j + 1) <= PEER_TOPK]
    for h in range(PEER_HEADS):
        base = h * 2 * PEER_HALF
        q1 = q_ref[:, base:base + PEER_HALF].astype(BF16)
        q2 = q_ref[:, base + PEER_HALF:base + 2 * PEER_HALF].astype(BF16)
        s1 = lax.dot_general(k1_ref[h], q1, NT_DIMS, preferred_element_type=F32)
        s2 = lax.dot_general(k2_ref[h], q2, NT_DIMS, preferred_element_type=F32)
        t1 = _top_values(s1, PEER_TOPK)
        t2 = _top_values(s2, PEER_TOPK)
        cand_ref[...] = jnp.full_like(cand_ref, -jnp.inf)
        for r, (i, j) in enumerate(pairs):
            cand_ref[r:r + 1, :] = t1[i] + t2[j]
        best = _top_values(cand_ref[...], PEER_TOPK)
        tau = best[PEER_TOPK - 1]
        z = sum(jnp.exp(bv - best[0]) for bv in best)
        counts = [sum(jnp.where(t1[i] + t2[j] >= tau, 1.0, 0.0) for j in range(PEER_TOPK) if (i, j) in pairs)
                  for i in range(PEER_TOPK)]
        cnt = jnp.zeros_like(s1)
        rank2 = jnp.full_like(s2, NOT_RANKED)
        for i in range(PEER_TOPK):
            cnt = jnp.where(s1 == t1[i], counts[i], cnt)
            rank2 = jnp.where(s2 == t2[i], float(i), rank2)
        cnt_ref[h] = cnt
        c1_ref[h] = jnp.exp(s1 - t1[0]) / z
        r2_ref[h] = rank2
        e2_ref[h] = jnp.exp(s2 - t2[0])


def _peer_select(q, keys1, keys2):
    m = q.shape[0]
    tm = _pick(m, (256, 128))
    assert m % tm == 0
    n_pairs = sum((i + 1) * (j + 1) <= PEER_TOPK for i in range(PEER_TOPK) for j in range(PEER_TOPK))
    cand_rows = -(-n_pairs // 8) * 8
    spec = pl.BlockSpec((PEER_HEADS, N_KEYS, tm), lambda i: (0, 0, i))
    est = 2 * tm * q.shape[1] * 4 + 2 * 4 * PEER_HEADS * N_KEYS * tm * 4 + (8 << 20)
    return pl.pallas_call(
        _peer_select_kernel,
        grid=(m // tm,),
        in_specs=[
            pl.BlockSpec((tm, q.shape[1]), lambda i: (i, 0)),
            pl.BlockSpec(keys1.shape, lambda i: (0, 0, 0)),
            pl.BlockSpec(keys2.shape, lambda i: (0, 0, 0)),
        ],
        out_specs=[spec, spec, spec, spec],
        out_shape=[jax.ShapeDtypeStruct((PEER_HEADS, N_KEYS, m), F32)] * 4,
        scratch_shapes=[pltpu.VMEM((cand_rows, tm), F32)],
        compiler_params=_params(("parallel",), est),
        name="peer_select",
    )(q, keys1, keys2)


def _peer_mix_kernel(xt_ref, u_ref, v_ref, cnt_ref, c1_ref, r2_ref, e2_ref, o_ref,
                     y_ref, coef_a, coef_b, act_ref, r2b_ref, e2b_ref, *, te, n_blocks):
    j = pl.program_id(1)
    d, tm = xt_ref.shape
    groups = te // N_KEYS
    jb = jnp.minimum(j, n_blocks - 1)
    pieces = [(g, c) for c in range(tm // LANES) for g in range(groups)]
    col_w = 2 * LANES
    n_vchunks = d // col_w

    @pl.when(j == 0)
    def _():
        y_ref[...] = jnp.zeros_like(y_ref)
        coef_b[...] = jnp.zeros_like(coef_b)
        r2b_ref[...] = r2_ref[...].astype(BF16)
        e2b_ref[...] = e2_ref[...].astype(BF16)

    def select_weights(g, c):
        a = jb * groups + g
        lanes = slice(c * LANES, (c + 1) * LANES)
        rows = (BF16_ROWS, LANES)
        cnt = [jnp.broadcast_to(cnt_ref[h, pl.ds(a, 1), :][:, lanes], rows).astype(BF16) for h in range(PEER_HEADS)]
        c1a = [jnp.broadcast_to(c1_ref[h, pl.ds(a, 1), :][:, lanes], rows).astype(BF16) for h in range(PEER_HEADS)]
        out = []
        for r in range(N_KEYS // BF16_ROWS):
            bs = slice(r * BF16_ROWS, (r + 1) * BF16_ROWS)
            acc = None
            for h in range(PEER_HEADS):
                wh = jnp.where(r2b_ref[h, bs, lanes] < cnt[h], e2b_ref[h, bs, lanes], 0.0) * c1a[h]
                acc = wh if acc is None else acc + wh
            out.append(acc)
        return jnp.concatenate(out, axis=0).astype(F32)

    def finish_piece(cur, g, c):
        rows = slice(g * N_KEYS, (g + 1) * N_KEYS)
        lanes = slice(c * LANES, (c + 1) * LANES)
        x = act_ref[rows, lanes]
        x = 0.5 * x * (1.0 + lax.erf(x * (2.0 ** -0.5)))
        cur[lanes, rows] = (x * select_weights(g, c)).T.astype(BF16)

    def step(cur, prev):
        if cur is not None:
            act_ref[...] = jnp.dot(u_ref[...], xt_ref[...], preferred_element_type=F32)
        for n in range(n_vchunks):
            cs = slice(n * col_w, (n + 1) * col_w)
            y_ref[:, cs] += jnp.dot(prev[...], v_ref[:, cs], preferred_element_type=F32)
            if cur is not None:
                for g, c in pieces[n::n_vchunks]:
                    finish_piece(cur, g, c)

    @pl.when(jnp.logical_and(j % 2 == 0, j < n_blocks))
    def _():
        step(coef_a, coef_b)

    @pl.when(jnp.logical_and(j % 2 == 1, j < n_blocks))
    def _():
        step(coef_b, coef_a)

    @pl.when(j == n_blocks)
    def _():
        step(None, coef_a if n_blocks % 2 else coef_b)
        o_ref[...] = y_ref[...].astype(o_ref.dtype)


def _peer_mix(xt, u, v, cnt, c1, r2, e2):
    d, m = xt.shape
    ne = v.shape[0]
    tm = _pick(m, (512, 256, 128))
    te = 512
    assert m % tm == 0 and tm % LANES == 0 and ne % te == 0 and te % N_KEYS == 0
    nj = ne // te
    once = pl.Buffered(1)
    sel_spec = pl.BlockSpec((PEER_HEADS, N_KEYS, tm), lambda i, j: (0, 0, i), pipeline_mode=once)
    est = (tm * d * 2 + 2 * 2 * te * d * 2 + 4 * PEER_HEADS * N_KEYS * tm * 4
           + 2 * tm * d * 4 + 2 * tm * te * 2 + 6 * te * tm * 4)
    return pl.pallas_call(
        functools.partial(_peer_mix_kernel, te=te, n_blocks=nj),
        grid=(m // tm, nj + 1),
        in_specs=[
            pl.BlockSpec((d, tm), lambda i, j: (0, i), pipeline_mode=once),
            pl.BlockSpec((te, d), lambda i, j: (jnp.minimum(j, nj - 1), 0)),
            pl.BlockSpec((te, d), lambda i, j: (jnp.maximum(j - 1, 0), 0)),
            sel_spec, sel_spec, sel_spec, sel_spec,
        ],
        out_specs=pl.BlockSpec((tm, d), lambda i, j: (i, 0)),
        out_shape=jax.ShapeDtypeStruct((m, d), BF16),
        scratch_shapes=[pltpu.VMEM((tm, d), F32), pltpu.VMEM((tm, te), BF16), pltpu.VMEM((tm, te), BF16),
                        pltpu.VMEM((te, tm), F32),
                        pltpu.VMEM((PEER_HEADS, N_KEYS, tm), BF16), pltpu.VMEM((PEER_HEADS, N_KEYS, tm), BF16)],
        compiler_params=_params(("parallel", "arbitrary"), est),
        name="peer_mix",
    )(xt, u, v, cnt, c1, r2, e2)


def _rope_tables(pos, batch):
    inv = ROPE_THETA ** (-jnp.arange(0, HD_DIFF, 2, dtype=F32) / HD_DIFF)
    ang = pos.astype(F32)[:, None] * inv[None, :]
    ang = jnp.concatenate([ang, ang], axis=-1)
    sign = jnp.where(jnp.arange(HD_DIFF) < HD_DIFF // 2, -1.0, 1.0).astype(F32)
    cos = jnp.tile(jnp.cos(ang), (batch, 1))
    sin_signed = jnp.tile(jnp.sin(ang) * sign[None, :], (batch, 1))
    return cos, sin_signed


def _prep_weights(w_in, w_gate2, b_gate, lam_q1, lam_k1, lam_q2, lam_k2, diff_norm_g, gla_norm_g, w_out,
                  ln1_g, ln1_b, ln2_g, ln2_b, peer_wq, peer_keys1, peer_keys2, peer_u, peer_v):
    wz = jnp.pad(w_in[:, IN_GATE:IN_GATE + GATE_RANK], ((0, 0), (0, LANES - GATE_RANK)))
    return dict(
        w_in=w_in.astype(BF16),
        wz=wz.astype(BF16),
        w2=jnp.pad(w_gate2, ((0, LANES - GATE_RANK), (0, 0))).astype(BF16),
        bg=b_gate.reshape(1, GLA_QK),
        lam=jnp.stack([lam_q1, lam_k1, lam_q2, lam_k2]),
        dng=diff_norm_g.reshape(1, 2 * HD_DIFF),
        gng=gla_norm_g.reshape(1, GLA_DV),
        wo_a=w_out[:DIFF_W].astype(BF16),
        wo_b=w_out[DIFF_W:].astype(BF16),
        ln1_g=ln1_g.reshape(1, -1), ln1_b=ln1_b.reshape(1, -1),
        ln2_g=ln2_g.reshape(1, -1), ln2_b=ln2_b.reshape(1, -1),
        pwq=peer_wq.astype(BF16),
        pk1=peer_keys1.astype(BF16), pk2=peer_keys2.astype(BF16),
        pu=peer_u.astype(BF16), pv=peer_v.astype(BF16),
    )


def _layer(x, pos, past_k, past_v, s0, chunk, lam_init, alpha, w):
    batch, seq, d = x.shape
    m = batch * seq
    x2d = x.reshape(m, d)
    xb = x2d.astype(BF16)
    cos, sin_signed = _rope_tables(pos, batch)

    proj = [(xb, w["w_in"])]
    (q,) = _matmul(proj, _epi_q, [BF16], (cos, sin_signed), name="proj_q", cols=(0, DIFF_W))
    k32, kb = _matmul(proj, _epi_k, [F32, BF16], (cos, sin_signed), name="proj_k", cols=(DIFF_W, DIFF_W))
    v32, vb = _matmul(proj, _epi_f32_bf16, [F32, BF16], name="proj_v", cols=(2 * DIFF_W, DIFF_W))
    (gp,) = _matmul(proj, _epi_f32, [F32], name="proj_gla", cols=(IN_GLA, IN_GATE - IN_GLA))
    (gz,) = _matmul([(xb, w["wz"])], _epi_f32, [F32], tn=LANES, name="proj_gate")

    new_k = k32.reshape(batch, seq, H_DIFF, 2 * HD_DIFF)
    new_v = v32.reshape(batch, seq, H_DIFF, 2 * HD_DIFF)
    if past_k is None:
        o_diff = _attn_prompt(q, kb, vb, w["lam"], w["dng"], batch, seq, lam_init)
    else:
        o_diff = _attn_cached(q, new_k, new_v, past_k, past_v, w["lam"], w["dng"], batch, seq, lam_init)
    o_gla, s_new = _gla(gp, gz, w["w2"], w["bg"], w["gng"], s0, batch, seq, chunk)

    (mix,) = _matmul([(o_diff, w["wo_a"]), (o_gla, w["wo_b"])], _epi_f32, [F32], tn=1024, name="out_proj")
    x1, x1b, x1t = _res_layernorm(x2d, mix, w["ln1_g"], w["ln1_b"], alpha, [F32, BF16], [BF16])

    (pq,) = _matmul([(x1b, w["pwq"])], _epi_f32, [F32], name="peer_query")
    cnt, c1, r2, e2 = _peer_select(pq, w["pk1"], w["pk2"])
    y = _peer_mix(x1t, w["pu"], w["pv"], cnt, c1, r2, e2)
    (x2,) = _res_layernorm(x1, y, w["ln2_g"], w["ln2_b"], alpha, [F32])
    return x2.reshape(batch, seq, d), new_k, new_v, s_new


def kernel(x_prompt, x_sample, cache_diff_k, cache_diff_v, state_gla, w_in, w_gate2, b_gate, lam_q1, lam_k1, lam_q2, lam_k2, diff_norm_g, gla_norm_g, w_out, ln1_g, ln1_b, ln2_g, ln2_b, peer_wq, peer_keys1, peer_keys2, peer_u, peer_v):
    depth = w_in.shape[0]
    tp = x_prompt.shape[1]
    ts = x_sample.shape[1]
    past = cache_diff_k.shape[2]
    alpha = (2 * depth) ** 0.25
    pos_p = jnp.arange(tp, dtype=jnp.int32)
    pos_s = past + jnp.arange(ts, dtype=jnp.int32)
    y_p, y_s = x_prompt, x_sample
    outs = [[] for _ in range(6)]
    for l in range(depth):
        lam_init = 0.8 - 0.6 * math.exp(-0.3 * l)
        w = _prep_weights(w_in[l], w_gate2[l], b_gate[l], lam_q1[l], lam_k1[l], lam_q2[l], lam_k2[l],
                          diff_norm_g[l], gla_norm_g[l], w_out[l], ln1_g[l], ln1_b[l], ln2_g[l], ln2_b[l],
                          peer_wq[l], peer_keys1[l], peer_keys2[l], peer_u[l], peer_v[l])
        y_p, kp, vp, sp = _layer(y_p, pos_p, None, None, None, CHUNK, lam_init, alpha, w)
        y_s, k_s, v_s, s_s = _layer(y_s, pos_s, cache_diff_k[l], cache_diff_v[l], state_gla[l],
                                    ts, lam_init, alpha, w)
        for lst, val in zip(outs, (kp, vp, sp, k_s, v_s, s_s)):
            lst.append(val)
    return (y_p, y_s) + tuple(jnp.stack(o) for o in outs)
```

```python
import functools
import math

import jax
import jax.numpy as jnp
from jax import lax
from jax.experimental import pallas as pl
from jax.experimental.pallas import tpu as pltpu

CHUNK = 64
HD_DIFF = 128
H_DIFF = 8
H_GLA = 4
GLA_DK = 256
GLA_DV = 512
GATE_RANK = 16
GLA_TAU = 16.0
ROPE_THETA = 10000.0
N_KEYS = 128
PEER_HEADS = 8
PEER_HALF = 128
PEER_TOPK = 16
EPS = 1e-5
NEG_INF = -1e30
ROW_BLOCK = 256

DIFF_W = H_DIFF * 2 * HD_DIFF
GLA_QK = H_GLA * GLA_DK
GLA_V = H_GLA * GLA_DV
IN_GLA = 3 * DIFF_W
IN_GATE = IN_GLA + 2 * GLA_QK + 2 * GLA_V

LANES = 128
SUBLANES = 8
BF16_ROWS = 2 * SUBLANES
VMEM_CAP = 60000 * 1024

BF16 = jnp.bfloat16
F32 = jnp.float32

NT_DIMS = (((1,), (1,)), ((), ()))
TN_DIMS = (((0,), (0,)), ((), ()))


def _pick(n, cands):
    for c in cands:
        if n % c == 0:
            return c
    return n


def _params(sem, est_bytes):
    limit = int(min(VMEM_CAP, max(32 * 1024 * 1024, est_bytes * 5 // 4)))
    return pltpu.CompilerParams(dimension_semantics=sem, vmem_limit_bytes=limit)


def _mm_kernel(*refs, n_pairs, n_extra, epilogue):
    a_refs = refs[:n_pairs]
    b_refs = refs[n_pairs:2 * n_pairs]
    extras = refs[2 * n_pairs:2 * n_pairs + n_extra]
    outs = refs[2 * n_pairs + n_extra:]
    acc = jnp.dot(a_refs[0][...], b_refs[0][...], preferred_element_type=F32)
    for a, b in zip(a_refs[1:], b_refs[1:]):
        acc = acc + jnp.dot(a[...], b[...], preferred_element_type=F32)
    epilogue(acc, extras, outs)


def _matmul(pairs, epilogue, out_dtypes, row_extras=(), tn=512, name="matmul", cols=None):
    m = pairs[0][0].shape[0]
    c0, n = cols if cols is not None else (0, pairs[0][1].shape[1])
    tm = _pick(m, (1024, 512, 256, 128, 64, 32, 16, 8))
    tn = min(tn, n)
    assert n % tn == 0 and c0 % tn == 0
    jb = c0 // tn
    in_specs = []
    est = 0
    for a, _ in pairs:
        in_specs.append(pl.BlockSpec((tm, a.shape[1]), lambda i, j: (i, 0)))
        est += 2 * tm * a.shape[1] * a.dtype.itemsize
    for _, b in pairs:
        in_specs.append(pl.BlockSpec((b.shape[0], tn), lambda i, j: (0, jb + j)))
        est += 2 * b.shape[0] * tn * b.dtype.itemsize
    for e in row_extras:
        in_specs.append(pl.BlockSpec((tm, e.shape[1]), lambda i, j: (i, 0)))
        est += 2 * tm * e.shape[1] * 4
    out_specs = [pl.BlockSpec((tm, tn), lambda i, j: (i, j)) for _ in out_dtypes]
    out_shape = [jax.ShapeDtypeStruct((m, n), d) for d in out_dtypes]
    est += sum(2 * tm * tn * jnp.dtype(d).itemsize for d in out_dtypes) + 3 * tm * tn * 4
    kern = functools.partial(_mm_kernel, n_pairs=len(pairs), n_extra=len(row_extras), epilogue=epilogue)
    return pl.pallas_call(
        kern,
        grid=(m // tm, n // tn),
        in_specs=in_specs,
        out_specs=out_specs,
        out_shape=out_shape,
        compiler_params=_params(("parallel", "arbitrary"), est),
        name=name,
    )(*[a for a, _ in pairs], *[b for _, b in pairs], *row_extras)


def _rope_tile(acc, cos, sin_signed):
    outs = []
    for g in range(acc.shape[1] // LANES):
        xg = acc[:, g * LANES:(g + 1) * LANES]
        outs.append(xg * cos + pltpu.roll(xg, LANES // 2, 1) * sin_signed)
    return outs[0] if len(outs) == 1 else jnp.concatenate(outs, axis=1)


def _epi_q(acc, extras, outs):
    r = _rope_tile(acc, extras[0][...], extras[1][...])
    outs[0][...] = (r * (HD_DIFF ** -0.5 * math.log2(math.e))).astype(BF16)


def _epi_k(acc, extras, outs):
    r = _rope_tile(acc, extras[0][...], extras[1][...])
    outs[0][...] = r
    outs[1][...] = r.astype(BF16)


def _epi_f32_bf16(acc, extras, outs):
    outs[0][...] = acc
    outs[1][...] = acc.astype(BF16)


def _epi_f32(acc, extras, outs):
    outs[0][...] = acc


def _lambda_value(lam_ref, lam_init):
    lq1 = lam_ref[0:1, :]
    lk1 = lam_ref[1:2, :]
    lq2 = lam_ref[2:3, :]
    lk2 = lam_ref[3:4, :]
    return (jnp.exp(jnp.sum(lq1 * lk1, axis=1, keepdims=True))
            - jnp.exp(jnp.sum(lq2 * lk2, axis=1, keepdims=True)) + lam_init)


def _diff_finalize(o1, o2, lam_ref, g_ref, lam_init):
    lam = _lambda_value(lam_ref, lam_init)
    o = o1 - lam * o2
    ms = jnp.mean(o * o, axis=1, keepdims=True)
    return (o * lax.rsqrt(ms + EPS)) * g_ref[...] * (1.0 - lam_init)


def _attn_prompt_kernel(qi_ref, kj_ref, q_ref, k_ref, v_ref, lam_ref, g_ref, o_ref,
                        m1, l1, a1, m2, l2, a2, *, tq, lam_init):
    p = pl.program_id(2)
    i = qi_ref[p]
    j = kj_ref[p]
    half_t = tq // 2

    @pl.when(j == 0)
    def _():
        m1[...] = jnp.full_like(m1, NEG_INF)
        m2[...] = jnp.full_like(m2, NEG_INF)
        l1[...] = jnp.zeros_like(l1)
        l2[...] = jnp.zeros_like(l2)
        a1[...] = jnp.zeros_like(a1)
        a2[...] = jnp.zeros_like(a2)

    def update(r0, nr, nc, masked):
        if nr > ROW_BLOCK:
            for rr in range(r0, r0 + nr, ROW_BLOCK):
                update(rr, ROW_BLOCK, nc, masked)
            return
        rows = slice(r0, r0 + nr)
        q = q_ref[rows, :]
        k = k_ref[0:nc, :]
        v = v_ref[0:nc, :]
        if masked:
            qc = (r0 + lax.broadcasted_iota(jnp.int32, (nr, nc), 0)) // CHUNK
            kc = lax.broadcasted_iota(jnp.int32, (nr, nc), 1) // CHUNK
            mask = qc >= kc
        for half, (m_ref, l_ref, a_ref) in enumerate(((m1, l1, a1), (m2, l2, a2))):
            qh = q[:, half * HD_DIFF:(half + 1) * HD_DIFF]
            kh = k[:, half * HD_DIFF:(half + 1) * HD_DIFF]
            s = lax.dot_general(qh, kh, NT_DIMS, preferred_element_type=F32)
            if masked:
                s = jnp.where(mask, s, NEG_INF)
            m_old = m_ref[rows, :]
            m_new = jnp.maximum(m_old, jnp.max(s, axis=1, keepdims=True))
            alpha = jnp.exp2(m_old - m_new)
            pr = jnp.exp2(s - m_new)
            l_ref[rows, :] = alpha * l_ref[rows, :] + jnp.sum(pr, axis=1, keepdims=True)
            a_ref[rows, :] = alpha * a_ref[rows, :] + jnp.dot(pr.astype(BF16), v, preferred_element_type=F32)
            m_ref[rows, :] = m_new

    @pl.when(j < i)
    def _():
        update(0, tq, tq, False)

    @pl.when(j == i)
    def _():
        if half_t % CHUNK == 0:
            update(0, half_t, half_t, True)
            update(half_t, half_t, tq, True)
        else:
            update(0, tq, tq, True)
        o = _diff_finalize(a1[...] / l1[...], a2[...] / l2[...], lam_ref, g_ref, lam_init)
        o_ref[...] = o.astype(o_ref.dtype)


def _attn_prompt(q, k, v, lam_vecs, g, batch, seq, lam_init):
    tq = _pick(seq, (2048, 1024, 512, 256, 128, 64))
    nq = seq // tq
    hw = 2 * HD_DIFF
    pairs =[(i, j) for i in range(nq) for j in range(i + 1)]
    qi = jnp.asarray([p[0] for p in pairs], jnp.int32)
    kj = jnp.asarray([p[1] for p in pairs], jnp.int32)

    def q_map(b, h, p, qi_ref, kj_ref):
        return (b * nq + qi_ref[p], h)

    def kv_map(b, h, p, qi_ref, kj_ref):
        return (b * nq + kj_ref[p], h)

    kern = functools.partial(_attn_prompt_kernel, tq=tq, lam_init=lam_init)
    est = 2 * 4 * tq * hw * 2 + 2 * tq * hw * 4 + 4 * tq * LANES * 4 + 8 * tq * tq * 4
    return pl.pallas_call(
        kern,
        grid_spec=pltpu.PrefetchScalarGridSpec(
            num_scalar_prefetch=2,
            grid=(batch, H_DIFF, len(pairs)),
            in_specs=[
                pl.BlockSpec((tq, hw), q_map),
                pl.BlockSpec((tq, hw), kv_map),
                pl.BlockSpec((tq, hw), kv_map),
                pl.BlockSpec((4, HD_DIFF), lambda b, h, p, qi_ref, kj_ref: (0, 0)),
                pl.BlockSpec((1, hw), lambda b, h, p, qi_ref, kj_ref: (0, 0)),
            ],
            out_specs=pl.BlockSpec((tq, hw), q_map),
            scratch_shapes=[
                pltpu.VMEM((tq, 1), F32), pltpu.VMEM((tq, 1), F32), pltpu.VMEM((tq, hw), F32),
                pltpu.VMEM((tq, 1), F32), pltpu.VMEM((tq, 1), F32), pltpu.VMEM((tq, hw), F32),
            ],
        ),
        out_shape=jax.ShapeDtypeStruct((batch * seq, DIFF_W), BF16),
        compiler_params=_params(("parallel", "parallel", "arbitrary"), est),
        name="diff_attn_prompt",
    )(qi, kj, q, k, v, lam_vecs, g)


def _attn_cached_kernel(q_ref, kn_ref, vn_ref, kc_ref, vc_ref, bc_ref, bn_ref, lam_ref, g_ref, o_ref,
                        *, past, lam_init):
    ts = q_ref.shape[0]
    hw = 2 * HD_DIFF
    kc = kc_ref[0].reshape(past * H_DIFF, hw).astype(BF16)
    vc = vc_ref[0].reshape(past * H_DIFF, hw).astype(BF16)
    kn = kn_ref[0].reshape(ts * H_DIFF, hw).astype(BF16)
    vn = vn_ref[0].reshape(ts * H_DIFF, hw).astype(BF16)
    halves = []
    for half in range(2):
        sl = slice(half * HD_DIFF, (half + 1) * HD_DIFF)
        qh = jnp.concatenate([q_ref[:, h * hw + half * HD_DIFF:h * hw + (half + 1) * HD_DIFF]
                              for h in range(H_DIFF)], axis=0)
        sc = lax.dot_general(kc[:, sl], qh, NT_DIMS, preferred_element_type=F32) + bc_ref[...]
        sn = lax.dot_general(kn[:, sl], qh, NT_DIMS, preferred_element_type=F32) + bn_ref[...]
        m = jnp.maximum(jnp.max(sc, axis=0, keepdims=True), jnp.max(sn, axis=0, keepdims=True))
        pc = jnp.exp2(sc - m)
        pn = jnp.exp2(sn - m)
        l = jnp.sum(pc, axis=0, keepdims=True) + jnp.sum(pn, axis=0, keepdims=True)
        acc = (jnp.dot(pc.T.astype(BF16), vc, preferred_element_type=F32)
               + jnp.dot(pn.T.astype(BF16), vn, preferred_element_type=F32))
        l_col = jnp.broadcast_to(l, (l.shape[1], l.shape[1])).T
        halves.append(acc / jnp.concatenate([l_col] * (hw // l.shape[1]), axis=1))
    for h in range(H_DIFF):
        rows = slice(h * ts, (h + 1) * ts)
        o = _diff_finalize(halves[0][rows], halves[1][rows], lam_ref, g_ref, lam_init)
        o_ref[:, h * hw:(h + 1) * hw] = o.astype(o_ref.dtype)


def _attn_cached(q, kn, vn, cache_k, cache_v, lam_vecs, g, batch, seq, lam_init):
    past = cache_k.shape[1]
    hw = 2 * HD_DIFF
    assert H_DIFF * seq == LANES
    col_head = jnp.arange(LANES) // seq
    col_chunk = (past + jnp.arange(LANES) % seq) // CHUNK

    def bias(key_pos):
        row_head = jnp.tile(jnp.arange(H_DIFF), key_pos.shape[0])
        row_chunk = jnp.repeat(key_pos, H_DIFF) // CHUNK
        ok = (row_head[:, None] == col_head[None, :]) & (col_chunk[None, :] >= row_chunk[:, None])
        return jnp.where(ok, 0.0, NEG_INF).astype(F32)

    bias_c = bias(jnp.arange(past))
    bias_n = bias(past + jnp.arange(seq))
    kern = functools.partial(_attn_cached_kernel, past=past, lam_init=lam_init)
    est = 2 * past * DIFF_W * 4 + 2 * past * DIFF_W * 2 + 8 * past * H_DIFF * LANES * 4
    once = pl.Buffered(1)
    cache_spec = pl.BlockSpec((1, past, H_DIFF, hw), lambda b: (b, 0, 0, 0), pipeline_mode=once)
    new_spec = pl.BlockSpec((1, seq, H_DIFF, hw), lambda b: (b, 0, 0, 0))
    row_spec = pl.BlockSpec((seq, DIFF_W), lambda b: (b, 0))
    return pl.pallas_call(
        kern,
        grid=(batch,),
        in_specs=[
            row_spec, new_spec, new_spec, cache_spec, cache_spec,
            pl.BlockSpec(bias_c.shape, lambda b: (0, 0), pipeline_mode=once),
            pl.BlockSpec(bias_n.shape, lambda b: (0, 0)),
            pl.BlockSpec((4, HD_DIFF), lambda b: (0, 0)),
            pl.BlockSpec((1, hw), lambda b: (0, 0)),
        ],
        out_specs=row_spec,
        out_shape=jax.ShapeDtypeStruct((batch * seq, DIFF_W), BF16),
        compiler_params=_params(("parallel",), est),
        name="diff_attn_cached",
    )(q, kn, vn, cache_k, cache_v, bias_c, bias_n, lam_vecs, g)


def _split3(x):
    hi = x.astype(BF16)
    r1 = x - hi.astype(F32)
    mid = r1.astype(BF16)
    lo = (r1 - mid.astype(F32)).astype(BF16)
    return hi, mid, lo


def _gla_kernel(*refs, chunk, n_chunks, has_s0):
    if has_s0:
        q_ref, k_ref, v_ref, r_ref, z_ref, w2_ref, bg_ref, gn_ref, s0_ref, o_ref, s_ref, st = refs
    else:
        q_ref, k_ref, v_ref, r_ref, z_ref, w2_ref, bg_ref, gn_ref, o_ref, s_ref, st = refs
    i = pl.program_id(2)

    @pl.when(i == 0)
    def _():
        if has_s0:
            st[...] = s0_ref[0, 0].T
        else:
            st[...] = jnp.zeros_like(st)

    scale = GLA_DK ** -0.5
    zz = jnp.dot(z_ref[...].astype(BF16), w2_ref[...], preferred_element_type=F32) + bg_ref[...]
    logf = jax.nn.log_sigmoid(zz) / GLA_TAU
    row = lax.broadcasted_iota(jnp.int32, (chunk, chunk), 0)
    col = lax.broadcasted_iota(jnp.int32, (chunk, chunk), 1)
    causal = row >= col
    tri = jnp.where(causal, 1.0, 0.0).astype(BF16)
    for c in range(n_chunks):
        rows = slice(c * chunk, (c + 1) * chunk)
        g = logf[rows]
        bcum = sum(jnp.dot(tri, part, preferred_element_type=F32) for part in _split3(g))
        b_last = bcum[chunk - 1:chunk, :]
        qc = q_ref[rows, :]
        kc = k_ref[rows, :]
        vc = v_ref[rows, :].astype(BF16)
        q_in = (qc * jnp.exp(bcum) * scale).astype(BF16)
        k_in = (kc * jnp.exp(-bcum)).astype(BF16)
        k_end = (kc * jnp.exp(b_last - bcum)).astype(BF16)
        a = lax.dot_general(q_in, k_in, NT_DIMS, preferred_element_type=F32)
        a = jnp.where(causal, a, 0.0)
        s_old = st[...]
        o = (jnp.dot(a.astype(BF16), vc, preferred_element_type=F32)
             + lax.dot_general(q_in, s_old.astype(BF16), NT_DIMS, preferred_element_type=F32))
        st[...] = s_old * jnp.exp(b_last) + lax.dot_general(vc, k_end, TN_DIMS, preferred_element_type=F32)
        ms = jnp.mean(o * o, axis=1, keepdims=True)
        on = (o * lax.rsqrt(ms + EPS)) * gn_ref[...]
        gate = r_ref[rows, :]
        o_ref[rows, :] = (on * (gate * jax.nn.sigmoid(gate))).astype(o_ref.dtype)

    @pl.when(i == pl.num_programs(2) - 1)
    def _():
        s_ref[0, 0] = st[...].T


def _gla(gp, gz, w2, bg, gn, s0, batch, seq, chunk):
    tb = chunk * max(1, min(16, seq // chunk))
    while seq % tb:
        tb -= chunk
    nb = seq // tb
    kq = GLA_QK // GLA_DK
    kv = (2 * GLA_QK) // GLA_DV
    kr = kv + GLA_V // GLA_DV
    in_specs = [
        pl.BlockSpec((tb, GLA_DK), lambda b, h, i: (b * nb + i, h)),
        pl.BlockSpec((tb, GLA_DK), lambda b, h, i: (b * nb + i, kq + h)),
        pl.BlockSpec((tb, GLA_DV), lambda b, h, i: (b * nb + i, kv + h)),
        pl.BlockSpec((tb, GLA_DV), lambda b, h, i: (b * nb + i, kr + h)),
        pl.BlockSpec((tb, LANES), lambda b, h, i: (b * nb + i, 0)),
        pl.BlockSpec((LANES, GLA_DK), lambda b, h, i: (0, h)),
        pl.BlockSpec((1, GLA_DK), lambda b, h, i: (0, h)),
        pl.BlockSpec((1, GLA_DV), lambda b, h, i: (0, 0)),
    ]
    args = [gp, gp, gp, gp, gz, w2, bg, gn]
    if s0 is not None:
        in_specs.append(pl.BlockSpec((1, 1, GLA_DK, GLA_DV), lambda b, h, i: (b, h, 0, 0)))
        args.append(s0)
    kern = functools.partial(_gla_kernel, chunk=chunk, n_chunks=tb // chunk, has_s0=s0 is not None)
    est = 2 * tb * (2 * GLA_DK + 2 * GLA_DV + LANES) * 4 + 2 * tb * GLA_DV * 2 + 7 * GLA_DK * GLA_DV * 4
    return pl.pallas_call(
        kern,
        grid=(batch, H_GLA, nb),
        in_specs=in_specs,
        out_specs=[
            pl.BlockSpec((tb, GLA_DV), lambda b, h, i: (b * nb + i, h)),
            pl.BlockSpec((1, 1, GLA_DK, GLA_DV), lambda b, h, i: (b, h, 0, 0)),
        ],
        out_shape=[
            jax.ShapeDtypeStruct((batch * seq, GLA_V), BF16),
            jax.ShapeDtypeStruct((batch, H_GLA, GLA_DK, GLA_DV), F32),
        ],
        scratch_shapes=[pltpu.VMEM((GLA_DV, GLA_DK), F32)],
        compiler_params=_params(("parallel", "parallel", "arbitrary"), est),
        name="gla",
    )(*args)


def _ln_kernel(x_ref, y_ref, g_ref, b_ref, *outs, alpha, n_plain):
    h = alpha * x_ref[...] + y_ref[...].astype(F32)
    mu = jnp.mean(h, axis=1, keepdims=True)
    d = h - mu
    var = jnp.mean(d * d, axis=1, keepdims=True)
    r = (d * lax.rsqrt(var + EPS)) * g_ref[...] + b_ref[...]
    for o in outs[:n_plain]:
        o[...] = r.astype(o.dtype)
    for o in outs[n_plain:]:
        o[...] = r.T.astype(o.dtype)


def _res_layernorm(x, y, g, b, alpha, out_dtypes, transposed_dtypes=()):
    m, d = x.shape
    tm = _pick(m, (256, 128)) if transposed_dtypes else _pick(m, (256, 128, 64, 32, 16, 8))
    assert m % tm == 0
    sizes = [jnp.dtype(t).itemsize for t in (*out_dtypes, *transposed_dtypes)]
    est = 2 * tm * d * (x.dtype.itemsize + y.dtype.itemsize + sum(sizes)) + 5 * tm * d * 4
    return pl.pallas_call(
        functools.partial(_ln_kernel, alpha=alpha, n_plain=len(out_dtypes)),
        grid=(m // tm,),
        in_specs=[
            pl.BlockSpec((tm, d), lambda i: (i, 0)),
            pl.BlockSpec((tm, d), lambda i: (i, 0)),
            pl.BlockSpec((1, d), lambda i: (0, 0)),
            pl.BlockSpec((1, d), lambda i: (0, 0)),
        ],
        out_specs=[pl.BlockSpec((tm, d), lambda i: (i, 0)) for _ in out_dtypes]
        + [pl.BlockSpec((d, tm), lambda i: (0, i)) for _ in transposed_dtypes],
        out_shape=[jax.ShapeDtypeStruct((m, d), t) for t in out_dtypes]
        + [jax.ShapeDtypeStruct((d, m), t) for t in transposed_dtypes],
        compiler_params=_params(("parallel",), est),
        name="res_layernorm",
    )(x, y, g, b)


NOT_RANKED = float(N_KEYS - 1)


def _top_values(s, count):
    vals = []
    cur = s
    for t in range(count):
        m = jnp.max(cur, axis=0, keepdims=True)
        vals.append(m)
        if t + 1 < count:
            cur = jnp.where(cur == m, -jnp.inf, cur)
    return vals


def _peer_select_kernel(q_ref, k1_ref, k2_ref, cnt_ref, c1_ref, r2_ref, e2_ref, cand_ref):
    pairs = [(i, j) for i in range(PEER_TOPK) for j in range(PEER_TOPK) if (i + 1) * (j + 1) <= PEER_TOPK]
    for h in range(PEER_HEADS):
        base = h * 2 * PEER_HALF
        q1 = q_ref[:, base:base + PEER_HALF].astype(BF16)
        q2 = q_ref[:, base + PEER_HALF:base + 2 * PEER_HALF].astype(BF16)
        s1 = lax.dot_general(k1_ref[h], q1, NT_DIMS, preferred_element_type=F32)
        s2 = lax.dot_general(k2_ref[h], q2, NT_DIMS, preferred_element_type=F32)
        t1 = _top_values(s1, PEER_TOPK)
        t2 = _top_values(s2, PEER_TOPK)
        cand_ref[...] = jnp.full_like(cand_ref, -jnp.inf)
        for r, (i, j) in enumerate(pairs):
            cand_ref[r:r + 1, :] = t1[i] + t2[j]
        best = _top_values(cand_ref[...], PEER_TOPK)
        tau = best[PEER_TOPK - 1]
        z = sum(jnp.exp(bv - best[0]) for bv in best)
        counts = [sum(jnp.where(t1[i] + t2[j] >= tau, 1.0, 0.0) for j in range(PEER_TOPK) if (i, j) in pairs)
                  for i in range(PEER_TOPK)]
        cnt = jnp.zeros_like(s1)
        rank2 = jnp.full_like(s2, NOT_RANKED)
        for i in range(PEER_TOPK):
            cnt = jnp.where(s1 == t1[i], counts[i], cnt)
            rank2 = jnp.where(s2 == t2[i], float(i), rank2)
        cnt_ref[h] = cnt
        c1_ref[h] = jnp.exp(s1 - t1[0]) / z
        r2_ref[h] = rank2
        e2_ref[h] = jnp.exp(s2 - t2[0])


def _peer_select(q, keys1, keys2):
    m = q.shape[0]
    tm = _pick(m, (256, 128))
    assert m % tm == 0
    n_pairs = sum((i + 1) * (j + 1) <= PEER_TOPK for i in range(PEER_TOPK) for j in range(PEER_TOPK))
    cand_rows = -(-n_pairs // 8) * 8
    spec = pl.BlockSpec((PEER_HEADS, N_KEYS, tm), lambda i: (0, 0, i))
    est = 2 * tm * q.shape[1] * 4 + 2 * 4 * PEER_HEADS * N_KEYS * tm * 4 + (8 << 20)
    return pl.pallas_call(
        _peer_select_kernel,
        grid=(m // tm,),
        in_specs=[
            pl.BlockSpec((tm, q.shape[1]), lambda i: (i, 0)),
            pl.BlockSpec(keys1.shape, lambda i: (0, 0, 0)),
            pl.BlockSpec(keys2.shape, lambda i: (0, 0, 0)),
        ],
        out_specs=[spec, spec, spec, spec],
        out_shape=[jax.ShapeDtypeStruct((PEER_HEADS, N_KEYS, m), F32)] * 4,
        scratch_shapes=[pltpu.VMEM((cand_rows, tm), F32)],
        compiler_params=_params(("parallel",), est),
        name="peer_select",
    )(q, keys1, keys2)


def _peer_mix_kernel(xt_ref, u_ref, v_ref, cnt_ref, c1_ref, r2_ref, e2_ref, o_ref,
                     y_ref, coef_a, coef_b, act_ref, r2b_ref, e2b_ref, *, te, n_blocks):
    j = pl.program_id(1)
    d, tm = xt_ref.shape
    groups = te // N_KEYS
    jb = jnp.minimum(j, n_blocks - 1)
    pieces = [(g, c) for c in range(tm // LANES) for g in range(groups)]
    col_w = 2 * LANES
    n_vchunks = d // col_w

    @pl.when(j == 0)
    def _():
        y_ref[...] = jnp.zeros_like(y_ref)
        coef_b[...] = jnp.zeros_like(coef_b)
        r2b_ref[...] = r2_ref[...].astype(BF16)
        e2b_ref[...] = e2_ref[...].astype(BF16)

    def select_weights(g, c):
        a = jb * groups + g
        lanes = slice(c * LANES, (c + 1) * LANES)
        rows = (BF16_ROWS, LANES)
        cnt = [jnp.broadcast_to(cnt_ref[h, pl.ds(a, 1), :][:, lanes], rows).astype(BF16) for h in range(PEER_HEADS)]
        c1a = [jnp.broadcast_to(c1_ref[h, pl.ds(a, 1), :][:, lanes], rows).astype(BF16) for h in range(PEER_HEADS)]
        out = []
        for r in range(N_KEYS // BF16_ROWS):
            bs = slice(r * BF16_ROWS, (r + 1) * BF16_ROWS)
            acc = None
            for h in range(PEER_HEADS):
                wh = jnp.where(r2b_ref[h, bs, lanes] < cnt[h], e2b_ref[h, bs, lanes], 0.0) * c1a[h]
                acc = wh if acc is None else acc + wh
            out.append(acc)
        return jnp.concatenate(out, axis=0).astype(F32)

    def finish_piece(cur, g, c):
        rows = slice(g * N_KEYS, (g + 1) * N_KEYS)
        lanes = slice(c * LANES, (c + 1) * LANES)
        x = act_ref[rows, lanes]
        x = 0.5 * x * (1.0 + lax.erf(x * (2.0 ** -0.5)))
        cur[lanes, rows] = (x * select_weights(g, c)).T.astype(BF16)

    def step(cur, prev):
        if cur is not None:
            act_ref[...] = jnp.dot(u_ref[...], xt_ref[...], preferred_element_type=F32)
        for n in range(n_vchunks):
            cs = slice(n * col_w, (n + 1) * col_w)
            y_ref[:, cs] += jnp.dot(prev[...], v_ref[:, cs], preferred_element_type=F32)
            if cur is not None:
                for g, c in pieces[n::n_vchunks]:
                    finish_piece(cur, g, c)

    @pl.when(jnp.logical_and(j % 2 == 0, j < n_blocks))
    def _():
        step(coef_a, coef_b)

    @pl.when(jnp.logical_and(j % 2 == 1, j < n_blocks))
    def _():
        step(coef_b, coef_a)

    @pl.when(j == n_blocks)
    def _():
        step(None, coef_a if n_blocks % 2 else coef_b)
        o_ref[...] = y_ref[...].astype(o_ref.dtype)


def _peer_mix(xt, u, v, cnt, c1, r2, e2):
    d, m = xt.shape
    ne = v.shape[0]
    tm = _pick(m, (512, 256, 128))
    te = 512
    assert m % tm == 0 and tm % LANES == 0 and ne % te == 0 and te % N_KEYS == 0
    nj = ne // te
    once = pl.Buffered(1)
    sel_spec = pl.BlockSpec((PEER_HEADS, N_KEYS, tm), lambda i, j: (0, 0, i), pipeline_mode=once)
    est = (tm * d * 2 + 2 * 2 * te * d * 2 + 4 * PEER_HEADS * N_KEYS * tm * 4
           + 2 * tm * d * 4 + 2 * tm * te * 2 + 6 * te * tm * 4)
    return pl.pallas_call(
        functools.partial(_peer_mix_kernel, te=te, n_blocks=nj),
        grid=(m // tm, nj + 1),
        in_specs=[
            pl.BlockSpec((d, tm), lambda i, j: (0, i), pipeline_mode=once),
            pl.BlockSpec((te, d), lambda i, j: (jnp.minimum(j, nj - 1), 0)),
            pl.BlockSpec((te, d), lambda i, j: (jnp.maximum(j - 1, 0), 0)),
            sel_spec, sel_spec, sel_spec, sel_spec,
        ],
        out_specs=pl.BlockSpec((tm, d), lambda i, j: (i, 0)),
        out_shape=jax.ShapeDtypeStruct((m, d), BF16),
        scratch_shapes=[pltpu.VMEM((tm, d), F32), pltpu.VMEM((tm, te), BF16), pltpu.VMEM((tm, te), BF16),
                        pltpu.VMEM((te, tm), F32),
                        pltpu.VMEM((PEER_HEADS, N_KEYS, tm), BF16), pltpu.VMEM((PEER_HEADS, N_KEYS, tm), BF16)],
        compiler_params=_params(("parallel", "arbitrary"), est),
        name="peer_mix",
    )(xt, u, v, cnt, c1, r2, e2)


def _rope_tables(pos, batch):
    inv = ROPE_THETA ** (-jnp.arange(0, HD_DIFF, 2, dtype=F32) / HD_DIFF)
    ang = pos.astype(F32)[:, None] * inv[None, :]
    ang = jnp.concatenate([ang, ang], axis=-1)
    sign = jnp.where(jnp.arange(HD_DIFF) < HD_DIFF // 2, -1.0, 1.0).astype(F32)
    cos = jnp.tile(jnp.cos(ang), (batch, 1))
    sin_signed = jnp.tile(jnp.sin(ang) * sign[None, :], (batch, 1))
    return cos, sin_signed


def _prep_weights(w_in, w_gate2, b_gate, lam_q1, lam_k1, lam_q2, lam_k2, diff_norm_g, gla_norm_g, w_out,
                  ln1_g, ln1_b, ln2_g, ln2_b, peer_wq, peer_keys1, peer_keys2, peer_u, peer_v):
    wz = jnp.pad(w_in[:, IN_GATE:IN_GATE + GATE_RANK], ((0, 0), (0, LANES - GATE_RANK)))
    return dict(
        w_in=w_in.astype(BF16),
        wz=wz.astype(BF16),
        w2=jnp.pad(w_gate2, ((0, LANES - GATE_RANK), (0, 0))).astype(BF16),
        bg=b_gate.reshape(1, GLA_QK),
        lam=jnp.stack([lam_q1, lam_k1, lam_q2, lam_k2]),
        dng=diff_norm_g.reshape(1, 2 * HD_DIFF),
        gng=gla_norm_g.reshape(1, GLA_DV),
        wo_a=w_out[:DIFF_W].astype(BF16),
        wo_b=w_out[DIFF_W:].astype(BF16),
        ln1_g=ln1_g.reshape(1, -1), ln1_b=ln1_b.reshape(1, -1),
        ln2_g=ln2_g.reshape(1, -1), ln2_b=ln2_b.reshape(1, -1),
        pwq=peer_wq.astype(BF16),
        pk1=peer_keys1.astype(BF16), pk2=peer_keys2.astype(BF16),
        pu=peer_u.astype(BF16), pv=peer_v.astype(BF16),
    )


def _layer(x, pos, past_k, past_v, s0, chunk, lam_init, alpha, w):
    batch, seq, d = x.shape
    m = batch * seq
    x2d = x.reshape(m, d)
    xb = x2d.astype(BF16)
    cos, sin_signed = _rope_tables(pos, batch)

    proj = [(xb, w["w_in"])]
    (q,) = _matmul(proj, _epi_q, [BF16], (cos, sin_signed), name="proj_q", cols=(0, DIFF_W))
    k32, kb = _matmul(proj, _epi_k, [F32, BF16], (cos, sin_signed), name="proj_k", cols=(DIFF_W, DIFF_W))
    v32, vb = _matmul(proj, _epi_f32_bf16, [F32, BF16], name="proj_v", cols=(2 * DIFF_W, DIFF_W))
    (gp,) = _matmul(proj, _epi_f32, [F32], name="proj_gla", cols=(IN_GLA, IN_GATE - IN_GLA))
    (gz,) = _matmul([(xb, w["wz"])], _epi_f32, [F32], tn=LANES, name="proj_gate")

    new_k = k32.reshape(batch, seq, H_DIFF, 2 * HD_DIFF)
    new_v = v32.reshape(batch, seq, H_DIFF, 2 * HD_DIFF)
    if past_k is None:
        o_diff = _attn_prompt(q, kb, vb, w["lam"], w["dng"], batch, seq, lam_init)
    else:
        o_diff = _attn_cached(q, new_k, new_v, past_k, past_v, w["lam"], w["dng"], batch, seq, lam_init)
    o_gla, s_new = _gla(gp, gz, w["w2"], w["bg"], w["gng"], s0, batch, seq, chunk)

    (mix,) = _matmul([(o_diff, w["wo_a"]), (o_gla, w["wo_b"])], _epi_f32, [F32], tn=1024, name="out_proj")
    x1, x1b, x1t = _res_layernorm(x2d, mix, w["ln1_g"], w["ln1_b"], alpha, [F32, BF16], [BF16])

    (pq,) = _matmul([(x1b, w["pwq"])], _epi_f32, [F32], name="peer_query")
    cnt, c1, r2, e2 = _peer_select(pq, w["pk1"], w["pk2"])
    y = _peer_mix(x1t, w["pu"], w["pv"], cnt, c1, r2, e2)
    (x2,) = _res_layernorm(x1, y, w["ln2_g"], w["ln2_b"], alpha, [F32])
    return x2.reshape(batch, seq, d), new_k, new_v, s_new


def kernel(x_prompt, x_sample, cache_diff_k, cache_diff_v, state_gla, w_in, w_gate2, b_gate, lam_q1, lam_k1, lam_q2, lam_k2, diff_norm_g, gla_norm_g, w_out, ln1_g, ln1_b, ln2_g, ln2_b, peer_wq, peer_keys1, peer_keys2, peer_u, peer_v):
    depth = w_in.shape[0]
    tp = x_prompt.shape[1]
    ts = x_sample.shape[1]
    past = cache_diff_k.shape[2]
    alpha = (2 * depth) ** 0.25
    pos_p = jnp.arange(tp, dtype=jnp.int32)
    pos_s = past + jnp.arange(ts, dtype=jnp.int32)
    y_p, y_s = x_prompt, x_sample
    outs = [[] for _ in range(6)]
    for l in range(depth):
        lam_init = 0.8 - 0.6 * math.exp(-0.3 * l)
        w = _prep_weights(w_in[l], w_gate2[l], b_gate[l], lam_q1[l], lam_k1[l], lam_q2[l], lam_k2[l],
                          diff_norm_g[l], gla_norm_g[l], w_out[l], ln1_g[l], ln1_b[l], ln2_g[l], ln2_b[l],
                          peer_wq[l], peer_keys1[l], peer_keys2[l], peer_u[l], peer_v[l])
        y_p, kp, vp, sp = _layer(y_p, pos_p, None, None, None, CHUNK, lam_init, alpha, w)
        y_s, k_s, v_s, s_s = _layer(y_s, pos_s, cache_diff_k[l], cache_diff_v[l], state_gla[l],
                                    ts, lam_init, alpha, w)
        for lst, val in zip(outs, (kp, vp, sp, k_s, v_s, s_s)):
            lst.append(val)
    return (y_p, y_s) + tuple(jnp.stack(o) for o in outs)
```

```python
import functools
import math

import jax
import jax.numpy as jnp
from jax import lax
from jax.experimental import pallas as pl
from jax.experimental.pallas import tpu as pltpu

CHUNK = 64
HD_DIFF = 128
H_DIFF = 8
H_GLA = 4
GLA_DK = 256
GLA_DV = 512
GATE_RANK = 16
GLA_TAU = 16.0
ROPE_THETA = 10000.0
N_KEYS = 128
PEER_HEADS = 8
PEER_HALF = 128
PEER_TOPK = 16
EPS = 1e-5
NEG_INF = -1e30
ROW_BLOCK = 256

DIFF_W = H_DIFF * 2 * HD_DIFF
GLA_QK = H_GLA * GLA_DK
GLA_V = H_GLA * GLA_DV
IN_GLA = 3 * DIFF_W
IN_GATE = IN_GLA + 2 * GLA_QK + 2 * GLA_V

LANES = 128
SUBLANES = 8
BF16_ROWS = 2 * SUBLANES
VMEM_CAP = 60000 * 1024

BF16 = jnp.bfloat16
F32 = jnp.float32

NT_DIMS = (((1,), (1,)), ((), ()))
TN_DIMS = (((0,), (0,)), ((), ()))


def _pick(n, cands):
    for c in cands:
        if n % c == 0:
            return c
    return n


def _params(sem, est_bytes):
    limit = int(min(VMEM_CAP, max(32 * 1024 * 1024, est_bytes * 5 // 4)))
    return pltpu.CompilerParams(dimension_semantics=sem, vmem_limit_bytes=limit)


def _mm_kernel(*refs, n_pairs, n_extra, epilogue):
    a_refs = refs[:n_pairs]
    b_refs = refs[n_pairs:2 * n_pairs]
    extras = refs[2 * n_pairs:2 * n_pairs + n_extra]
    outs = refs[2 * n_pairs + n_extra:]
    acc = jnp.dot(a_refs[0][...], b_refs[0][...], preferred_element_type=F32)
    for a, b in zip(a_refs[1:], b_refs[1:]):
        acc = acc + jnp.dot(a[...], b[...], preferred_element_type=F32)
    epilogue(acc, extras, outs)


def _matmul(pairs, epilogue, out_dtypes, row_extras=(), tn=512, name="matmul", cols=None):
    m = pairs[0][0].shape[0]
    c0, n = cols if cols is not None else (0, pairs[0][1].shape[1])
    tm = _pick(m, (1024, 512, 256, 128, 64, 32, 16, 8))
    tn = min(tn, n)
    assert n % tn == 0 and c0 % tn == 0
    jb = c0 // tn
    in_specs = []
    est = 0
    for a, _ in pairs:
        in_specs.append(pl.BlockSpec((tm, a.shape[1]), lambda i, j: (i, 0)))
        est += 2 * tm * a.shape[1] * a.dtype.itemsize
    for _, b in pairs:
        in_specs.append(pl.BlockSpec((b.shape[0], tn), lambda i, j: (0, jb + j)))
        est += 2 * b.shape[0] * tn * b.dtype.itemsize
    for e in row_extras:
        in_specs.append(pl.BlockSpec((tm, e.shape[1]), lambda i, j: (i, 0)))
        est += 2 * tm * e.shape[1] * 4
    out_specs = [pl.BlockSpec((tm, tn), lambda i, j: (i, j)) for _ in out_dtypes]
    out_shape = [jax.ShapeDtypeStruct((m, n), d) for d in out_dtypes]
    est += sum(2 * tm * tn * jnp.dtype(d).itemsize for d in out_dtypes) + 3 * tm * tn * 4
    kern = functools.partial(_mm_kernel, n_pairs=len(pairs), n_extra=len(row_extras), epilogue=epilogue)
    return pl.pallas_call(
        kern,
        grid=(m // tm, n // tn),
        in_specs=in_specs,
        out_specs=out_specs,
        out_shape=out_shape,
        compiler_params=_params(("parallel", "arbitrary"), est),
        name=name,
    )(*[a for a, _ in pairs], *[b for _, b in pairs], *row_extras)


def _rope_tile(acc, cos, sin_signed):
    outs = []
    for g in range(acc.shape[1] // LANES):
        xg = acc[:, g * LANES:(g + 1) * LANES]
        outs.append(xg * cos + pltpu.roll(xg, LANES // 2, 1) * sin_signed)
    return outs[0] if len(outs) == 1 else jnp.concatenate(outs, axis=1)


def _epi_q(acc, extras, outs):
    r = _rope_tile(acc, extras[0][...], extras[1][...])
    outs[0][...] = (r * (HD_DIFF ** -0.5 * math.log2(math.e))).astype(BF16)


def _epi_k(acc, extras, outs):
    r = _rope_tile(acc, extras[0][...], extras[1][...])
    outs[0][...] = r
    outs[1][...] = r.astype(BF16)


def _epi_f32_bf16(acc, extras, outs):
    outs[0][...] = acc
    outs[1][...] = acc.astype(BF16)


def _epi_f32(acc, extras, outs):
    outs[0][...] = acc


def _lambda_value(lam_ref, lam_init):
    lq1 = lam_ref[0:1, :]
    lk1 = lam_ref[1:2, :]
    lq2 = lam_ref[2:3, :]
    lk2 = lam_ref[3:4, :]
    return (jnp.exp(jnp.sum(lq1 * lk1, axis=1, keepdims=True))
            - jnp.exp(jnp.sum(lq2 * lk2, axis=1, keepdims=True)) + lam_init)


def _diff_finalize(o1, o2, lam_ref, g_ref, lam_init):
    lam = _lambda_value(lam_ref, lam_init)
    o = o1 - lam * o2
    ms = jnp.mean(o * o, axis=1, keepdims=True)
    return (o * lax.rsqrt(ms + EPS)) * g_ref[...] * (1.0 - lam_init)


def _attn_prompt_kernel(qi_ref, kj_ref, q_ref, k_ref, v_ref, lam_ref, g_ref, o_ref,
                        m1, l1, a1, m2, l2, a2, *, tq, lam_init):
    p = pl.program_id(2)
    i = qi_ref[p]
    j = kj_ref[p]
    half_t = tq // 2

    @pl.when(j == 0)
    def _():
        m1[...] = jnp.full_like(m1, NEG_INF)
        m2[...] = jnp.full_like(m2, NEG_INF)
        l1[...] = jnp.zeros_like(l1)
        l2[...] = jnp.zeros_like(l2)
        a1[...] = jnp.zeros_like(a1)
        a2[...] = jnp.zeros_like(a2)

    def update(r0, nr, nc, masked):
        if nr > ROW_BLOCK:
            for rr in range(r0, r0 + nr, ROW_BLOCK):
                update(rr, ROW_BLOCK, nc, masked)
            return
        rows = slice(r0, r0 + nr)
        q = q_ref[rows, :]
        k = k_ref[0:nc, :]
        v = v_ref[0:nc, :]
        if masked:
            qc = (r0 + lax.broadcasted_iota(jnp.int32, (nr, nc), 0)) // CHUNK
            kc = lax.broadcasted_iota(jnp.int32, (nr, nc), 1) // CHUNK
            mask = qc >= kc
        for half, (m_ref, l_ref, a_ref) in enumerate(((m1, l1, a1), (m2, l2, a2))):
            qh = q[:, half * HD_DIFF:(half + 1) * HD_DIFF]
            kh = k[:, half * HD_DIFF:(half + 1) * HD_DIFF]
            s = lax.dot_general(qh, kh, NT_DIMS, preferred_element_type=F32)
            if masked:
                s = jnp.where(mask, s, NEG_INF)
            m_old = m_ref[rows, :]
            m_new = jnp.maximum(m_old, jnp.max(s, axis=1, keepdims=True))
            alpha = jnp.exp2(m_old - m_new)
            pr = jnp.exp2(s - m_new)
            l_ref[rows, :] = alpha * l_ref[rows, :] + jnp.sum(pr, axis=1, keepdims=True)
            a_ref[rows, :] = alpha * a_ref[rows, :] + jnp.dot(pr.astype(BF16), v, preferred_element_type=F32)
            m_ref[rows, :] = m_new

    @pl.when(j < i)
    def _():
        update(0, tq, tq, False)

    @pl.when(j == i)
    def _():
        if half_t % CHUNK == 0:
            update(0, half_t, half_t, True)
            update(half_t, half_t, tq, True)
        else:
            update(0, tq, tq, True)
        o = _diff_finalize(a1[...] / l1[...], a2[...] / l2[...], lam_ref, g_ref, lam_init)
        o_ref[...] = o.astype(o_ref.dtype)


def _attn_prompt(q, k, v, lam_vecs, g, batch, seq, lam_init):
    tq = _pick(seq, (2048, 1024, 512, 256, 128, 64))
    nq = seq // tq
    hw = 2 * HD_DIFF
    pairs =[(i, j) for i in range(nq) for j in range(i + 1)]
    qi = jnp.asarray([p[0] for p in pairs], jnp.int32)
    kj = jnp.asarray([p[1] for p in pairs], jnp.int32)

    def q_map(b, h, p, qi_ref, kj_ref):
        return (b * nq + qi_ref[p], h)

    def kv_map(b, h, p, qi_ref, kj_ref):
        return (b * nq + kj_ref[p], h)

    kern = functools.partial(_attn_prompt_kernel, tq=tq, lam_init=lam_init)
    est = 2 * 4 * tq * hw * 2 + 2 * tq * hw * 4 + 4 * tq * LANES * 4 + 8 * tq * tq * 4
    return pl.pallas_call(
        kern,
        grid_spec=pltpu.PrefetchScalarGridSpec(
            num_scalar_prefetch=2,
            grid=(batch, H_DIFF, len(pairs)),
            in_specs=[
                pl.BlockSpec((tq, hw), q_map),
                pl.BlockSpec((tq, hw), kv_map),
                pl.BlockSpec((tq, hw), kv_map),
                pl.BlockSpec((4, HD_DIFF), lambda b, h, p, qi_ref, kj_ref: (0, 0)),
                pl.BlockSpec((1, hw), lambda b, h, p, qi_ref, kj_ref: (0, 0)),
            ],
            out_specs=pl.BlockSpec((tq, hw), q_map),
            scratch_shapes=[
                pltpu.VMEM((tq, 1), F32), pltpu.VMEM((tq, 1), F32), pltpu.VMEM((tq, hw), F32),
                pltpu.VMEM((tq, 1), F32), pltpu.VMEM((tq, 1), F32), pltpu.VMEM((tq, hw), F32),
            ],
        ),
        out_shape=jax.ShapeDtypeStruct((batch * seq, DIFF_W), BF16),
        compiler_params=_params(("parallel", "parallel", "arbitrary"), est),
        name="diff_attn_prompt",
    )(qi, kj, q, k, v, lam_vecs, g)


def _attn_cached_kernel(q_ref, kn_ref, vn_ref, kc_ref, vc_ref, bc_ref, bn_ref, lam_ref, g_ref, o_ref,
                        *, past, lam_init):
    ts = q_ref.shape[0]
    hw = 2 * HD_DIFF
    kc = kc_ref[0].reshape(past * H_DIFF, hw).astype(BF16)
    vc = vc_ref[0].reshape(past * H_DIFF, hw).astype(BF16)
    kn = kn_ref[0].reshape(ts * H_DIFF, hw).astype(BF16)
    vn = vn_ref[0].reshape(ts * H_DIFF, hw).astype(BF16)
    halves = []
    for half in range(2):
        sl = slice(half * HD_DIFF, (half + 1) * HD_DIFF)
        qh = jnp.concatenate([q_ref[:, h * hw + half * HD_DIFF:h * hw + (half + 1) * HD_DIFF]
                              for h in range(H_DIFF)], axis=0)
        sc = lax.dot_general(kc[:, sl], qh, NT_DIMS, preferred_element_type=F32) + bc_ref[...]
        sn = lax.dot_general(kn[:, sl], qh, NT_DIMS, preferred_element_type=F32) + bn_ref[...]
        m = jnp.maximum(jnp.max(sc, axis=0, keepdims=True), jnp.max(sn, axis=0, keepdims=True))
        pc = jnp.exp2(sc - m)
        pn = jnp.exp2(sn - m)
        l = jnp.sum(pc, axis=0, keepdims=True) + jnp.sum(pn, axis=0, keepdims=True)
        acc = (jnp.dot(pc.T.astype(BF16), vc, preferred_element_type=F32)
               + jnp.dot(pn.T.astype(BF16), vn, preferred_element_type=F32))
        l_col = jnp.broadcast_to(l, (l.shape[1], l.shape[1])).T
        halves.append(acc / jnp.concatenate([l_col] * (hw // l.shape[1]), axis=1))
    for h in range(H_DIFF):
        rows = slice(h * ts, (h + 1) * ts)
        o = _diff_finalize(halves[0][rows], halves[1][rows], lam_ref, g_ref, lam_init)
        o_ref[:, h * hw:(h + 1) * hw] = o.astype(o_ref.dtype)


def _attn_cached(q, kn, vn, cache_k, cache_v, lam_vecs, g, batch, seq, lam_init):
    past = cache_k.shape[1]
    hw = 2 * HD_DIFF
    assert H_DIFF * seq == LANES
    col_head = jnp.arange(LANES) // seq
    col_chunk = (past + jnp.arange(LANES) % seq) // CHUNK

    def bias(key_pos):
        row_head = jnp.tile(jnp.arange(H_DIFF), key_pos.shape[0])
        row_chunk = jnp.repeat(key_pos, H_DIFF) // CHUNK
        ok = (row_head[:, None] == col_head[None, :]) & (col_chunk[None, :] >= row_chunk[:, None])
        return jnp.where(ok, 0.0, NEG_INF).astype(F32)

    bias_c = bias(jnp.arange(past))
    bias_n = bias(past + jnp.arange(seq))
    kern = functools.partial(_attn_cached_kernel, past=past, lam_init=lam_init)
    est = 2 * past * DIFF_W * 4 + 2 * past * DIFF_W * 2 + 8 * past * H_DIFF * LANES * 4
    once = pl.Buffered(1)
    cache_spec = pl.BlockSpec((1, past, H_DIFF, hw), lambda b: (b, 0, 0, 0), pipeline_mode=once)
    new_spec = pl.BlockSpec((1, seq, H_DIFF, hw), lambda b: (b, 0, 0, 0))
    row_spec = pl.BlockSpec((seq, DIFF_W), lambda b: (b, 0))
    return pl.pallas_call(
        kern,
        grid=(batch,),
        in_specs=[
            row_spec, new_spec, new_spec, cache_spec, cache_spec,
            pl.BlockSpec(bias_c.shape, lambda b: (0, 0), pipeline_mode=once),
            pl.BlockSpec(bias_n.shape, lambda b: (0, 0)),
            pl.BlockSpec((4, HD_DIFF), lambda b: (0, 0)),
            pl.BlockSpec((1, hw), lambda b: (0, 0)),
        ],
        out_specs=row_spec,
        out_shape=jax.ShapeDtypeStruct((batch * seq, DIFF_W), BF16),
        compiler_params=_params(("parallel",), est),
        name="diff_attn_cached",
    )(q, kn, vn, cache_k, cache_v, bias_c, bias_n, lam_vecs, g)


def _split3(x):
    hi = x.astype(BF16)
    r1 = x - hi.astype(F32)
    mid = r1.astype(BF16)
    lo = (r1 - mid.astype(F32)).astype(BF16)
    return hi, mid, lo


def _gla_kernel(*refs, chunk, n_chunks, has_s0):
    if has_s0:
        q_ref, k_ref, v_ref, r_ref, z_ref, w2_ref, bg_ref, gn_ref, s0_ref, o_ref, s_ref, st = refs
    else:
        q_ref, k_ref, v_ref, r_ref, z_ref, w2_ref, bg_ref, gn_ref, o_ref, s_ref, st = refs
    i = pl.program_id(2)

    @pl.when(i == 0)
    def _():
        if has_s0:
            st[...] = s0_ref[0, 0].T
        else:
            st[...] = jnp.zeros_like(st)

    scale = GLA_DK ** -0.5
    zz = jnp.dot(z_ref[...].astype(BF16), w2_ref[...], preferred_element_type=F32) + bg_ref[...]
    logf = jax.nn.log_sigmoid(zz) / GLA_TAU
    row = lax.broadcasted_iota(jnp.int32, (chunk, chunk), 0)
    col = lax.broadcasted_iota(jnp.int32, (chunk, chunk), 1)
    causal = row >= col
    tri = jnp.where(causal, 1.0, 0.0).astype(BF16)
    for c in range(n_chunks):
        rows = slice(c * chunk, (c + 1) * chunk)
        g = logf[rows]
        bcum = sum(jnp.dot(tri, part, preferred_element_type=F32) for part in _split3(g))
        b_last = bcum[chunk - 1:chunk, :]
        qc = q_ref[rows, :]
        kc = k_ref[rows, :]
        vc = v_ref[rows, :].astype(BF16)
        q_in = (qc * jnp.exp(bcum) * scale).astype(BF16)
        k_in = (kc * jnp.exp(-bcum)).astype(BF16)
        k_end = (kc * jnp.exp(b_last - bcum)).astype(BF16)
        a = lax.dot_general(q_in, k_in, NT_DIMS, preferred_element_type=F32)
        a = jnp.where(causal, a, 0.0)
        s_old = st[...]
        o = (jnp.dot(a.astype(BF16), vc, preferred_element_type=F32)
             + lax.dot_general(q_in, s_old.astype(BF16), NT_DIMS, preferred_element_type=F32))
        st[...] = s_old * jnp.exp(b_last) + lax.dot_general(vc, k_end, TN_DIMS, preferred_element_type=F32)
        ms = jnp.mean(o * o, axis=1, keepdims=True)
        on = (o * lax.rsqrt(ms + EPS)) * gn_ref[...]
        gate = r_ref[rows, :]
        o_ref[rows, :] = (on * (gate * jax.nn.sigmoid(gate))).astype(o_ref.dtype)

    @pl.when(i == pl.num_programs(2) - 1)
    def _():
        s_ref[0, 0] = st[...].T


def _gla(gp, gz, w2, bg, gn, s0, batch, seq, chunk):
    tb = chunk * max(1, min(16, seq // chunk))
    while seq % tb:
        tb -= chunk
    nb = seq // tb
    kq = GLA_QK // GLA_DK
    kv = (2 * GLA_QK) // GLA_DV
    kr = kv + GLA_V // GLA_DV
    in_specs = [
        pl.BlockSpec((tb, GLA_DK), lambda b, h, i: (b * nb + i, h)),
        pl.BlockSpec((tb, GLA_DK), lambda b, h, i: (b * nb + i, kq + h)),
        pl.BlockSpec((tb, GLA_DV), lambda b, h, i: (b * nb + i, kv + h)),
        pl.BlockSpec((tb, GLA_DV), lambda b, h, i: (b * nb + i, kr + h)),
        pl.BlockSpec((tb, LANES), lambda b, h, i: (b * nb + i, 0)),
        pl.BlockSpec((LANES, GLA_DK), lambda b, h, i: (0, h)),
        pl.BlockSpec((1, GLA_DK), lambda b, h, i: (0, h)),
        pl.BlockSpec((1, GLA_DV), lambda b, h, i: (0, 0)),
    ]
    args = [gp, gp, gp, gp, gz, w2, bg, gn]
    if s0 is not None:
        in_specs.append(pl.BlockSpec((1, 1, GLA_DK, GLA_DV), lambda b, h, i: (b, h, 0, 0)))
        args.append(s0)
    kern = functools.partial(_gla_kernel, chunk=chunk, n_chunks=tb // chunk, has_s0=s0 is not None)
    est = 2 * tb * (2 * GLA_DK + 2 * GLA_DV + LANES) * 4 + 2 * tb * GLA_DV * 2 + 7 * GLA_DK * GLA_DV * 4
    return pl.pallas_call(
        kern,
        grid=(batch, H_GLA, nb),
        in_specs=in_specs,
        out_specs=[
            pl.BlockSpec((tb, GLA_DV), lambda b, h, i: (b * nb + i, h)),
            pl.BlockSpec((1, 1, GLA_DK, GLA_DV), lambda b, h, i: (b, h, 0, 0)),
        ],
        out_shape=[
            jax.ShapeDtypeStruct((batch * seq, GLA_V), BF16),
            jax.ShapeDtypeStruct((batch, H_GLA, GLA_DK, GLA_DV), F32),
        ],
        scratch_shapes=[pltpu.VMEM((GLA_DV, GLA_DK), F32)],
        compiler_params=_params(("parallel", "parallel", "arbitrary"), est),
        name="gla",
    )(*args)


def _ln_kernel(x_ref, y_ref, g_ref, b_ref, *outs, alpha, n_plain):
    h = alpha * x_ref[...] + y_ref[...].astype(F32)
    mu = jnp.mean(h, axis=1, keepdims=True)
    d = h - mu
    var = jnp.mean(d * d, axis=1, keepdims=True)
    r = (d * lax.rsqrt(var + EPS)) * g_ref[...] + b_ref[...]
    for o in outs[:n_plain]:
        o[...] = r.astype(o.dtype)
    for o in outs[n_plain:]:
        o[...] = r.T.astype(o.dtype)


def _res_layernorm(x, y, g, b, alpha, out_dtypes, transposed_dtypes=()):
    m, d = x.shape
    tm = _pick(m, (256, 128)) if transposed_dtypes else _pick(m, (256, 128, 64, 32, 16, 8))
    assert m % tm == 0
    sizes = [jnp.dtype(t).itemsize for t in (*out_dtypes, *transposed_dtypes)]
    est = 2 * tm * d * (x.dtype.itemsize + y.dtype.itemsize + sum(sizes)) + 5 * tm * d * 4
    return pl.pallas_call(
        functools.partial(_ln_kernel, alpha=alpha, n_plain=len(out_dtypes)),
        grid=(m // tm,),
        in_specs=[
            pl.BlockSpec((tm, d), lambda i: (i, 0)),
            pl.BlockSpec((tm, d), lambda i: (i, 0)),
            pl.BlockSpec((1, d), lambda i: (0, 0)),
            pl.BlockSpec((1, d), lambda i: (0, 0)),
        ],
        out_specs=[pl.BlockSpec((tm, d), lambda i: (i, 0)) for _ in out_dtypes]
        + [pl.BlockSpec((d, tm), lambda i: (0, i)) for _ in transposed_dtypes],
        out_shape=[jax.ShapeDtypeStruct((m, d), t) for t in out_dtypes]
        + [jax.ShapeDtypeStruct((d, m), t) for t in transposed_dtypes],
        compiler_params=_params(("parallel",), est),
        name="res_layernorm",
    )(x, y, g, b)


NOT_RANKED = float(N_KEYS - 1)


def _top_values(s, count):
    vals = []
    cur = s
    for t in range(count):
        m = jnp.max(cur, axis=0, keepdims=True)
        vals.append(m)
        if t + 1 < count:
            cur = jnp.where(cur == m, -jnp.inf, cur)
    return vals


def _peer_select_kernel(q_ref, k1_ref, k2_ref, cnt_ref, c1_ref, r2_ref, e2_ref, cand_ref):
    pairs = [(i, j) for i in range(PEER_TOPK) for j in range(PEER_TOPK) if (i + 1) * (j + 1) <= PEER_TOPK]
    for h in range(PEER_HEADS):
        base = h * 2 * PEER_HALF
        q1 = q_ref[:, base:base + PEER_HALF].astype(BF16)
        q2 = q_ref[:, base + PEER_HALF:base + 2 * PEER_HALF].astype(BF16)
        s1 = lax.dot_general(k1_ref[h], q1, NT_DIMS, preferred_element_type=F32)
        s2 = lax.dot_general(k2_ref[h], q2, NT_DIMS, preferred_element_type=F32)
        t1 = _top_values(s1, PEER_TOPK)
        t2 = _top_values(s2, PEER_TOPK)
        cand_ref[...] = jnp.full_like(cand_ref, -jnp.inf)
        for r, (i, j) in enumerate(pairs):
            cand_ref[r:r + 1, :] = t1[i] + t2[j]
        best = _top_values(cand_ref[...], PEER_TOPK)
        tau = best[PEER_TOPK - 1]
        z = sum(jnp.exp(bv - best[0]) for bv in best)
        counts = [sum(jnp.where(t1[i] + t2[j] >= tau, 1.0, 0.0) for j in range(PEER_TOPK) if (i, j) in pairs)
                  for i in range(PEER_TOPK)]
        cnt = jnp.zeros_like(s1)
        rank2 = jnp.full_like(s2, NOT_RANKED)
        for i in range(PEER_TOPK):
            cnt = jnp.where(s1 == t1[i], counts[i], cnt)
            rank2 = jnp.where(s2 == t2[i], float(i), rank2)
        cnt_ref[h] = cnt
        c1_ref[h] = jnp.exp(s1 - t1[0]) / z
        r2_ref[h] = rank2
        e2_ref[h] = jnp.exp(s2 - t2[0])


def _peer_select(q, keys1, keys2):
    m = q.shape[0]
    tm = _pick(m, (256, 128))
    assert m % tm == 0
    n_pairs = sum((i + 1) * (j + 1) <= PEER_TOPK for i in range(PEER_TOPK) for j in range(PEER_TOPK))
    cand_rows = -(-n_pairs // 8) * 8
    spec = pl.BlockSpec((PEER_HEADS, N_KEYS, tm), lambda i: (0, 0, i))
    est = 2 * tm * q.shape[1] * 4 + 2 * 4 * PEER_HEADS * N_KEYS * tm * 4 + (8 << 20)
    return pl.pallas_call(
        _peer_select_kernel,
        grid=(m // tm,),
        in_specs=[
            pl.BlockSpec((tm, q.shape[1]), lambda i: (i, 0)),
            pl.BlockSpec(keys1.shape, lambda i: (0, 0, 0)),
            pl.BlockSpec(keys2.shape, lambda i: (0, 0, 0)),
        ],
        out_specs=[spec, spec, spec, spec],
        out_shape=[jax.ShapeDtypeStruct((PEER_HEADS, N_KEYS, m), F32)] * 4,
        scratch_shapes=[pltpu.VMEM((cand_rows, tm), F32)],
        compiler_params=_params(("parallel",), est),
        name="peer_select",
    )(q, keys1, keys2)


def _peer_mix_kernel(xt_ref, u_ref, v_ref, cnt_ref, c1_ref, r2_ref, e2_ref, o_ref,
                     y_ref, coef_a, coef_b, act_ref, r2b_ref, e2b_ref, *, te, n_blocks):
    j = pl.program_id(1)
    d, tm = xt_ref.shape
    groups = te // N_KEYS
    jb = jnp.minimum(j, n_blocks - 1)
    pieces = [(g, c) for c in range(tm // LANES) for g in range(groups)]
    col_w = 2 * LANES
    n_vchunks = d // col_w

    @pl.when(j == 0)
    def _():
        y_ref[...] = jnp.zeros_like(y_ref)
        coef_b[...] = jnp.zeros_like(coef_b)
        r2b_ref[...] = r2_ref[...].astype(BF16)
        e2b_ref[...] = e2_ref[...].astype(BF16)

    def select_weights(g, c):
        a = jb * groups + g
        lanes = slice(c * LANES, (c + 1) * LANES)
        rows = (BF16_ROWS, LANES)
        cnt = [jnp.broadcast_to(cnt_ref[h, pl.ds(a, 1), :][:, lanes], rows).astype(BF16) for h in range(PEER_HEADS)]
        c1a = [jnp.broadcast_to(c1_ref[h, pl.ds(a, 1), :][:, lanes], rows).astype(BF16) for h in range(PEER_HEADS)]
        out = []
        for r in range(N_KEYS // BF16_ROWS):
            bs = slice(r * BF16_ROWS, (r + 1) * BF16_ROWS)
            acc = None
            for h in range(PEER_HEADS):
                wh = jnp.where(r2b_ref[h, bs, lanes] < cnt[h], e2b_ref[h, bs, lanes], 0.0) * c1a[h]
                acc = wh if acc is None else acc + wh
            out.append(acc)
        return jnp.concatenate(out, axis=0).astype(F32)

    def finish_piece(cur, g, c):
        rows = slice(g * N_KEYS, (g + 1) * N_KEYS)
        lanes = slice(c * LANES, (c + 1) * LANES)
        x = act_ref[rows, lanes]
        x = 0.5 * x * (1.0 + lax.erf(x * (2.0 ** -0.5)))
        cur[lanes, rows] = (x * select_weights(g, c)).T.astype(BF16)

    def step(cur, prev):
        if cur is not None:
            act_ref[...] = jnp.dot(u_ref[...], xt_ref[...], preferred_element_type=F32)
        for n in range(n_vchunks):
            cs = slice(n * col_w, (n + 1) * col_w)
            y_ref[:, cs] += jnp.dot(prev[...], v_ref[:, cs], preferred_element_type=F32)
            if cur is not None:
                for g, c in pieces[n::n_vchunks]:
                    finish_piece(cur, g, c)

    @pl.when(jnp.logical_and(j % 2 == 0, j < n_blocks))
    def _():
        step(coef_a, coef_b)

    @pl.when(jnp.logical_and(j % 2 == 1, j < n_blocks))
    def _():
        step(coef_b, coef_a)

    @pl.when(j == n_blocks)
    def _():
        step(None, coef_a if n_blocks % 2 else coef_b)
        o_ref[...] = y_ref[...].astype(o_ref.dtype)


def _peer_mix(xt, u, v, cnt, c1, r2, e2):
    d, m = xt.shape
    ne = v.shape[0]
    tm = _pick(m, (512, 256, 128))
    te = 512
    assert m % tm == 0 and tm % LANES == 0 and ne % te == 0 and te % N_KEYS == 0
    nj = ne // te
    once = pl.Buffered(1)
    sel_spec = pl.BlockSpec((PEER_HEADS, N_KEYS, tm), lambda i, j: (0, 0, i), pipeline_mode=once)
    est = (tm * d * 2 + 2 * 2 * te * d * 2 + 4 * PEER_HEADS * N_KEYS * tm * 4
           + 2 * tm * d * 4 + 2 * tm * te * 2 + 6 * te * tm * 4)
    return pl.pallas_call(
        functools.partial(_peer_mix_kernel, te=te, n_blocks=nj),
        grid=(m // tm, nj + 1),
        in_specs=[
            pl.BlockSpec((d, tm), lambda i, j: (0, i), pipeline_mode=once),
            pl.BlockSpec((te, d), lambda i, j: (jnp.minimum(j, nj - 1), 0)),
            pl.BlockSpec((te, d), lambda i, j: (jnp.maximum(j - 1, 0), 0)),
            sel_spec, sel_spec, sel_spec, sel_spec,
        ],
        out_specs=pl.BlockSpec((tm, d), lambda i, j: (i, 0)),
        out_shape=jax.ShapeDtypeStruct((m, d), BF16),
        scratch_shapes=[pltpu.VMEM((tm, d), F32), pltpu.VMEM((tm, te), BF16), pltpu.VMEM((tm, te), BF16),
                        pltpu.VMEM((te, tm), F32),
                        pltpu.VMEM((PEER_HEADS, N_KEYS, tm), BF16), pltpu.VMEM((PEER_HEADS, N_KEYS, tm), BF16)],
        compiler_params=_params(("parallel", "arbitrary"), est),
        name="peer_mix",
    )(xt, u, v, cnt, c1, r2, e2)


def _rope_tables(pos, batch):
    inv = ROPE_THETA ** (-jnp.arange(0, HD_DIFF, 2, dtype=F32) / HD_DIFF)
    ang = pos.astype(F32)[:, None] * inv[None, :]
    ang = jnp.concatenate([ang, ang], axis=-1)
    sign = jnp.where(jnp.arange(HD_DIFF) < HD_DIFF // 2, -1.0, 1.0).astype(F32)
    cos = jnp.tile(jnp.cos(ang), (batch, 1))
    sin_signed = jnp.tile(jnp.sin(ang) * sign[None, :], (batch, 1))
    return cos, sin_signed


def _prep_weights(w_in, w_gate2, b_gate, lam_q1, lam_k1, lam_q2, lam_k2, diff_norm_g, gla_norm_g, w_out,
                  ln1_g, ln1_b, ln2_g, ln2_b, peer_wq, peer_keys1, peer_keys2, peer_u, peer_v):
    wz = jnp.pad(w_in[:, IN_GATE:IN_GATE + GATE_RANK], ((0, 0), (0, LANES - GATE_RANK)))
    return dict(
        w_in=w_in.astype(BF16),
        wz=wz.astype(BF16),
        w2=jnp.pad(w_gate2, ((0, LANES - GATE_RANK), (0, 0))).astype(BF16),
        bg=b_gate.reshape(1, GLA_QK),
        lam=jnp.stack([lam_q1, lam_k1, lam_q2, lam_k2]),
        dng=diff_norm_g.reshape(1, 2 * HD_DIFF),
        gng=gla_norm_g.reshape(1, GLA_DV),
        wo_a=w_out[:DIFF_W].astype(BF16),
        wo_b=w_out[DIFF_W:].astype(BF16),
        ln1_g=ln1_g.reshape(1, -1), ln1_b=ln1_b.reshape(1, -1),
        ln2_g=ln2_g.reshape(1, -1), ln2_b=ln2_b.reshape(1, -1),
        pwq=peer_wq.astype(BF16),
        pk1=peer_keys1.astype(BF16), pk2=peer_keys2.astype(BF16),
        pu=peer_u.astype(BF16), pv=peer_v.astype(BF16),
    )


def _layer(x, pos, past_k, past_v, s0, chunk, lam_init, alpha, w):
    batch, seq, d = x.shape
    m = batch * seq
    x2d = x.reshape(m, d)
    xb = x2d.astype(BF16)
    cos, sin_signed = _rope_tables(pos, batch)

    proj = [(xb, w["w_in"])]
    (q,) = _matmul(proj, _epi_q, [BF16], (cos, sin_signed), tn=1024, name="proj_q", cols=(0, DIFF_W))
    k32, kb = _matmul(proj, _epi_k, [F32, BF16], (cos, sin_signed), tn=1024, name="proj_k", cols=(DIFF_W, DIFF_W))
    v32, vb = _matmul(proj, _epi_f32_bf16, [F32, BF16], tn=1024, name="proj_v", cols=(2 * DIFF_W, DIFF_W))
    (gp,) = _matmul(proj, _epi_f32, [F32], tn=1024, name="proj_gla", cols=(IN_GLA, IN_GATE - IN_GLA))
    (gz,) = _matmul([(xb, w["wz"])], _epi_f32, [F32], tn=LANES, name="proj_gate")

    new_k = k32.reshape(batch, seq, H_DIFF, 2 * HD_DIFF)
    new_v = v32.reshape(batch, seq, H_DIFF, 2 * HD_DIFF)
    if past_k is None:
        o_diff = _attn_prompt(q, kb, vb, w["lam"], w["dng"], batch, seq, lam_init)
    else:
        o_diff = _attn_cached(q, new_k, new_v, past_k, past_v, w["lam"], w["dng"], batch, seq, lam_init)
    o_gla, s_new = _gla(gp, gz, w["w2"], w["bg"], w["gng"], s0, batch, seq, chunk)

    (mix,) = _matmul([(o_diff, w["wo_a"]), (o_gla, w["wo_b"])], _epi_f32, [F32], tn=1024, name="out_proj")
    x1, x1b, x1t = _res_layernorm(x2d, mix, w["ln1_g"], w["ln1_b"], alpha, [F32, BF16], [BF16])

    (pq,) = _matmul([(x1b, w["pwq"])], _epi_f32, [F32], tn=1024, name="peer_query")
    cnt, c1, r2, e2 = _peer_select(pq, w["pk1"], w["pk2"])
    y = _peer_mix(x1t, w["pu"], w["pv"], cnt, c1, r2, e2)
    (x2,) = _res_layernorm(x1, y, w["ln2_g"], w["ln2_b"], alpha, [F32])
    return x2.reshape(batch, seq, d), new_k, new_v, s_new


def kernel(x_prompt, x_sample, cache_diff_k, cache_diff_v, state_gla, w_in, w_gate2, b_gate, lam_q1, lam_k1, lam_q2, lam_k2, diff_norm_g, gla_norm_g, w_out, ln1_g, ln1_b, ln2_g, ln2_b, peer_wq, peer_keys1, peer_keys2, peer_u, peer_v):
    depth = w_in.shape[0]
    tp = x_prompt.shape[1]
    ts = x_sample.shape[1]
    past = cache_diff_k.shape[2]
    alpha = (2 * depth) ** 0.25
    pos_p = jnp.arange(tp, dtype=jnp.int32)
    pos_s = past + jnp.arange(ts, dtype=jnp.int32)
    y_p, y_s = x_prompt, x_sample
    outs = [[] for _ in range(6)]
    for l in range(depth):
        lam_init = 0.8 - 0.6 * math.exp(-0.3 * l)
        w = _prep_weights(w_in[l], w_gate2[l], b_gate[l], lam_q1[l], lam_k1[l], lam_q2[l], lam_k2[l],
                          diff_norm_g[l], gla_norm_g[l], w_out[l], ln1_g[l], ln1_b[l], ln2_g[l], ln2_b[l],
                          peer_wq[l], peer_keys1[l], peer_keys2[l], peer_u[l], peer_v[l])
        y_p, kp, vp, sp = _layer(y_p, pos_p, None, None, None, CHUNK, lam_init, alpha, w)
        y_s, k_s, v_s, s_s = _layer(y_s, pos_s, cache_diff_k[l], cache_diff_v[l], state_gla[l],
                                    ts, lam_init, alpha, w)
        for lst, val in zip(outs, (kp, vp, sp, k_s, v_s, s_s)):
            lst.append(val)
    return (y_p, y_s) + tuple(jnp.stack(o) for o in outs)
```

```python
import functools
import math

import jax
import jax.numpy as jnp
from jax import lax
from jax.experimental import pallas as pl
from jax.experimental.pallas import tpu as pltpu

CHUNK = 64
HD_DIFF = 128
H_DIFF = 8
H_GLA = 4
GLA_DK = 256
GLA_DV = 512
GATE_RANK = 16
GLA_TAU = 16.0
ROPE_THETA = 10000.0
N_KEYS = 128
PEER_HEADS = 8
PEER_HALF = 128
PEER_TOPK = 16
EPS = 1e-5
NEG_INF = -1e30
ROW_BLOCK = 256

DIFF_W = H_DIFF * 2 * HD_DIFF
GLA_QK = H_GLA * GLA_DK
GLA_V = H_GLA * GLA_DV
IN_GLA = 3 * DIFF_W
IN_GATE = IN_GLA + 2 * GLA_QK + 2 * GLA_V

LANES = 128
SUBLANES = 8
BF16_ROWS = 2 * SUBLANES
VMEM_CAP = 60000 * 1024

BF16 = jnp.bfloat16
F32 = jnp.float32

NT_DIMS = (((1,), (1,)), ((), ()))
TN_DIMS = (((0,), (0,)), ((), ()))


def _pick(n, cands):
    for c in cands:
        if n % c == 0:
            return c
    return n


def _params(sem, est_bytes):
    limit = int(min(VMEM_CAP, max(32 * 1024 * 1024, est_bytes * 5 // 4)))
    return pltpu.CompilerParams(dimension_semantics=sem, vmem_limit_bytes=limit)


def _mm_kernel(*refs, n_pairs, n_extra, epilogue):
    a_refs = refs[:n_pairs]
    b_refs = refs[n_pairs:2 * n_pairs]
    extras = refs[2 * n_pairs:2 * n_pairs + n_extra]
    outs = refs[2 * n_pairs + n_extra:]
    acc = jnp.dot(a_refs[0][...], b_refs[0][...], preferred_element_type=F32)
    for a, b in zip(a_refs[1:], b_refs[1:]):
        acc = acc + jnp.dot(a[...], b[...], preferred_element_type=F32)
    epilogue(acc, extras, outs)


def _matmul(pairs, epilogue, out_dtypes, row_extras=(), tn=512, name="matmul", cols=None):
    m = pairs[0][0].shape[0]
    c0, n = cols if cols is not None else (0, pairs[0][1].shape[1])
    tm = _pick(m, (1024, 512, 256, 128, 64, 32, 16, 8))
    tn = min(tn, n)
    assert n % tn == 0 and c0 % tn == 0
    jb = c0 // tn
    in_specs = []
    est = 0
    for a, _ in pairs:
        in_specs.append(pl.BlockSpec((tm, a.shape[1]), lambda i, j: (i, 0)))
        est += 2 * tm * a.shape[1] * a.dtype.itemsize
    for _, b in pairs:
        in_specs.append(pl.BlockSpec((b.shape[0], tn), lambda i, j: (0, jb + j)))
        est += 2 * b.shape[0] * tn * b.dtype.itemsize
    for e in row_extras:
        in_specs.append(pl.BlockSpec((tm, e.shape[1]), lambda i, j: (i, 0)))
        est += 2 * tm * e.shape[1] * 4
    out_specs = [pl.BlockSpec((tm, tn), lambda i, j: (i, j)) for _ in out_dtypes]
    out_shape = [jax.ShapeDtypeStruct((m, n), d) for d in out_dtypes]
    est += sum(2 * tm * tn * jnp.dtype(d).itemsize for d in out_dtypes) + 3 * tm * tn * 4
    kern = functools.partial(_mm_kernel, n_pairs=len(pairs), n_extra=len(row_extras), epilogue=epilogue)
    return pl.pallas_call(
        kern,
        grid=(m // tm, n // tn),
        in_specs=in_specs,
        out_specs=out_specs,
        out_shape=out_shape,
        compiler_params=_params(("parallel", "arbitrary"), est),
        name=name,
    )(*[a for a, _ in pairs], *[b for _, b in pairs], *row_extras)


def _rope_tile(acc, cos, sin_signed):
    outs = []
    for g in range(acc.shape[1] // LANES):
        xg = acc[:, g * LANES:(g + 1) * LANES]
        outs.append(xg * cos + pltpu.roll(xg, LANES // 2, 1) * sin_signed)
    return outs[0] if len(outs) == 1 else jnp.concatenate(outs, axis=1)


def _epi_q(acc, extras, outs):
    r = _rope_tile(acc, extras[0][...], extras[1][...])
    outs[0][...] = (r * (HD_DIFF ** -0.5 * math.log2(math.e))).astype(BF16)


def _epi_k(acc, extras, outs):
    r = _rope_tile(acc, extras[0][...], extras[1][...])
    outs[0][...] = r
    outs[1][...] = r.astype(BF16)


def _epi_f32_bf16(acc, extras, outs):
    outs[0][...] = acc
    outs[1][...] = acc.astype(BF16)


def _epi_f32(acc, extras, outs):
    outs[0][...] = acc


def _lambda_value(lam_ref, lam_init):
    lq1 = lam_ref[0:1, :]
    lk1 = lam_ref[1:2, :]
    lq2 = lam_ref[2:3, :]
    lk2 = lam_ref[3:4, :]
    return (jnp.exp(jnp.sum(lq1 * lk1, axis=1, keepdims=True))
            - jnp.exp(jnp.sum(lq2 * lk2, axis=1, keepdims=True)) + lam_init)


def _diff_finalize(o1, o2, lam_ref, g_ref, lam_init):
    lam = _lambda_value(lam_ref, lam_init)
    o = o1 - lam * o2
    ms = jnp.mean(o * o, axis=1, keepdims=True)
    return (o * lax.rsqrt(ms + EPS)) * g_ref[...] * (1.0 - lam_init)


def _attn_prompt_kernel(qi_ref, kj_ref, q_ref, k_ref, v_ref, lam_ref, g_ref, o_ref,
                        m1, l1, a1, m2, l2, a2, *, tq, lam_init):
    p = pl.program_id(2)
    i = qi_ref[p]
    j = kj_ref[p]
    half_t = tq // 2

    @pl.when(j == 0)
    def _():
        m1[...] = jnp.full_like(m1, NEG_INF)
        m2[...] = jnp.full_like(m2, NEG_INF)
        l1[...] = jnp.zeros_like(l1)
        l2[...] = jnp.zeros_like(l2)
        a1[...] = jnp.zeros_like(a1)
        a2[...] = jnp.zeros_like(a2)

    def update(r0, nr, nc, masked):
        if nr > ROW_BLOCK:
            for rr in range(r0, r0 + nr, ROW_BLOCK):
                update(rr, ROW_BLOCK, nc, masked)
            return
        rows = slice(r0, r0 + nr)
        q = q_ref[rows, :]
        k = k_ref[0:nc, :]
        v = v_ref[0:nc, :]
        if masked:
            qc = (r0 + lax.broadcasted_iota(jnp.int32, (nr, nc), 0)) // CHUNK
            kc = lax.broadcasted_iota(jnp.int32, (nr, nc), 1) // CHUNK
            mask = qc >= kc
        for half, (m_ref, l_ref, a_ref) in enumerate(((m1, l1, a1), (m2, l2, a2))):
            qh = q[:, half * HD_DIFF:(half + 1) * HD_DIFF]
            kh = k[:, half * HD_DIFF:(half + 1) * HD_DIFF]
            s = lax.dot_general(qh, kh, NT_DIMS, preferred_element_type=F32)
            if masked:
                s = jnp.where(mask, s, NEG_INF)
            m_old = m_ref[rows, :]
            m_new = jnp.maximum(m_old, jnp.max(s, axis=1, keepdims=True))
            alpha = jnp.exp2(m_old - m_new)
            pr = jnp.exp2(s - m_new)
            l_ref[rows, :] = alpha * l_ref[rows, :] + jnp.sum(pr, axis=1, keepdims=True)
            a_ref[rows, :] = alpha * a_ref[rows, :] + jnp.dot(pr.astype(BF16), v, preferred_element_type=F32)
            m_ref[rows, :] = m_new

    @pl.when(j < i)
    def _():
        update(0, tq, tq, False)

    @pl.when(j == i)
    def _():
        if half_t % CHUNK == 0:
            update(0, half_t, half_t, True)
            update(half_t, half_t, tq, True)
        else:
            update(0, tq, tq, True)
        o = _diff_finalize(a1[...] / l1[...], a2[...] / l2[...], lam_ref, g_ref, lam_init)
        o_ref[...] = o.astype(o_ref.dtype)


def _attn_prompt(q, k, v, lam_vecs, g, batch, seq, lam_init):
    tq = _pick(seq, (2048, 1024, 512, 256, 128, 64))
    nq = seq // tq
    hw = 2 * HD_DIFF
    pairs =[(i, j) for i in range(nq) for j in range(i + 1)]
    qi = jnp.asarray([p[0] for p in pairs], jnp.int32)
    kj = jnp.asarray([p[1] for p in pairs], jnp.int32)

    def q_map(b, h, p, qi_ref, kj_ref):
        return (b * nq + qi_ref[p], h)

    def kv_map(b, h, p, qi_ref, kj_ref):
        return (b * nq + kj_ref[p], h)

    kern = functools.partial(_attn_prompt_kernel, tq=tq, lam_init=lam_init)
    est = 2 * 4 * tq * hw * 2 + 2 * tq * hw * 4 + 4 * tq * LANES * 4 + 8 * tq * tq * 4
    return pl.pallas_call(
        kern,
        grid_spec=pltpu.PrefetchScalarGridSpec(
            num_scalar_prefetch=2,
            grid=(batch, H_DIFF, len(pairs)),
            in_specs=[
                pl.BlockSpec((tq, hw), q_map),
                pl.BlockSpec((tq, hw), kv_map),
                pl.BlockSpec((tq, hw), kv_map),
                pl.BlockSpec((4, HD_DIFF), lambda b, h, p, qi_ref, kj_ref: (0, 0)),
                pl.BlockSpec((1, hw), lambda b, h, p, qi_ref, kj_ref: (0, 0)),
            ],
            out_specs=pl.BlockSpec((tq, hw), q_map),
            scratch_shapes=[
                pltpu.VMEM((tq, 1), F32), pltpu.VMEM((tq, 1), F32), pltpu.VMEM((tq, hw), F32),
                pltpu.VMEM((tq, 1), F32), pltpu.VMEM((tq, 1), F32), pltpu.VMEM((tq, hw), F32),
            ],
        ),
        out_shape=jax.ShapeDtypeStruct((batch * seq, DIFF_W), BF16),
        compiler_params=_params(("parallel", "parallel", "arbitrary"), est),
        name="diff_attn_prompt",
    )(qi, kj, q, k, v, lam_vecs, g)


def _attn_cached_kernel(q_ref, kn_ref, vn_ref, kc_ref, vc_ref, bc_ref, bn_ref, lam_ref, g_ref, o_ref,
                        *, past, lam_init):
    ts = q_ref.shape[0]
    hw = 2 * HD_DIFF
    kc = kc_ref[0].reshape(past * H_DIFF, hw).astype(BF16)
    vc = vc_ref[0].reshape(past * H_DIFF, hw).astype(BF16)
    kn = kn_ref[0].reshape(ts * H_DIFF, hw).astype(BF16)
    vn = vn_ref[0].reshape(ts * H_DIFF, hw).astype(BF16)
    halves = []
    for half in range(2):
        sl = slice(half * HD_DIFF, (half + 1) * HD_DIFF)
        qh = jnp.concatenate([q_ref[:, h * hw + half * HD_DIFF:h * hw + (half + 1) * HD_DIFF]
                              for h in range(H_DIFF)], axis=0)
        sc = lax.dot_general(kc[:, sl], qh, NT_DIMS, preferred_element_type=F32) + bc_ref[...]
        sn = lax.dot_general(kn[:, sl], qh, NT_DIMS, preferred_element_type=F32) + bn_ref[...]
        m = jnp.maximum(jnp.max(sc, axis=0, keepdims=True), jnp.max(sn, axis=0, keepdims=True))
        pc = jnp.exp2(sc - m)
        pn = jnp.exp2(sn - m)
        l = jnp.sum(pc, axis=0, keepdims=True) + jnp.sum(pn, axis=0, keepdims=True)
        acc = (jnp.dot(pc.T.astype(BF16), vc, preferred_element_type=F32)
               + jnp.dot(pn.T.astype(BF16), vn, preferred_element_type=F32))
        l_col = jnp.broadcast_to(l, (l.shape[1], l.shape[1])).T
        halves.append(acc / jnp.concatenate([l_col] * (hw // l.shape[1]), axis=1))
    for h in range(H_DIFF):
        rows = slice(h * ts, (h + 1) * ts)
        o = _diff_finalize(halves[0][rows], halves[1][rows], lam_ref, g_ref, lam_init)
        o_ref[:, h * hw:(h + 1) * hw] = o.astype(o_ref.dtype)


def _attn_cached(q, kn, vn, cache_k, cache_v, lam_vecs, g, batch, seq, lam_init):
    past = cache_k.shape[1]
    hw = 2 * HD_DIFF
    assert H_DIFF * seq == LANES
    col_head = jnp.arange(LANES) // seq
    col_chunk = (past + jnp.arange(LANES) % seq) // CHUNK

    def bias(key_pos):
        row_head = jnp.tile(jnp.arange(H_DIFF), key_pos.shape[0])
        row_chunk = jnp.repeat(key_pos, H_DIFF) // CHUNK
        ok = (row_head[:, None] == col_head[None, :]) & (col_chunk[None, :] >= row_chunk[:, None])
        return jnp.where(ok, 0.0, NEG_INF).astype(F32)

    bias_c = bias(jnp.arange(past))
    bias_n = bias(past + jnp.arange(seq))
    kern = functools.partial(_attn_cached_kernel, past=past, lam_init=lam_init)
    est = 2 * past * DIFF_W * 4 + 2 * past * DIFF_W * 2 + 8 * past * H_DIFF * LANES * 4
    once = pl.Buffered(1)
    cache_spec = pl.BlockSpec((1, past, H_DIFF, hw), lambda b: (b, 0, 0, 0), pipeline_mode=once)
    new_spec = pl.BlockSpec((1, seq, H_DIFF, hw), lambda b: (b, 0, 0, 0))
    row_spec = pl.BlockSpec((seq, DIFF_W), lambda b: (b, 0))
    return pl.pallas_call(
        kern,
        grid=(batch,),
        in_specs=[
            row_spec, new_spec, new_spec, cache_spec, cache_spec,
            pl.BlockSpec(bias_c.shape, lambda b: (0, 0), pipeline_mode=once),
            pl.BlockSpec(bias_n.shape, lambda b: (0, 0)),
            pl.BlockSpec((4, HD_DIFF), lambda b: (0, 0)),
            pl.BlockSpec((1, hw), lambda b: (0, 0)),
        ],
        out_specs=row_spec,
        out_shape=jax.ShapeDtypeStruct((batch * seq, DIFF_W), BF16),
        compiler_params=_params(("parallel",), est),
        name="diff_attn_cached",
    )(q, kn, vn, cache_k, cache_v, bias_c, bias_n, lam_vecs, g)


def _split3(x):
    hi = x.astype(BF16)
    r1 = x - hi.astype(F32)
    mid = r1.astype(BF16)
    lo = (r1 - mid.astype(F32)).astype(BF16)
    return hi, mid, lo


def _gla_kernel(*refs, chunk, n_chunks, has_s0):
    if has_s0:
        q_ref, k_ref, v_ref, r_ref, z_ref, w2_ref, bg_ref, gn_ref, s0_ref, o_ref, s_ref, st = refs
    else:
        q_ref, k_ref, v_ref, r_ref, z_ref, w2_ref, bg_ref, gn_ref, o_ref, s_ref, st = refs
    i = pl.program_id(2)

    @pl.when(i == 0)
    def _():
        if has_s0:
            st[...] = s0_ref[0, 0].T
        else:
            st[...] = jnp.zeros_like(st)

    scale = GLA_DK ** -0.5
    zz = jnp.dot(z_ref[...].astype(BF16), w2_ref[...], preferred_element_type=F32) + bg_ref[...]
    logf = jax.nn.log_sigmoid(zz) / GLA_TAU
    row = lax.broadcasted_iota(jnp.int32, (chunk, chunk), 0)
    col = lax.broadcasted_iota(jnp.int32, (chunk, chunk), 1)
    causal = row >= col
    tri = jnp.where(causal, 1.0, 0.0).astype(BF16)
    for c in range(n_chunks):
        rows = slice(c * chunk, (c + 1) * chunk)
        g = logf[rows]
        bcum = sum(jnp.dot(tri, part, preferred_element_type=F32) for part in _split3(g))
        b_last = bcum[chunk - 1:chunk, :]
        qc = q_ref[rows, :]
        kc = k_ref[rows, :]
        vc = v_ref[rows, :].astype(BF16)
        q_in = (qc * jnp.exp(bcum) * scale).astype(BF16)
        k_in = (kc * jnp.exp(-bcum)).astype(BF16)
        k_end = (kc * jnp.exp(b_last - bcum)).astype(BF16)
        a = lax.dot_general(q_in, k_in, NT_DIMS, preferred_element_type=F32)
        a = jnp.where(causal, a, 0.0)
        s_old = st[...]
        o = (jnp.dot(a.astype(BF16), vc, preferred_element_type=F32)
             + lax.dot_general(q_in, s_old.astype(BF16), NT_DIMS, preferred_element_type=F32))
        st[...] = s_old * jnp.exp(b_last) + lax.dot_general(vc, k_end, TN_DIMS, preferred_element_type=F32)
        ms = jnp.mean(o * o, axis=1, keepdims=True)
        on = (o * lax.rsqrt(ms + EPS)) * gn_ref[...]
        gate = r_ref[rows, :]
        o_ref[rows, :] = (on * (gate * jax.nn.sigmoid(gate))).astype(o_ref.dtype)

    @pl.when(i == pl.num_programs(2) - 1)
    def _():
        s_ref[0, 0] = st[...].T


def _gla(gp, gz, w2, bg, gn, s0, batch, seq, chunk):
    tb = chunk * max(1, min(16, seq // chunk))
    while seq % tb:
        tb -= chunk
    nb = seq // tb
    kq = GLA_QK // GLA_DK
    kv = (2 * GLA_QK) // GLA_DV
    kr = kv + GLA_V // GLA_DV
    in_specs = [
        pl.BlockSpec((tb, GLA_DK), lambda b, h, i: (b * nb + i, h)),
        pl.BlockSpec((tb, GLA_DK), lambda b, h, i: (b * nb + i, kq + h)),
        pl.BlockSpec((tb, GLA_DV), lambda b, h, i: (b * nb + i, kv + h)),
        pl.BlockSpec((tb, GLA_DV), lambda b, h, i: (b * nb + i, kr + h)),
        pl.BlockSpec((tb, LANES), lambda b, h, i: (b * nb + i, 0)),
        pl.BlockSpec((LANES, GLA_DK), lambda b, h, i: (0, h)),
        pl.BlockSpec((1, GLA_DK), lambda b, h, i: (0, h)),
        pl.BlockSpec((1, GLA_DV), lambda b, h, i: (0, 0)),
    ]
    args = [gp, gp, gp, gp, gz, w2, bg, gn]
    if s0 is not None:
        in_specs.append(pl.BlockSpec((1, 1, GLA_DK, GLA_DV), lambda b, h, i: (b, h, 0, 0)))
        args.append(s0)
    kern = functools.partial(_gla_kernel, chunk=chunk, n_chunks=tb // chunk, has_s0=s0 is not None)
    est = 2 * tb * (2 * GLA_DK + 2 * GLA_DV + LANES) * 4 + 2 * tb * GLA_DV * 2 + 7 * GLA_DK * GLA_DV * 4
    return pl.pallas_call(
        kern,
        grid=(batch, H_GLA, nb),
        in_specs=in_specs,
        out_specs=[
            pl.BlockSpec((tb, GLA_DV), lambda b, h, i: (b * nb + i, h)),
            pl.BlockSpec((1, 1, GLA_DK, GLA_DV), lambda b, h, i: (b, h, 0, 0)),
        ],
        out_shape=[
            jax.ShapeDtypeStruct((batch * seq, GLA_V), BF16),
            jax.ShapeDtypeStruct((batch, H_GLA, GLA_DK, GLA_DV), F32),
        ],
        scratch_shapes=[pltpu.VMEM((GLA_DV, GLA_DK), F32)],
        compiler_params=_params(("parallel", "parallel", "arbitrary"), est),
        name="gla",
    )(*args)


def _ln_kernel(x_ref, y_ref, g_ref, b_ref, *outs, alpha, n_plain):
    h = alpha * x_ref[...] + y_ref[...].astype(F32)
    mu = jnp.mean(h, axis=1, keepdims=True)
    d = h - mu
    var = jnp.mean(d * d, axis=1, keepdims=True)
    r = (d * lax.rsqrt(var + EPS)) * g_ref[...] + b_ref[...]
    for o in outs[:n_plain]:
        o[...] = r.astype(o.dtype)
    for o in outs[n_plain:]:
        o[...] = r.T.astype(o.dtype)


def _res_layernorm(x, y, g, b, alpha, out_dtypes, transposed_dtypes=()):
    m, d = x.shape
    tm = _pick(m, (256, 128)) if transposed_dtypes else _pick(m, (256, 128, 64, 32, 16, 8))
    assert m % tm == 0
    sizes = [jnp.dtype(t).itemsize for t in (*out_dtypes, *transposed_dtypes)]
    est = 2 * tm * d * (x.dtype.itemsize + y.dtype.itemsize + sum(sizes)) + 5 * tm * d * 4
    return pl.pallas_call(
        functools.partial(_ln_kernel, alpha=alpha, n_plain=len(out_dtypes)),
        grid=(m // tm,),
        in_specs=[
            pl.BlockSpec((tm, d), lambda i: (i, 0)),
            pl.BlockSpec((tm, d), lambda i: (i, 0)),
            pl.BlockSpec((1, d), lambda i: (0, 0)),
            pl.BlockSpec((1, d), lambda i: (0, 0)),
        ],
        out_specs=[pl.BlockSpec((tm, d), lambda i: (i, 0)) for _ in out_dtypes]
        + [pl.BlockSpec((d, tm), lambda i: (0, i)) for _ in transposed_dtypes],
        out_shape=[jax.ShapeDtypeStruct((m, d), t) for t in out_dtypes]
        + [jax.ShapeDtypeStruct((d, m), t) for t in transposed_dtypes],
        compiler_params=_params(("parallel",), est),
        name="res_layernorm",
    )(x, y, g, b)


NOT_RANKED = float(N_KEYS - 1)


def _top_values(s, count):
    vals = []
    cur = s
    for t in range(count):
        m = jnp.max(cur, axis=0, keepdims=True)
        vals.append(m)
        if t + 1 < count:
            cur = jnp.where(cur == m, -jnp.inf, cur)
    return vals


def _peer_select_kernel(q_ref, k1_ref, k2_ref, cnt_ref, c1_ref, r2_ref, e2_ref, cand_ref):
    pairs = [(i, j) for i in range(PEER_TOPK) for j in range(PEER_TOPK) if (i + 1) * (j + 1) <= PEER_TOPK]
    for h in range(PEER_HEADS):
        base = h * 2 * PEER_HALF
        q1 = q_ref[:, base:base + PEER_HALF].astype(BF16)
        q2 = q_ref[:, base + PEER_HALF:base + 2 * PEER_HALF].astype(BF16)
        s1 = lax.dot_general(k1_ref[h], q1, NT_DIMS, preferred_element_type=F32)
        s2 = lax.dot_general(k2_ref[h], q2, NT_DIMS, preferred_element_type=F32)
        t1 = _top_values(s1, PEER_TOPK)
        t2 = _top_values(s2, PEER_TOPK)
        cand_ref[...] = jnp.full_like(cand_ref, -jnp.inf)
        for r, (i, j) in enumerate(pairs):
            cand_ref[r:r + 1, :] = t1[i] + t2[j]
        best = _top_values(cand_ref[...], PEER_TOPK)
        tau = best[PEER_TOPK - 1]
        z = sum(jnp.exp(bv - best[0]) for bv in best)
        counts = [sum(jnp.where(t1[i] + t2[j] >= tau, 1.0, 0.0) for j in range(PEER_TOPK) if (i, j) in pairs)
                  for i in range(PEER_TOPK)]
        cnt = jnp.zeros_like(s1)
        rank2 = jnp.full_like(s2, NOT_RANKED)
        for i in range(PEER_TOPK):
            cnt = jnp.where(s1 == t1[i], counts[i], cnt)
            rank2 = jnp.where(s2 == t2[i], float(i), rank2)
        cnt_ref[h] = cnt
        c1_ref[h] = jnp.exp(s1 - t1[0]) / z
        r2_ref[h] = rank2
        e2_ref[h] = jnp.exp(s2 - t2[0])


def _peer_select(q, keys1, keys2):
    m = q.shape[0]
    tm = _pick(m, (256, 128))
    assert m % tm == 0
    n_pairs = sum((i + 1) * (j + 1) <= PEER_TOPK for i in range(PEER_TOPK) for j in range(PEER_TOPK))
    cand_rows = -(-n_pairs // 8) * 8
    spec = pl.BlockSpec((PEER_HEADS, N_KEYS, tm), lambda i: (0, 0, i))
    est = 2 * tm * q.shape[1] * 4 + 2 * 4 * PEER_HEADS * N_KEYS * tm * 4 + (8 << 20)
    return pl.pallas_call(
        _peer_select_kernel,
        grid=(m // tm,),
        in_specs=[
            pl.BlockSpec((tm, q.shape[1]), lambda i: (i, 0)),
            pl.BlockSpec(keys1.shape, lambda i: (0, 0, 0)),
            pl.BlockSpec(keys2.shape, lambda i: (0, 0, 0)),
        ],
        out_specs=[spec, spec, spec, spec],
        out_shape=[jax.ShapeDtypeStruct((PEER_HEADS, N_KEYS, m), F32)] * 4,
        scratch_shapes=[pltpu.VMEM((cand_rows, tm), F32)],
        compiler_params=_params(("parallel",), est),
        name="peer_select",
    )(q, keys1, keys2)


def _peer_mix_kernel(xt_ref, u_ref, v_ref, cnt_ref, c1_ref, r2_ref, e2_ref, o_ref,
                     y_ref, coef_a, act_ref, r2b_ref, e2b_ref, *, te, n_blocks):
    j = pl.program_id(1)
    d, tm = xt_ref.shape
    groups = te // N_KEYS
    jb = j
    pieces = [(g, c) for c in range(tm // LANES) for g in range(groups)]
    col_w = 2 * LANES
    n_vchunks = d // col_w

    @pl.when(j == 0)
    def _():
        y_ref[...] = jnp.zeros_like(y_ref)
        r2b_ref[...] = r2_ref[...].astype(BF16)
        e2b_ref[...] = e2_ref[...].astype(BF16)

    def select_weights(g, c):
        a = jb * groups + g
        lanes = slice(c * LANES, (c + 1) * LANES)
        rows = (BF16_ROWS, LANES)
        cnt = [jnp.broadcast_to(cnt_ref[h, pl.ds(a, 1), :][:, lanes], rows).astype(BF16) for h in range(PEER_HEADS)]
        c1a = [jnp.broadcast_to(c1_ref[h, pl.ds(a, 1), :][:, lanes], rows).astype(BF16) for h in range(PEER_HEADS)]
        out = []
        for r in range(N_KEYS // BF16_ROWS):
            bs = slice(r * BF16_ROWS, (r + 1) * BF16_ROWS)
            acc = None
            for h in range(PEER_HEADS):
                wh = jnp.where(r2b_ref[h, bs, lanes] < cnt[h], e2b_ref[h, bs, lanes], 0.0) * c1a[h]
                acc = wh if acc is None else acc + wh
            out.append(acc)
        return jnp.concatenate(out, axis=0).astype(F32)

    def finish_piece(cur, g, c):
        rows = slice(g * N_KEYS, (g + 1) * N_KEYS)
        lanes = slice(c * LANES, (c + 1) * LANES)
        x = act_ref[rows, lanes]
        x = 0.5 * x * (1.0 + lax.erf(x * (2.0 ** -0.5)))
        cur[lanes, rows] = (x * select_weights(g, c)).T.astype(BF16)

    act_ref[...] = jnp.dot(u_ref[...], xt_ref[...], preferred_element_type=F32)
    for g, c in pieces:
        finish_piece(coef_a, g, c)
    for n in range(n_vchunks):
        cs = slice(n * col_w, (n + 1) * col_w)
        y_ref[:, cs] += jnp.dot(coef_a[...], v_ref[:, cs], preferred_element_type=F32)

    @pl.when(j == n_blocks - 1)
    def _():
        o_ref[...] = y_ref[...].astype(o_ref.dtype)


def _peer_mix(xt, u, v, cnt, c1, r2, e2):
    d, m = xt.shape
    ne = v.shape[0]
    tm = _pick(m, (512, 256, 128))
    te = 512
    assert m % tm == 0 and tm % LANES == 0 and ne % te == 0 and te % N_KEYS == 0
    nj = ne // te
    once = pl.Buffered(1)
    sel_spec = pl.BlockSpec((PEER_HEADS, N_KEYS, tm), lambda i, j: (0, 0, i), pipeline_mode=once)
    est = (tm * d * 2 + 2 * 2 * te * d * 2 + 4 * PEER_HEADS * N_KEYS * tm * 4
           + 2 * tm * d * 4 + 2 * tm * te * 2 + 6 * te * tm * 4)
    return pl.pallas_call(
        functools.partial(_peer_mix_kernel, te=te, n_blocks=nj),
        grid=(m // tm, nj),
        in_specs=[
            pl.BlockSpec((d, tm), lambda i, j: (0, i), pipeline_mode=once),
            pl.BlockSpec((te, d), lambda i, j: (j, 0)),
            pl.BlockSpec((te, d), lambda i, j: (j, 0)),
            sel_spec, sel_spec, sel_spec, sel_spec,
        ],
        out_specs=pl.BlockSpec((tm, d), lambda i, j: (i, 0)),
        out_shape=jax.ShapeDtypeStruct((m, d), BF16),
        scratch_shapes=[pltpu.VMEM((tm, d), F32), pltpu.VMEM((tm, te), BF16),
                        pltpu.VMEM((te, tm), F32),
                        pltpu.VMEM((PEER_HEADS, N_KEYS, tm), BF16), pltpu.VMEM((PEER_HEADS, N_KEYS, tm), BF16)],
        compiler_params=_params(("parallel", "arbitrary"), est),
        name="peer_mix",
    )(xt, u, v, cnt, c1, r2, e2)


def _rope_tables(pos, batch):
    inv = ROPE_THETA ** (-jnp.arange(0, HD_DIFF, 2, dtype=F32) / HD_DIFF)
    ang = pos.astype(F32)[:, None] * inv[None, :]
    ang = jnp.concatenate([ang, ang], axis=-1)
    sign = jnp.where(jnp.arange(HD_DIFF) < HD_DIFF // 2, -1.0, 1.0).astype(F32)
    cos = jnp.tile(jnp.cos(ang), (batch, 1))
    sin_signed = jnp.tile(jnp.sin(ang) * sign[None, :], (batch, 1))
    return cos, sin_signed


def _prep_weights(w_in, w_gate2, b_gate, lam_q1, lam_k1, lam_q2, lam_k2, diff_norm_g, gla_norm_g, w_out,
                  ln1_g, ln1_b, ln2_g, ln2_b, peer_wq, peer_keys1, peer_keys2, peer_u, peer_v):
    wz = jnp.pad(w_in[:, IN_GATE:IN_GATE + GATE_RANK], ((0, 0), (0, LANES - GATE_RANK)))
    return dict(
        w_in=w_in.astype(BF16),
        wz=wz.astype(BF16),
        w2=jnp.pad(w_gate2, ((0, LANES - GATE_RANK), (0, 0))).astype(BF16),
        bg=b_gate.reshape(1, GLA_QK),
        lam=jnp.stack([lam_q1, lam_k1, lam_q2, lam_k2]),
        dng=diff_norm_g.reshape(1, 2 * HD_DIFF),
        gng=gla_norm_g.reshape(1, GLA_DV),
        wo_a=w_out[:DIFF_W].astype(BF16),
        wo_b=w_out[DIFF_W:].astype(BF16),
        ln1_g=ln1_g.reshape(1, -1), ln1_b=ln1_b.reshape(1, -1),
        ln2_g=ln2_g.reshape(1, -1), ln2_b=ln2_b.reshape(1, -1),
        pwq=peer_wq.astype(BF16),
        pk1=peer_keys1.astype(BF16), pk2=peer_keys2.astype(BF16),
        pu=peer_u.astype(BF16), pv=peer_v.astype(BF16),
    )


def _layer(x, pos, past_k, past_v, s0, chunk, lam_init, alpha, w):
    batch, seq, d = x.shape
    m = batch * seq
    x2d = x.reshape(m, d)
    xb = x2d.astype(BF16)
    cos, sin_signed = _rope_tables(pos, batch)

    proj = [(xb, w["w_in"])]
    (q,) = _matmul(proj, _epi_q, [BF16], (cos, sin_signed), tn=1024, name="proj_q", cols=(0, DIFF_W))
    k32, kb = _matmul(proj, _epi_k, [F32, BF16], (cos, sin_signed), tn=1024, name="proj_k", cols=(DIFF_W, DIFF_W))
    v32, vb = _matmul(proj, _epi_f32_bf16, [F32, BF16], tn=1024, name="proj_v", cols=(2 * DIFF_W, DIFF_W))
    (gp,) = _matmul(proj, _epi_f32, [F32], tn=1024, name="proj_gla", cols=(IN_GLA, IN_GATE - IN_GLA))
    (gz,) = _matmul([(xb, w["wz"])], _epi_f32, [F32], tn=LANES, name="proj_gate")

    new_k = k32.reshape(batch, seq, H_DIFF, 2 * HD_DIFF)
    new_v = v32.reshape(batch, seq, H_DIFF, 2 * HD_DIFF)
    if past_k is None:
        o_diff = _attn_prompt(q, kb, vb, w["lam"], w["dng"], batch, seq, lam_init)
    else:
        o_diff = _attn_cached(q, new_k, new_v, past_k, past_v, w["lam"], w["dng"], batch, seq, lam_init)
    o_gla, s_new = _gla(gp, gz, w["w2"], w["bg"], w["gng"], s0, batch, seq, chunk)

    (mix,) = _matmul([(o_diff, w["wo_a"]), (o_gla, w["wo_b"])], _epi_f32, [F32], tn=1024, name="out_proj")
    x1, x1b, x1t = _res_layernorm(x2d, mix, w["ln1_g"], w["ln1_b"], alpha, [F32, BF16], [BF16])

    (pq,) = _matmul([(x1b, w["pwq"])], _epi_f32, [F32], tn=1024, name="peer_query")
    cnt, c1, r2, e2 = _peer_select(pq, w["pk1"], w["pk2"])
    y = _peer_mix(x1t, w["pu"], w["pv"], cnt, c1, r2, e2)
    (x2,) = _res_layernorm(x1, y, w["ln2_g"], w["ln2_b"], alpha, [F32])
    return x2.reshape(batch, seq, d), new_k, new_v, s_new


def kernel(x_prompt, x_sample, cache_diff_k, cache_diff_v, state_gla, w_in, w_gate2, b_gate, lam_q1, lam_k1, lam_q2, lam_k2, diff_norm_g, gla_norm_g, w_out, ln1_g, ln1_b, ln2_g, ln2_b, peer_wq, peer_keys1, peer_keys2, peer_u, peer_v):
    depth = w_in.shape[0]
    tp = x_prompt.shape[1]
    ts = x_sample.shape[1]
    past = cache_diff_k.shape[2]
    alpha = (2 * depth) ** 0.25
    pos_p = jnp.arange(tp, dtype=jnp.int32)
    pos_s = past + jnp.arange(ts, dtype=jnp.int32)
    y_p, y_s = x_prompt, x_sample
    outs = [[] for _ in range(6)]
    for l in range(depth):
        lam_init = 0.8 - 0.6 * math.exp(-0.3 * l)
        w = _prep_weights(w_in[l], w_gate2[l], b_gate[l], lam_q1[l], lam_k1[l], lam_q2[l], lam_k2[l],
                          diff_norm_g[l], gla_norm_g[l], w_out[l], ln1_g[l], ln1_b[l], ln2_g[l], ln2_b[l],
                          peer_wq[l], peer_keys1[l], peer_keys2[l], peer_u[l], peer_v[l])
        y_p, kp, vp, sp = _layer(y_p, pos_p, None, None, None, CHUNK, lam_init, alpha, w)
        y_s, k_s, v_s, s_s = _layer(y_s, pos_s, cache_diff_k[l], cache_diff_v[l], state_gla[l],
                                    ts, lam_init, alpha, w)
        for lst, val in zip(outs, (kp, vp, sp, k_s, v_s, s_s)):
            lst.append(val)
    return (y_p, y_s) + tuple(jnp.stack(o) for o in outs)
```
